```python
import functools
import jax, jax.numpy as jnp
from jax import lax
import numpy as np

D_MODEL = 1024
BATCH = 8
SEQ = 2048
DEPTH = 2
DEC_BATCH = 128
DEC_SEQ = 1
PAST_LEN = 16384
PAGE_SIZE = 128

M_HEADS = 4
M_WIDTH = 3 * D_MODEL // 8
M_DH = M_WIDTH // M_HEADS
R_HEADS = 4
R_WIDTH = 3 * D_MODEL // 8
R_DH = R_WIDTH // R_HEADS
H_HEADS = 4
H_WIDTH = D_MODEL // 4
H_DK = H_WIDTH // H_HEADS
H_DV = H_WIDTH // H_HEADS
D_MIX = M_WIDTH + R_WIDTH + H_WIDTH
CONV_W = 4
CHUNK = 64
ROPE_BASE = 10000.0
LN_EPS = 1e-5
HEAD_EPS = 1e-6
ALPHA = (2 * DEPTH) ** 0.25
BETA = (8 * DEPTH) ** -0.25
N_IN = 5 * M_WIDTH + 2 * M_HEADS + 4 * R_WIDTH + 4 * H_WIDTH

kernel_name = "hybrid_mlstm_retention_hgrn2_step"


def _split_points():
    sizes = [2 * M_WIDTH, M_WIDTH, M_WIDTH, M_WIDTH, 2 * M_HEADS,
             R_WIDTH, R_WIDTH, R_WIDTH, R_WIDTH,
             H_WIDTH, H_WIDTH, H_WIDTH, H_WIDTH]
    points, acc = [], 0
    for s in sizes[:-1]:
        acc += s
        points.append(acc)
    return points


def layer_norm(x, g, b):
    x32 = x.astype(jnp.float32)
    mu = jnp.mean(x32, -1, keepdims=True)
    var = jnp.mean(jnp.square(x32 - mu), -1, keepdims=True)
    y = (x32 - mu) * lax.rsqrt(var + LN_EPS) * g.astype(jnp.float32) + b.astype(jnp.float32)
    return y.astype(x.dtype)


def head_layer_norm(h, w):
    mu = jnp.mean(h, -1, keepdims=True)
    var = jnp.mean(jnp.square(h - mu), -1, keepdims=True)
    hn = (h - mu) * lax.rsqrt(var + HEAD_EPS)
    return hn.reshape(h.shape[0], h.shape[1], -1) * w.astype(jnp.float32)


def head_rms_norm(h, w):
    hn = h * lax.rsqrt(jnp.mean(jnp.square(h), -1, keepdims=True) + HEAD_EPS)
    return hn.reshape(h.shape[0], h.shape[1], -1) * w.astype(jnp.float32)


def causal_conv(x, buf, w, b):
    T = x.shape[1]
    xp = jnp.concatenate([buf.astype(x.dtype), x], axis=1)
    out = b.astype(x.dtype)
    for j in range(CONV_W):
        out = out + w[j].astype(x.dtype) * xp[:, j:j + T]
    return out, xp[:, xp.shape[1] - (CONV_W - 1):]


def rope(x, pos):
    half = x.shape[-1] // 2
    inv = ROPE_BASE ** (-jnp.arange(half, dtype=jnp.float32) / half)
    ang = pos.astype(jnp.float32)[:, None] * inv[None, :]
    cos = jnp.cos(ang)[None, :, None, :]
    sin = jnp.sin(ang)[None, :, None, :]
    x1, x2 = x[..., :half], x[..., half:]
    return jnp.concatenate([x1 * cos - x2 * sin, x1 * sin + x2 * cos], axis=-1)


def run_chunked(step, carry, xs, T):
    L = CHUNK if T % CHUNK == 0 else T
    n = T // L
    xs_c = tuple(jnp.moveaxis(a.reshape(a.shape[0], n, L, *a.shape[2:]), 1, 0) for a in xs)
    carry, ys = lax.scan(step, carry, xs_c)
    ys = jnp.moveaxis(ys, 0, 1)
    return carry, ys.reshape(ys.shape[0], T, *ys.shape[3:])


def mlstm_step(carry, inp):
    C, n, m = carry
    q, k, v, ig, lf = inp
    L = q.shape[1]
    b = jnp.swapaxes(jnp.cumsum(lf, axis=1), 1, 2)
    igT = jnp.swapaxes(ig, 1, 2)
    mask = jnp.tril(jnp.ones((L, L), dtype=bool))
    logD = jnp.where(mask, b[..., :, None] - b[..., None, :] + igT[..., None, :], -jnp.inf)
    prior = b + m[..., None]
    m_t = jnp.maximum(jnp.max(logD, axis=-1), prior)
    s = jnp.einsum('blhd,bshd->bhls', q, k) * jnp.exp(logD - m_t[..., None])
    w_prior = jnp.exp(prior - m_t)
    num = (jnp.einsum('bhls,bshd->blhd', s, v)
           + jnp.einsum('blhk,bhkv->blhv', q, C) * jnp.swapaxes(w_prior, 1, 2)[..., None])
    den = jnp.sum(s, axis=-1) + jnp.einsum('blhk,bhk->bhl', q, n) * w_prior
    denom = jnp.maximum(jnp.abs(den), jnp.exp(-m_t))
    h = num / jnp.swapaxes(denom, 1, 2)[..., None]
    m_new = m_t[..., -1]
    dec = jnp.exp(b[..., -1] + m - m_new)
    w_s = jnp.exp(b[..., -1:] - b + igT - m_new[..., None])
    C_new = dec[..., None, None] * C + jnp.einsum('bhs,bshk,bshv->bhkv', w_s, k, v)
    n_new = dec[..., None] * n + jnp.einsum('bhs,bshk->bhk', w_s, k)
    return (C_new, n_new, m_new), h


def ret_step(lg, R, inp):
    q, k, v = inp
    L = q.shape[1]
    idx = jnp.arange(L, dtype=jnp.float32)
    mask = jnp.tril(jnp.ones((L, L), dtype=bool))
    diff = jnp.where(mask, idx[:, None] - idx[None, :], 0.0)
    D = jnp.where(mask[None], jnp.exp(diff[None] * lg[:, None, None]), 0.0)
    s = jnp.einsum('blhd,bshd->bhls', q, k) * D[None]
    intra = jnp.einsum('bhls,bshv->blhv', s, v)
    inter = jnp.einsum('blhk,bhkv->blhv', q, R) * jnp.exp((idx[:, None] + 1.0) * lg[None, :])[None, :, :, None]
    k_dec = k * jnp.exp((L - 1.0 - idx)[:, None] * lg[None, :])[None, :, :, None]
    R_new = jnp.exp(L * lg)[None, :, None, None] * R + jnp.einsum('bshk,bshv->bhkv', k_dec, v)
    return R_new, intra + inter


def hgrn_step(S, inp):
    q, k, v, lf = inp
    L = q.shape[1]
    Bc = jnp.cumsum(lf, axis=1)
    mask = jnp.tril(jnp.ones((L, L), dtype=bool))[None, :, :, None, None]
    diff = jnp.where(mask, Bc[:, :, None] - Bc[:, None, :], 0.0)
    D = jnp.where(mask, jnp.exp(diff), 0.0)
    s = jnp.einsum('bthk,bshk,btshk->bhts', q, k, D)
    intra = jnp.einsum('bhts,bshv->bthv', s, v)
    inter = jnp.einsum('bthk,bhkv->bthv', q * jnp.exp(Bc), S)
    S_new = (jnp.exp(Bc[:, -1])[..., None] * S
             + jnp.einsum('bshk,bshv->bhkv', k * jnp.exp(Bc[:, -1:] - Bc), v))
    return S_new, intra + inter


def hybrid_layer(x, pos, st, w_in, conv_w, conv_b, b_mgate, m_norm_w, r_norm_w, h_norm_w, lb, w_out, ln_g, ln_b):
    f32 = jnp.float32
    C0, n0, m0, buf0, R0, S0 = st
    Bsz, T = x.shape[0], x.shape[1]
    proj = jnp.einsum('btd,dn->btn', x, w_in)
    (m_qk, m_v, m_o, m_z, m_if, r_q, r_k, r_v, r_g,
     h_f, h_i, h_q, h_g) = jnp.split(proj, _split_points(), axis=-1)

    def heads(a, H):
        return a.reshape(Bsz, T, H, -1).astype(f32)

    qk, buf_new = causal_conv(m_qk, buf0, conv_w, conv_b)
    qk = jax.nn.silu(qk)
    q_m, k_m = jnp.split(qk, 2, axis=-1)
    q_m = heads(q_m, M_HEADS)
    k_m = heads(k_m, M_HEADS) * (M_DH ** -0.5)
    v_m = heads(m_v, M_HEADS)
    gates = (m_if + b_mgate).astype(f32)
    ig = gates[..., :M_HEADS]
    lf_m = jax.nn.log_sigmoid(gates[..., M_HEADS:])
    (C1, n1, m1), h_m = run_chunked(
        mlstm_step, (C0.astype(f32), n0.astype(f32), m0.astype(f32)), (q_m, k_m, v_m, ig, lf_m), T)
    y_m = head_layer_norm(h_m, m_norm_w) * jax.nn.sigmoid(m_o.astype(f32)) * jax.nn.silu(m_z.astype(f32))

    q_r = rope(heads(r_q, R_HEADS), pos)
    k_r = rope(heads(r_k, R_HEADS), pos) * (R_DH ** -0.5)
    v_r = heads(r_v, R_HEADS)
    lg = jnp.log1p(-jnp.exp2(-5.0 - jnp.arange(R_HEADS, dtype=f32)))
    R1, h_r = run_chunked(functools.partial(ret_step, lg), R0.astype(f32), (q_r, k_r, v_r), T)
    y_r = head_layer_norm(h_r, r_norm_w) * jax.nn.silu(r_g.astype(f32))

    fpre = heads(h_f, H_HEADS)
    lbh = lb.reshape(H_HEADS, H_DK)
    lf_h = jnp.logaddexp(jnp.log(lbh), jnp.log1p(-lbh) + jax.nn.log_sigmoid(fpre))
    k_h = (1.0 - lbh) * jax.nn.sigmoid(-fpre)
    S1, h_h = run_chunked(hgrn_step, S0.astype(f32), (heads(h_q, H_HEADS), k_h, heads(h_i, H_HEADS), lf_h), T)
    y_h = head_rms_norm(h_h, h_norm_w) * jax.nn.silu(h_g.astype(f32))

    mix = jnp.concatenate([y_m, y_r, y_h], axis=-1).astype(x.dtype)
    out = jnp.einsum('btm,md->btd', mix, w_out)
    x_new = layer_norm(ALPHA * x + out, ln_g, ln_b)
    return x_new, (C1, n1, m1, buf_new, R1, S1)


def _zero_state(Bsz, dtype):
    f32 = jnp.float32
    return (jnp.zeros((Bsz, M_HEADS, M_DH, M_DH), f32),
            jnp.zeros((Bsz, M_HEADS, M_DH), f32),
            jnp.zeros((Bsz, M_HEADS), f32),
            jnp.zeros((Bsz, CONV_W - 1, 2 * M_WIDTH), dtype),
            jnp.zeros((Bsz, R_HEADS, R_DH, R_DH), f32),
            jnp.zeros((Bsz, H_HEADS, H_DK, H_DV), f32))


def setup_inputs(seed: int = 0) -> dict:
    key = jax.random.key(seed)
    ks = jax.random.split(key, 20)
    nrm = jax.random.normal
    f_bias = jnp.linspace(3.0, 6.0, M_HEADS, dtype=jnp.float32)[None, :] + 0.1 * nrm(ks[10], (DEPTH, M_HEADS))
    i_bias = 0.1 * nrm(ks[11], (DEPTH, M_HEADS))
    return {
        "x_prompt": nrm(ks[0], (BATCH, SEQ, D_MODEL), jnp.float32),
        "x_sample": nrm(ks[1], (DEC_BATCH, DEC_SEQ, D_MODEL), jnp.float32),
        "state_mlstm_C": 0.3 * nrm(ks[2], (DEPTH, DEC_BATCH, M_HEADS, M_DH, M_DH), jnp.float32),
        "state_mlstm_n": 0.3 * nrm(ks[3], (DEPTH, DEC_BATCH, M_HEADS, M_DH), jnp.float32),
        "state_mlstm_m": nrm(ks[4], (DEPTH, DEC_BATCH, M_HEADS), jnp.float32),
        "state_mlstm_conv": nrm(ks[5], (DEPTH, DEC_BATCH, CONV_W - 1, 2 * M_WIDTH), jnp.float32),
        "state_ret": 0.5 * nrm(ks[6], (DEPTH, DEC_BATCH, R_HEADS, R_DH, R_DH), jnp.float32),
        "state_hgrn": 0.5 * nrm(ks[7], (DEPTH, DEC_BATCH, H_HEADS, H_DK, H_DV), jnp.float32),
        "w_in": nrm(ks[8], (DEPTH, D_MODEL, N_IN), jnp.float32) * D_MODEL ** -0.5,
        "conv_w": nrm(ks[9], (DEPTH, CONV_W, 2 * M_WIDTH), jnp.float32) * CONV_W ** -0.5,
        "conv_b": 0.01 * nrm(ks[12], (DEPTH, 2 * M_WIDTH), jnp.float32),
        "b_mgate": jnp.concatenate([i_bias, f_bias], axis=-1),
        "m_norm_w": 1.0 + 0.02 * nrm(ks[13], (DEPTH, M_WIDTH), jnp.float32),
        "r_norm_w": 1.0 + 0.02 * nrm(ks[14], (DEPTH, R_WIDTH), jnp.float32),
        "h_norm_w": 1.0 + 0.02 * nrm(ks[15], (DEPTH, H_WIDTH), jnp.float32),
        "hgrn_lb": nrm(ks[16], (DEPTH, H_WIDTH), jnp.float32),
        "w_out": nrm(ks[17], (DEPTH, D_MIX, D_MODEL), jnp.float32) * (D_MIX ** -0.5) * BETA,
        "ln_g": 1.0 + 0.02 * nrm(ks[18], (DEPTH, D_MODEL), jnp.float32),
        "ln_b": 0.01 * nrm(ks[19], (DEPTH, D_MODEL), jnp.float32),
    }


def reference(x_prompt, x_sample, state_mlstm_C, state_mlstm_n, state_mlstm_m, state_mlstm_conv,
              state_ret, state_hgrn, w_in, conv_w, conv_b, b_mgate, m_norm_w, r_norm_w, h_norm_w,
              hgrn_lb, w_out, ln_g, ln_b):
    T_p = x_prompt.shape[1]
    T_s = x_sample.shape[1]
    pos_p = jnp.arange(T_p, dtype=jnp.int32)
    pos_s = PAST_LEN + jnp.arange(T_s, dtype=jnp.int32)
    lb_all = jnp.cumsum(jax.nn.softmax(hgrn_lb.astype(jnp.float32), axis=0), axis=0)
    lb_all = lb_all - lb_all[0:1]
    hp, hs = x_prompt, x_sample
    new_p = [[] for _ in range(6)]
    new_s = [[] for _ in range(6)]
    for l in range(DEPTH):
        params = (w_in[l], conv_w[l], conv_b[l], b_mgate[l], m_norm_w[l], r_norm_w[l], h_norm_w[l],
                  lb_all[l], w_out[l], ln_g[l], ln_b[l])
        st_p = _zero_state(x_prompt.shape[0], x_prompt.dtype)
        st_s = (state_mlstm_C[l], state_mlstm_n[l], state_mlstm_m[l], state_mlstm_conv[l],
                state_ret[l], state_hgrn[l])
        hp, sp = hybrid_layer(hp, pos_p, st_p, *params)
        hs, ss = hybrid_layer(hs, pos_s, st_s, *params)
        for j in range(6):
            new_p[j].append(sp[j])
            new_s[j].append(ss[j])
    mC_p, mn_p, mm_p, conv_p, ret_p, hgrn_p = [jnp.stack(a) for a in new_p]
    mC_s, mn_s, mm_s, conv_s, ret_s, hgrn_s = [jnp.stack(a) for a in new_s]
    return (hp, hs, mC_p, mn_p, mm_p, conv_p, ret_p, hgrn_p, mC_s, mn_s, mm_s, conv_s, ret_s, hgrn_s)
```

```python
import functools
import math

import numpy as np
import jax
import jax.numpy as jnp
from jax import lax
from jax.experimental import pallas as pl
from jax.experimental.pallas import tpu as pltpu

F32 = jnp.float32
BF16 = jnp.bfloat16

D_MODEL = 1024
DEPTH = 2
PAST_LEN = 16384
N_HEADS = 4
M_DH = 96
R_DH = 96
H_DK = 64
M_WIDTH = N_HEADS * M_DH
R_WIDTH = N_HEADS * R_DH
H_WIDTH = N_HEADS * H_DK
D_MIX = M_WIDTH + R_WIDTH + H_WIDTH
CONV_W = 4
ROPE_BASE = 10000.0
LN_EPS = 1e-5
HEAD_EPS = 1e-6
ALPHA = (2 * DEPTH) ** 0.25

LANES = 128
SUBLANES = 8
HP = N_HEADS * LANES
ROPE_HALF = R_DH // 2
VMEM_LIMIT = 56 * 1024 * 1024

C_MQ, C_MK, C_MV, C_MO, C_MZ = 0, HP, 2 * HP, 3 * HP, 4 * HP
C_RQ, C_RK, C_RV, C_RG = 5 * HP, 6 * HP, 7 * HP, 8 * HP
C_HF = 9 * HP
C_HI = C_HF + H_WIDTH
C_HQ = C_HI + H_WIDTH
C_HG = C_HQ + H_WIDTH
C_IF = C_HG + H_WIDTH
NP_IN = C_IF + LANES
C_HK = NP_IN
NP_SCR = NP_IN + H_WIDTH
MIXP = 2 * HP + H_WIDTH

CHUNK = 128
N_LEVELS = int(math.log2(CHUNK))
LG = [math.log1p(-2.0 ** (-5.0 - h)) for h in range(N_HEADS)]


def _head_pad_idx(base, dh):
    idx = []
    for h in range(N_HEADS):
        idx += [base + h * dh + j for j in range(dh)] + [-1] * (LANES - dh)
    return idx


def _rope_pad_idx(base):
    idx = []
    gap = [-1] * (LANES // 2 - ROPE_HALF)
    for h in range(N_HEADS):
        b = base + h * R_DH
        idx += [b + j for j in range(ROPE_HALF)] + gap + [b + ROPE_HALF + j for j in range(ROPE_HALF)] + gap
    return idx


def _in_col_index():
    o_mq, o_mk, o_mv, o_mo, o_mz = 0, M_WIDTH, 2 * M_WIDTH, 3 * M_WIDTH, 4 * M_WIDTH
    o_if = 5 * M_WIDTH
    o_rq = o_if + 2 * N_HEADS
    o_rk, o_rv, o_rg = o_rq + R_WIDTH, o_rq + 2 * R_WIDTH, o_rq + 3 * R_WIDTH
    o_hf = o_rq + 4 * R_WIDTH
    idx = []
    for base in (o_mq, o_mk, o_mv, o_mo, o_mz):
        idx += _head_pad_idx(base, M_DH)
    idx += _rope_pad_idx(o_rq) + _rope_pad_idx(o_rk)
    idx += _head_pad_idx(o_rv, R_DH) + _head_pad_idx(o_rg, R_DH)
    idx += list(range(o_hf, o_hf + 4 * H_WIDTH))
    idx += list(range(o_if, o_if + 2 * N_HEADS)) + [-1] * (LANES - 2 * N_HEADS)
    assert len(idx) == NP_IN
    return np.asarray(idx, np.int32)


def _gather_cols(a, idx):
    safe = np.maximum(idx, 0)
    return jnp.where(jnp.asarray(idx >= 0), jnp.take(a, jnp.asarray(safe), axis=-1), 0.0)


def _level_tables(L):
    nl = int(math.log2(L))
    t = np.arange(L)[:, None]
    r = np.arange(L)[None, :]
    blocks = []
    for n in range(nl):
        w = L >> (n + 1)
        m = (t // (2 * w)) * 2 * w + w
        upper = t >= m
        blocks.append(np.where(upper, (r > m) & (r <= t), (r > t) & (r <= m)))
    blocks.append(r <= t)
    blocks.append(r > t)
    mall = np.concatenate(blocks, axis=0).astype(np.float32)
    tt = np.arange(L)[:, None]
    ss = np.arange(L)[None, :]
    lv = np.full((L, L), -1, np.int32)
    for n in range(nl):
        w = L >> (n + 1)
        same = (tt // (2 * w)) == (ss // (2 * w))
        cond = same & ((tt % (2 * w)) >= w) & ((ss % (2 * w)) < w)
        lv = np.where(cond, n, lv)
    lv = np.where(tt == ss, nl, lv)
    return mall, lv.astype(np.int32)


def _split3(x):
    hi = x.astype(BF16)
    r1 = x - hi.astype(F32)
    mid = r1.astype(BF16)
    lo = (r1 - mid.astype(F32)).astype(BF16)
    return hi, mid, lo


def _dot01(m01, parts):
    acc = jnp.dot(m01, parts[0], preferred_element_type=F32)
    acc = acc + jnp.dot(m01, parts[1], preferred_element_type=F32)
    return acc + jnp.dot(m01, parts[2], preferred_element_type=F32)


def _dot(a, b):
    return jnp.dot(a, b, preferred_element_type=F32)


def _dot_nt(a, b):
    return lax.dot_general(a, b, (((1,), (1,)), ((), ())), preferred_element_type=F32)


def _sigmoid(x):
    return 1.0 / (1.0 + jnp.exp(-x))


def _silu(x):
    return x * _sigmoid(x)


def _log_sigmoid(x):
    return jnp.minimum(x, 0.0) - jnp.log1p(jnp.exp(-jnp.abs(x)))


def _lane(shape):
    return lax.broadcasted_iota(jnp.int32, shape, len(shape) - 1)


def _hgrn_lower_bounds(hlb):
    mx = jnp.max(hlb, axis=0, keepdims=True)
    e = jnp.exp(hlb - mx)
    p = e / jnp.sum(e, axis=0, keepdims=True)
    rows = []
    acc = None
    for l in range(DEPTH):
        acc = p[l:l + 1] if acc is None else acc + p[l:l + 1]
        rows.append(acc - p[0:1])
    return rows


def _hgrn_gates(fpre, lb):
    a = jnp.log(lb)
    c = jnp.log1p(-lb) + _log_sigmoid(fpre)
    mx = jnp.maximum(a, c)
    lf = mx + jnp.log(jnp.exp(a - mx) + jnp.exp(c - mx))
    kh = (1.0 - lb) * _sigmoid(-fpre)
    return lf, kh


def _head_layer_norm_padded(h, valid):
    dh = M_DH
    hz = jnp.where(valid, h, 0.0)
    mu = jnp.sum(hz, axis=-1, keepdims=True) * (1.0 / dh)
    d = jnp.where(valid, h - mu, 0.0)
    var = jnp.sum(d * d, axis=-1, keepdims=True) * (1.0 / dh)
    return d * lax.rsqrt(var + HEAD_EPS)


def _pair_rms_norm(h, low):
    sq = h * h
    m0 = jnp.sum(jnp.where(low, sq, 0.0), axis=-1, keepdims=True) * (1.0 / H_DK)
    m1 = jnp.sum(jnp.where(low, 0.0, sq), axis=-1, keepdims=True) * (1.0 / H_DK)
    return h * jnp.where(low, lax.rsqrt(m0 + HEAD_EPS), lax.rsqrt(m1 + HEAD_EPS))


def _mix_and_project(p_ref, hm, hr, hh, mnw, rnw, hnw, x, wout, lng, lnb):
    T = x.shape[0]
    valid = _lane((T, LANES)) < M_DH
    low = _lane((T, LANES)) < H_DK
    parts = []
    for h in range(N_HEADS):
        sl = slice(h * LANES, (h + 1) * LANES)
        hn = _head_layer_norm_padded(hm[:, sl], valid)
        o = p_ref[:, C_MO + h * LANES:C_MO + (h + 1) * LANES]
        z = p_ref[:, C_MZ + h * LANES:C_MZ + (h + 1) * LANES]
        parts.append((hn * mnw[:, sl] * _sigmoid(o) * _silu(z)).astype(BF16))
    for h in range(N_HEADS):
        sl = slice(h * LANES, (h + 1) * LANES)
        hn = _head_layer_norm_padded(hr[:, sl], valid)
        g = p_ref[:, C_RG + h * LANES:C_RG + (h + 1) * LANES]
        parts.append((hn * rnw[:, sl] * _silu(g)).astype(BF16))
    for p in range(N_HEADS // 2):
        sl = slice(p * LANES, (p + 1) * LANES)
        hn = _pair_rms_norm(hh[:, sl], low)
        g = p_ref[:, C_HG + p * LANES:C_HG + (p + 1) * LANES]
        parts.append((hn * hnw[:, sl] * _silu(g)).astype(BF16))
    mix = jnp.concatenate(parts, axis=-1)
    out = _dot(mix, wout)
    r = ALPHA * x + out
    mu = jnp.mean(r, axis=-1, keepdims=True)
    d = r - mu
    var = jnp.mean(d * d, axis=-1, keepdims=True)
    return d * lax.rsqrt(var + LN_EPS) * lng + lnb


def _prefill_kernel(layer, TB,
                    x_ref, win_ref, wout_ref, cos_ref, sin_ref, convw_ref, convb_ref, bif_ref,
                    mnw_ref, rnw_ref, hnw_ref, hlb_ref, lng_ref, lnb_ref, mall_ref, lv_ref,
                    y_ref, cout_ref, nout_ref, mout_ref, convout_ref, rout_ref, sout_ref,
                    p_ref, pre_ref, ho_ref, qt_ref, kt_ref, c_ref, r_ref, s_ref, m_ref, dm_ref, dec_ref):
    L = CHUNK
    NL = N_LEVELS
    j = pl.program_id(1)
    n_tb = pl.num_programs(1)

    @pl.when(j == 0)
    def _init():
        c_ref[...] = jnp.zeros_like(c_ref)
        r_ref[...] = jnp.zeros_like(r_ref)
        s_ref[...] = jnp.zeros_like(s_ref)
        m_ref[...] = jnp.zeros_like(m_ref)
        pre_ref[0:SUBLANES, :] = jnp.zeros((SUBLANES, 2 * HP), F32)
        ti = lax.broadcasted_iota(jnp.int32, (L, L), 0)
        si = lax.broadcasted_iota(jnp.int32, (L, L), 1)
        dist = (ti - si).astype(F32)
        row = lax.broadcasted_iota(jnp.int32, (L, LANES), 0).astype(F32)
        for h in range(N_HEADS):
            dm_ref[h] = jnp.where(ti >= si, jnp.exp(dist * LG[h]), 0.0)
            dec_ref[0:L, h * LANES:(h + 1) * LANES] = jnp.exp((row + 1.0) * LG[h])
            dec_ref[L:2 * L, h * LANES:(h + 1) * LANES] = jnp.exp((L - 1.0 - row) * LG[h])

    x = x_ref[0]
    p_ref[:, 0:NP_IN] = _dot(x.astype(BF16), win_ref[...])

    pre_ref[SUBLANES:SUBLANES + TB, :] = p_ref[:, C_MQ:C_MQ + 2 * HP]
    acc = convb_ref[...] + convw_ref[CONV_W - 1:CONV_W, :] * pre_ref[SUBLANES:SUBLANES + TB, :]
    for i in range(1, CONV_W):
        acc = acc + convw_ref[CONV_W - 1 - i:CONV_W - i, :] * pre_ref[SUBLANES - i:SUBLANES - i + TB, :]
    qk = _silu(acc)
    p_ref[:, C_MQ:C_MQ + HP] = qk[:, 0:HP]
    p_ref[:, C_MK:C_MK + HP] = qk[:, HP:2 * HP] * (M_DH ** -0.5)
    tail = pre_ref[TB + SUBLANES - (CONV_W - 1):TB + SUBLANES, :]
    pre_ref[SUBLANES - (CONV_W - 1):SUBLANES, :] = tail

    @pl.when(j == n_tb - 1)
    def _conv_state():
        convout_ref[0] = tail

    g = p_ref[:, C_IF:C_IF + LANES] + bif_ref[...]
    p_ref[:, C_IF:C_IF + LANES] = jnp.where(_lane((TB, LANES)) < N_HEADS, g, _log_sigmoid(g))
    one96 = jnp.where(_lane((TB, LANES)) == M_DH, 1.0, 0.0)
    for h in range(N_HEADS):
        sl = slice(C_MV + h * LANES, C_MV + (h + 1) * LANES)
        p_ref[:, sl] = p_ref[:, sl] + one96

    cos = cos_ref[...]
    sin = sin_ref[...]
    for h in range(N_HEADS):
        sl = slice(C_RQ + h * LANES, C_RQ + (h + 1) * LANES)
        v = p_ref[:, sl]
        p_ref[:, sl] = v * cos + pltpu.roll(v, LANES // 2, 1) * sin
        sl = slice(C_RK + h * LANES, C_RK + (h + 1) * LANES)
        v = p_ref[:, sl]
        p_ref[:, sl] = (v * cos + pltpu.roll(v, LANES // 2, 1) * sin) * (R_DH ** -0.5)

    lb = _hgrn_lower_bounds(hlb_ref[...])[layer]
    lf, kh = _hgrn_gates(p_ref[:, C_HF:C_HF + H_WIDTH], lb)
    p_ref[:, C_HF:C_HF + H_WIDTH] = lf
    p_ref[:, C_HK:C_HK + H_WIDTH] = kh

    tril = lax.broadcasted_iota(jnp.int32, (L, L), 0) >= lax.broadcasted_iota(jnp.int32, (L, L), 1)
    bd = (lax.broadcasted_iota(jnp.int32, (LANES, LANES), 0) < H_DK) == (_lane((LANES, LANES)) < H_DK)
    low = _lane((L, LANES)) < H_DK

    def chunk_body(c, carry):
        r0 = pl.multiple_of(c * L, L)
        rows = pl.ds(r0, L)

        ga = p_ref[rows, C_IF:C_IF + LANES]
        cum = _dot01(mall_ref[NL * L:(NL + 1) * L, :], _split3(ga))
        a_all = ga - pltpu.roll(cum, LANES - N_HEADS, 1)
        a_t = a_all.T
        for h in range(N_HEADS):
            b_col = cum[:, N_HEADS + h:N_HEADS + h + 1]
            a_row = a_t[h:h + 1, :]
            log_d = jnp.where(tril, b_col + a_row, -jnp.inf)
            m_prev = m_ref[h:h + 1, 0:1]
            m_t = jnp.maximum(jnp.max(log_d, axis=1, keepdims=True), b_col + m_prev)
            pmat = jnp.exp(log_d - m_t)
            q = p_ref[rows, C_MQ + h * LANES:C_MQ + (h + 1) * LANES].astype(BF16)
            k = p_ref[rows, C_MK + h * LANES:C_MK + (h + 1) * LANES]
            v = p_ref[rows, C_MV + h * LANES:C_MV + (h + 1) * LANES].astype(BF16)
            s = (_dot_nt(q, k.astype(BF16)) * pmat).astype(BF16)
            w_prior = jnp.exp(b_col + m_prev - m_t)
            c_aug = c_ref[h]
            num = _dot(s, v) + _dot(q, c_aug.astype(BF16)) * w_prior
            den = num[:, M_DH:M_DH + 1]
            denom = jnp.maximum(jnp.abs(den), jnp.exp(-m_t))
            ho_ref[rows, h * LANES:(h + 1) * LANES] = num / denom
            m_new = m_t[L - 1:L, :]
            b_last = b_col[L - 1:L, :]
            dec = jnp.exp(b_last + m_prev - m_new)
            w_s = jnp.exp(b_last + a_all[:, h:h + 1] - m_new)
            kw_t = (k * w_s).T.astype(BF16)
            c_ref[h] = dec * c_aug + _dot(kw_t, v)
            m_ref[h:h + 1, :] = jnp.broadcast_to(m_new, (1, LANES))

        for h in range(N_HEADS):
            q = p_ref[rows, C_RQ + h * LANES:C_RQ + (h + 1) * LANES]
            k = p_ref[rows, C_RK + h * LANES:C_RK + (h + 1) * LANES]
            v = p_ref[rows, C_RV + h * LANES:C_RV + (h + 1) * LANES].astype(BF16)
            s = (_dot_nt(q.astype(BF16), k.astype(BF16)) * dm_ref[h]).astype(BF16)
            qd = (q * dec_ref[0:L, h * LANES:(h + 1) * LANES]).astype(BF16)
            kd_t = (k * dec_ref[L:2 * L, h * LANES:(h + 1) * LANES]).T.astype(BF16)
            r_old = r_ref[h]
            ho_ref[rows, HP + h * LANES:HP + (h + 1) * LANES] = _dot(s, v) + _dot(qd, r_old.astype(BF16))
            r_ref[h] = math.exp(L * LG[h]) * r_old + _dot(kd_t, v)

        lf3 = _split3(p_ref[rows, C_HF:C_HF + H_WIDTH])
        qh = p_ref[rows, C_HQ:C_HQ + H_WIDTH]
        khh = p_ref[rows, C_HK:C_HK + H_WIDTH]
        ri = lax.broadcasted_iota(jnp.int32, (L, H_WIDTH), 0)
        even_head = (_lane((L, H_WIDTH)) & H_DK) == 0
        for n in range(NL + 1):
            if n < NL:
                fac = jnp.exp(_dot01(mall_ref[n * L:(n + 1) * L, :], lf3))
                upper = (ri & (L >> (n + 1))) != 0
                qn = jnp.where(upper, qh * fac, 0.0)
                kn = jnp.where(upper, 0.0, khh * fac)
            else:
                qn, kn = qh, khh
            qt_ref[n, 0] = jnp.where(even_head, qn, 0.0).astype(BF16)
            qt_ref[n, 1] = jnp.where(even_head, 0.0, qn).astype(BF16)
            kt_ref[n] = kn.astype(BF16)
        e_pre = jnp.exp(_dot01(mall_ref[NL * L:(NL + 1) * L, :], lf3))
        e_suf = jnp.exp(_dot01(mall_ref[(NL + 1) * L:(NL + 2) * L, :], lf3))
        q_int = (qh * e_pre).astype(BF16)
        k_suf = khh * e_suf
        d_col = e_pre[L - 1:L, :].T
        lv = lv_ref[...]
        for p in range(N_HEADS // 2):
            psl = slice(p * LANES, (p + 1) * LANES)
            vp = p_ref[rows, C_HI + p * LANES:C_HI + (p + 1) * LANES]
            o_pair = _dot(q_int[:, psl], s_ref[p].astype(BF16))
            for hh in range(2):
                in_head = low if hh == 0 else jnp.logical_not(low)
                s_mat = jnp.zeros((L, L), F32)
                for n in range(NL + 1):
                    s_mat = jnp.where(lv == n, _dot_nt(qt_ref[n, hh, :, psl], kt_ref[n, :, psl]), s_mat)
                v_h = jnp.where(in_head, vp, 0.0).astype(BF16)
                o_pair = o_pair + _dot(s_mat.astype(BF16), v_h)
            ho_ref[rows, 2 * HP + p * LANES:2 * HP + (p + 1) * LANES] = o_pair
            upd = _dot(k_suf[:, psl].T.astype(BF16), vp.astype(BF16))
            s_ref[p] = jnp.where(bd, d_col[p * LANES:(p + 1) * LANES, :] * s_ref[p] + upd, 0.0)
        return carry

    lax.fori_loop(0, TB // L, chunk_body, 0)

    y_ref[0] = _mix_and_project(p_ref, ho_ref[:, 0:HP], ho_ref[:, HP:2 * HP], ho_ref[:, 2 * HP:MIXP],
                                mnw_ref[...], rnw_ref[...], hnw_ref[...], x_ref[0], wout_ref[...],
                                lng_ref[...], lnb_ref[...])

    @pl.when(j == n_tb - 1)
    def _final_states():
        for h in range(N_HEADS):
            c_aug = c_ref[h]
            cout_ref[0, h] = c_aug[0:M_DH, 0:M_DH]
            nout_ref[0, h:h + 1, :] = c_aug.T[M_DH:M_DH + 1, 0:M_DH]
            r_full = r_ref[h]
            rout_ref[0, h] = jnp.concatenate(
                [r_full[0:ROPE_HALF, 0:R_DH], r_full[LANES // 2:LANES // 2 + ROPE_HALF, 0:R_DH]], axis=0)
            p, hh = divmod(h, 2)
            sout_ref[0, h] = s_ref[p][hh * H_DK:(hh + 1) * H_DK, hh * H_DK:(hh + 1) * H_DK]
        mout_ref[0] = jnp.concatenate([m_ref[h:h + 1, 0:1] for h in range(N_HEADS)], axis=1)


def _full_spec(shape):
    nd = len(shape)
    return pl.BlockSpec(shape, lambda *_: (0,) * nd)


def _prefill_layer(layer, x, win, wout, cos, sin, convw, convb, bif, mnw, rnw, hnw, hlb, lng, lnb, mall, lv, TB):
    B, T, _ = x.shape
    n_tb = T // TB
    kern = functools.partial(_prefill_kernel, layer, TB)
    in_specs = [
        pl.BlockSpec((1, TB, D_MODEL), lambda b, j: (b, j, 0)),
        _full_spec(win.shape), _full_spec(wout.shape),
        pl.BlockSpec((TB, LANES), lambda b, j: (j, 0)),
        pl.BlockSpec((TB, LANES), lambda b, j: (j, 0)),
        _full_spec(convw.shape), _full_spec(convb.shape), _full_spec(bif.shape),
        _full_spec(mnw.shape), _full_spec(rnw.shape), _full_spec(hnw.shape), _full_spec(hlb.shape),
        _full_spec(lng.shape), _full_spec(lnb.shape), _full_spec(mall.shape), _full_spec(lv.shape),
    ]
    out_shape = (
        jax.ShapeDtypeStruct((B, T, D_MODEL), F32),
        jax.ShapeDtypeStruct((B, N_HEADS, M_DH, M_DH), F32),
        jax.ShapeDtypeStruct((B, N_HEADS, M_DH), F32),
        jax.ShapeDtypeStruct((B, 1, N_HEADS), F32),
        jax.ShapeDtypeStruct((B, CONV_W - 1, 2 * HP), F32),
        jax.ShapeDtypeStruct((B, N_HEADS, R_DH, R_DH), F32),
        jax.ShapeDtypeStruct((B, N_HEADS, H_DK, H_DK), F32),
    )
    out_specs = (
        pl.BlockSpec((1, TB, D_MODEL), lambda b, j: (b, j, 0)),
        pl.BlockSpec((1, N_HEADS, M_DH, M_DH), lambda b, j: (b, 0, 0, 0)),
        pl.BlockSpec((1, N_HEADS, M_DH), lambda b, j: (b, 0, 0)),
        pl.BlockSpec((1, 1, N_HEADS), lambda b, j: (b, 0, 0)),
        pl.BlockSpec((1, CONV_W - 1, 2 * HP), lambda b, j: (b, 0, 0)),
        pl.BlockSpec((1, N_HEADS, R_DH, R_DH), lambda b, j: (b, 0, 0, 0)),
        pl.BlockSpec((1, N_HEADS, H_DK, H_DK), lambda b, j: (b, 0, 0, 0)),
    )
    L = CHUNK
    scratch = [
        pltpu.VMEM((TB, NP_SCR), F32),
        pltpu.VMEM((TB + SUBLANES, 2 * HP), F32),
        pltpu.VMEM((TB, MIXP), F32),
        pltpu.VMEM((N_LEVELS + 1, 2, L, H_WIDTH), BF16),
        pltpu.VMEM((N_LEVELS + 1, L, H_WIDTH), BF16),
        pltpu.VMEM((N_HEADS, LANES, LANES), F32),
        pltpu.VMEM((N_HEADS, LANES, LANES), F32),
        pltpu.VMEM((N_HEADS // 2, LANES, LANES), F32),
        pltpu.VMEM((SUBLANES, LANES), F32),
        pltpu.VMEM((N_HEADS, L, L), F32),
        pltpu.VMEM((2 * L, HP), F32),
    ]
    return pl.pallas_call(
        kern, grid=(B, n_tb), in_specs=in_specs, out_specs=out_specs, out_shape=out_shape,
        scratch_shapes=scratch,
        compiler_params=pltpu.CompilerParams(dimension_semantics=("arbitrary", "arbitrary"),
                                             vmem_limit_bytes=VMEM_LIMIT),
        name=f"prefill_layer{layer}",
    )(x, win, wout, cos, sin, convw, convb, bif, mnw, rnw, hnw, hlb, lng, lnb, mall, lv)


DEC_TILE = SUBLANES


def _decode_kernel(x_ref, win_ref, wout_ref, cos_ref, sin_ref, convw_ref, convb_ref, bif_ref,
                   mnw_ref, rnw_ref, hnw_ref, hlb_ref, lng_ref, lnb_ref,
                   cin_ref, nin_ref, min_ref, convin_ref, rin_ref, sin_st_ref,
                   y_ref, cout_ref, nout_ref, mout_ref, convout_ref, rout_ref, sout_ref,
                   p_ref, hm_ref, hr_ref, hh_ref, xs_ref, lb_ref):
    BT = DEC_TILE
    l = pl.program_id(0)
    i = pl.program_id(1)
    r0 = pl.multiple_of(i * BT, BT)

    @pl.when(jnp.logical_and(l == 0, i == 0))
    def _init():
        rows = _hgrn_lower_bounds(hlb_ref[...])
        lb_ref[...] = jnp.concatenate(rows + [jnp.zeros((SUBLANES - DEPTH, H_WIDTH), F32)], axis=0)

    @pl.when(l == 0)
    def _load_x():
        xs_ref[pl.ds(r0, BT), :] = x_ref[...]

    x = xs_ref[pl.ds(r0, BT), :]
    p_ref[:, 0:NP_IN] = _dot(x.astype(BF16), win_ref[0])

    pre = p_ref[:, C_MQ:C_MQ + 2 * HP]
    cw = convw_ref[0]
    acc = convb_ref[0] + cw[CONV_W - 1:CONV_W, :] * pre
    for jj in range(CONV_W - 1):
        acc = acc + cw[jj:jj + 1, :] * convin_ref[0, :, jj, :]
    for jj in range(CONV_W - 2):
        convout_ref[0, :, jj, :] = convin_ref[0, :, jj + 1, :]
    convout_ref[0, :, CONV_W - 2, :] = pre
    qk = _silu(acc)
    p_ref[:, C_MQ:C_MQ + HP] = qk[:, 0:HP]
    p_ref[:, C_MK:C_MK + HP] = qk[:, HP:2 * HP] * (M_DH ** -0.5)

    g = p_ref[:, C_IF:C_IF + LANES] + bif_ref[0]
    ig = g[:, 0:N_HEADS]
    lf = _log_sigmoid(g[:, N_HEADS:2 * N_HEADS])
    m_prev = min_ref[0]
    m_t = jnp.maximum(ig, lf + m_prev)
    dec_all = jnp.exp(lf + m_prev - m_t)
    ws_all = jnp.exp(ig - m_t)
    en_all = jnp.exp(-m_t)
    mout_ref[0] = m_t

    cos = cos_ref[...]
    sin = sin_ref[...]
    for h in range(N_HEADS):
        sl = slice(C_RQ + h * LANES, C_RQ + (h + 1) * LANES)
        v = p_ref[:, sl]
        p_ref[:, sl] = v * cos + pltpu.roll(v, LANES // 2, 1) * sin
        sl = slice(C_RK + h * LANES, C_RK + (h + 1) * LANES)
        v = p_ref[:, sl]
        p_ref[:, sl] = (v * cos + pltpu.roll(v, LANES // 2, 1) * sin) * (R_DH ** -0.5)

    lb = lb_ref[pl.ds(l, 1), :]
    lfh, kh = _hgrn_gates(p_ref[:, C_HF:C_HF + H_WIDTH], lb)
    eh = jnp.exp(lfh)
    p_ref[:, C_HF:C_HF + H_WIDTH] = eh
    p_ref[:, C_HK:C_HK + H_WIDTH] = kh

    hm_ref[...] = jnp.zeros_like(hm_ref)
    hr_ref[...] = jnp.zeros_like(hr_ref)

    for r in range(BT):
        rs = slice(r, r + 1)
        for h in range(N_HEADS):
            hs = slice(h, h + 1)
            q_row = p_ref[rs, C_MQ + h * LANES:C_MQ + (h + 1) * LANES]
            k_row = p_ref[rs, C_MK + h * LANES:C_MK + (h + 1) * LANES]
            v_row = p_ref[rs, C_MV + h * LANES:C_MV + h * LANES + M_DH]
            dec = dec_all[rs, hs]
            w_s = ws_all[rs, hs]
            q_col = q_row.T[0:M_DH, :]
            k_col = k_row.T[0:M_DH, :]
            c_old = cin_ref[0, r, h]
            n_old = nin_ref[0, r, hs, :]
            s = jnp.sum(q_row * k_row, axis=1, keepdims=True) * w_s
            num = s * v_row + jnp.sum(q_col * c_old, axis=0, keepdims=True) * dec
            den = s + jnp.sum(q_row[:, 0:M_DH] * n_old, axis=1, keepdims=True) * dec
            denom = jnp.maximum(jnp.abs(den), en_all[rs, hs])
            hm_ref[rs, h * LANES:h * LANES + M_DH] = num / denom
            cout_ref[0, r, h] = dec * c_old + (w_s * k_col) * v_row
            nout_ref[0, r, hs, :] = dec * n_old + w_s * k_row[:, 0:M_DH]
        for h in range(N_HEADS):
            q_row = p_ref[rs, C_RQ + h * LANES:C_RQ + (h + 1) * LANES]
            k_row = p_ref[rs, C_RK + h * LANES:C_RK + (h + 1) * LANES]
            v_row = p_ref[rs, C_RV + h * LANES:C_RV + h * LANES + R_DH]
            q_t = q_row.T
            k_t = k_row.T
            q_col = jnp.concatenate([q_t[0:ROPE_HALF], q_t[LANES // 2:LANES // 2 + ROPE_HALF]], axis=0)
            k_col = jnp.concatenate([k_t[0:ROPE_HALF], k_t[LANES // 2:LANES // 2 + ROPE_HALF]], axis=0)
            r_old = rin_ref[0, r, h]
            gam = math.exp(LG[h])
            s = jnp.sum(q_row * k_row, axis=1, keepdims=True)
            hr_ref[rs, h * LANES:h * LANES + R_DH] = s * v_row + jnp.sum(q_col * r_old, axis=0, keepdims=True) * gam
            rout_ref[0, r, h] = gam * r_old + k_col * v_row
        e_row = p_ref[rs, C_HF:C_HF + H_WIDTH]
        k_row = p_ref[rs, C_HK:C_HK + H_WIDTH]
        q_row = p_ref[rs, C_HQ:C_HQ + H_WIDTH]
        e_col = e_row.T
        k_col = k_row.T
        qe_col = (q_row * e_row).T
        qk_row = q_row * k_row
        for h in range(N_HEADS):
            ks = slice(h * H_DK, (h + 1) * H_DK)
            v_row = p_ref[rs, C_HI + h * H_DK:C_HI + (h + 1) * H_DK]
            s_old = sin_st_ref[0, r, h]
            sc = jnp.sum(qk_row[:, ks], axis=1, keepdims=True)
            hh_ref[rs, ks] = sc * v_row + jnp.sum(qe_col[ks, :] * s_old, axis=0, keepdims=True)
            sout_ref[0, r, h] = e_col[ks, :] * s_old + k_col[ks, :] * v_row

    y = _mix_and_project(p_ref, hm_ref[...], hr_ref[...], hh_ref[...], mnw_ref[0], rnw_ref[0], hnw_ref[0],
                         x, wout_ref[0], lng_ref[0], lnb_ref[0])
    xs_ref[pl.ds(r0, BT), :] = y
    y_ref[0] = y


def _decode(x, win, wout, cos, sin, convw, convb, bif, mnw, rnw, hnw, hlb, lng, lnb, st_c, st_n, st_m, st_conv,
            st_r, st_s):
    B = x.shape[0]
    BT = DEC_TILE
    n_bt = B // BT

    def lspec(a):
        nd = a.ndim
        return pl.BlockSpec((1,) + a.shape[1:], lambda l, i: (l,) + (0,) * (nd - 1))

    def sspec(a):
        nd = a.ndim
        return pl.BlockSpec((1, BT) + a.shape[2:], lambda l, i: (l, i) + (0,) * (nd - 2))

    in_specs = [
        pl.BlockSpec((BT, D_MODEL), lambda l, i: (i, 0)),
        lspec(win), lspec(wout), _full_spec(cos.shape), _full_spec(sin.shape),
        lspec(convw), lspec(convb), lspec(bif), lspec(mnw), lspec(rnw), lspec(hnw), _full_spec(hlb.shape),
        lspec(lng), lspec(lnb),
        sspec(st_c), sspec(st_n), sspec(st_m), sspec(st_conv), sspec(st_r), sspec(st_s),
    ]
    out_shape = (
        jax.ShapeDtypeStruct((DEPTH, B, D_MODEL), F32),
        jax.ShapeDtypeStruct(st_c.shape, F32), jax.ShapeDtypeStruct(st_n.shape, F32),
        jax.ShapeDtypeStruct(st_m.shape, F32), jax.ShapeDtypeStruct(st_conv.shape, F32),
        jax.ShapeDtypeStruct(st_r.shape, F32), jax.ShapeDtypeStruct(st_s.shape, F32),
    )
    out_specs = (
        pl.BlockSpec((1, BT, D_MODEL), lambda l, i: (l, i, 0)),
        sspec(st_c), sspec(st_n), sspec(st_m), sspec(st_conv), sspec(st_r), sspec(st_s),
    )
    scratch = [
        pltpu.VMEM((BT, NP_SCR), F32),
        pltpu.VMEM((BT, HP), F32), pltpu.VMEM((BT, HP), F32), pltpu.VMEM((BT, H_WIDTH), F32),
        pltpu.VMEM((B, D_MODEL), F32),
        pltpu.VMEM((SUBLANES, H_WIDTH), F32),
    ]
    return pl.pallas_call(
        _decode_kernel, grid=(DEPTH, n_bt), in_specs=in_specs, out_specs=out_specs, out_shape=out_shape,
        scratch_shapes=scratch,
        compiler_params=pltpu.CompilerParams(dimension_semantics=("arbitrary", "arbitrary"),
                                             vmem_limit_bytes=VMEM_LIMIT),
        name="decode_step",
    )(x, win, wout, cos, sin, convw, convb, bif, mnw, rnw, hnw, hlb, lng, lnb,
      st_c, st_n, st_m, st_conv, st_r, st_s)


def _rope_tables(pos):
    inv = ROPE_BASE ** (-jnp.arange(ROPE_HALF, dtype=F32) / ROPE_HALF)
    ang = pos.astype(F32)[:, None] * inv[None, :]
    c, s = jnp.cos(ang), jnp.sin(ang)
    z = jnp.zeros((pos.shape[0], LANES // 2 - ROPE_HALF), F32)
    return jnp.concatenate([c, z, c, z], axis=1), jnp.concatenate([-s, z, s, z], axis=1)


PREFILL_BLOCK = 256


def kernel(x_prompt, x_sample, state_mlstm_C, state_mlstm_n, state_mlstm_m, state_mlstm_conv, state_ret, state_hgrn, w_in, conv_w, conv_b, b_mgate, m_norm_w, r_norm_w, h_norm_w, hgrn_lb, w_out, ln_g, ln_b):
    B, T, _ = x_prompt.shape
    col = _in_col_index()
    qk_idx = np.asarray(_head_pad_idx(0, M_DH) + _head_pad_idx(M_WIDTH, M_DH), np.int32)
    hd_idx = np.asarray(_head_pad_idx(0, M_DH), np.int32)
    mix_idx = np.concatenate([hd_idx, np.where(hd_idx >= 0, hd_idx + M_WIDTH, -1),
                              np.arange(M_WIDTH + R_WIDTH, D_MIX)]).astype(np.int32)

    win_p = _gather_cols(w_in, col).astype(BF16)
    wout_p = jnp.swapaxes(_gather_cols(jnp.swapaxes(w_out, 1, 2), mix_idx), 1, 2).astype(BF16)
    convw_p = _gather_cols(conv_w, qk_idx)
    convb_p = _gather_cols(conv_b, qk_idx)[:, None, :]
    bif_p = jnp.pad(b_mgate, ((0, 0), (0, LANES - 2 * N_HEADS)))[:, None, :]
    mnw_p = _gather_cols(m_norm_w, hd_idx)[:, None, :]
    rnw_p = _gather_cols(r_norm_w, hd_idx)[:, None, :]
    hnw_p = h_norm_w[:, None, :]
    lng_p = ln_g[:, None, :]
    lnb_p = ln_b[:, None, :]
    hlb = hgrn_lb.astype(F32)

    mall_np, lv_np = _level_tables(CHUNK)
    mall = jnp.asarray(mall_np, BF16)
    lv = jnp.asarray(lv_np)

    cos_p, sin_p = _rope_tables(jnp.arange(T, dtype=jnp.int32))
    cos_s, sin_s = _rope_tables(PAST_LEN + jnp.arange(x_sample.shape[1], dtype=jnp.int32))

    hp = x_prompt
    st = [[] for _ in range(6)]
    for l in range(DEPTH):
        outs = _prefill_layer(l, hp, win_p[l], wout_p[l], cos_p, sin_p, convw_p[l], convb_p[l], bif_p[l],
                              mnw_p[l], rnw_p[l], hnw_p[l], hlb, lng_p[l], lnb_p[l], mall, lv, PREFILL_BLOCK)
        hp = outs[0]
        for k in range(6):
            st[k].append(outs[1 + k])
    mC_p = jnp.stack(st[0])
    mn_p = jnp.stack(st[1])
    mm_p = jnp.stack(st[2])[:, :, 0, :]
    conv_keep = np.nonzero(qk_idx >= 0)[0]
    conv_p = jnp.take(jnp.stack(st[3]), jnp.asarray(conv_keep), axis=-1)
    ret_p = jnp.stack(st[4])
    hgrn_p = jnp.stack(st[5])

    conv_in = _gather_cols(state_mlstm_conv, qk_idx)
    ys, mC_s, mn_s, mm_s, conv_s_p, ret_s, hgrn_s = _decode(
        x_sample[:, 0, :], win_p, wout_p, cos_s, sin_s, convw_p, convb_p, bif_p, mnw_p, rnw_p, hnw_p, hlb,
        lng_p, lnb_p, state_mlstm_C, state_mlstm_n, state_mlstm_m, conv_in, state_ret, state_hgrn)
    conv_s = jnp.take(conv_s_p, jnp.asarray(conv_keep), axis=-1)
    hs = ys[DEPTH - 1][:, None, :]

    return (hp, hs, mC_p, mn_p, mm_p, conv_p, ret_p, hgrn_p, mC_s, mn_s, mm_s, conv_s, ret_s, hgrn_s)
```

```python
import functools
import math

import numpy as np
import jax
import jax.numpy as jnp
from jax import lax
from jax.experimental import pallas as pl
from jax.experimental.pallas import tpu as pltpu

F32 = jnp.float32
BF16 = jnp.bfloat16

D_MODEL = 1024
DEPTH = 2
PAST_LEN = 16384
N_HEADS = 4
M_DH = 96
R_DH = 96
H_DK = 64
M_WIDTH = N_HEADS * M_DH
R_WIDTH = N_HEADS * R_DH
H_WIDTH = N_HEADS * H_DK
D_MIX = M_WIDTH + R_WIDTH + H_WIDTH
CONV_W = 4
ROPE_BASE = 10000.0
LN_EPS = 1e-5
HEAD_EPS = 1e-6
ALPHA = (2 * DEPTH) ** 0.25

LANES = 128
SUBLANES = 8
HP = N_HEADS * LANES
ROPE_HALF = R_DH // 2
VMEM_LIMIT = 56 * 1024 * 1024

O_MQ, O_MK, O_MV, O_MO, O_MZ = 0, M_WIDTH, 2 * M_WIDTH, 3 * M_WIDTH, 4 * M_WIDTH
O_IF = 5 * M_WIDTH
O_RQ = O_IF + 2 * N_HEADS
O_RK, O_RV, O_RG = O_RQ + R_WIDTH, O_RQ + 2 * R_WIDTH, O_RQ + 3 * R_WIDTH
O_HF = O_RQ + 4 * R_WIDTH
O_HI, O_HQ, O_HG = O_HF + H_WIDTH, O_HF + 2 * H_WIDTH, O_HF + 3 * H_WIDTH
N_IN = O_HF + 4 * H_WIDTH
O_HK = N_IN

C_MQ, C_MK, C_MV, C_MO, C_MZ = 0, HP, 2 * HP, 3 * HP, 4 * HP
C_RQ, C_RK, C_RV, C_RG = 5 * HP, 6 * HP, 7 * HP, 8 * HP
C_HF = 9 * HP
C_HI = C_HF + H_WIDTH
C_HQ = C_HI + H_WIDTH
C_HG = C_HQ + H_WIDTH
C_IF = C_HG + H_WIDTH
NP_IN = C_IF + LANES
C_HK = NP_IN
NP_SCR = NP_IN + H_WIDTH
MIXP = 2 * HP + H_WIDTH

CHUNK = 128
N_LEVELS = int(math.log2(CHUNK))
LG = [math.log1p(-2.0 ** (-5.0 - h)) for h in range(N_HEADS)]


def _pad_heads(a, axis, dh=M_DH):
    shp = a.shape
    a = a.reshape(shp[:axis] + (N_HEADS, dh) + shp[axis + 1:])
    pad = [(0, 0)] * a.ndim
    pad[axis + 1] = (0, LANES - dh)
    return jnp.pad(a, pad).reshape(shp[:axis] + (HP,) + shp[axis + 1:])


def _pad_rope_heads(a, axis):
    shp = a.shape
    a = a.reshape(shp[:axis] + (N_HEADS, 2, ROPE_HALF) + shp[axis + 1:])
    pad = [(0, 0)] * a.ndim
    pad[axis + 2] = (0, LANES // 2 - ROPE_HALF)
    return jnp.pad(a, pad).reshape(shp[:axis] + (HP,) + shp[axis + 1:])


def _unpad_heads(a, dh=M_DH):
    shp = a.shape
    return a.reshape(shp[:-1] + (shp[-1] // LANES, LANES))[..., :dh].reshape(shp[:-1] + (shp[-1] // LANES * dh,))


def _level_tables(L):
    nl = int(math.log2(L))
    t = np.arange(L)[:, None]
    r = np.arange(L)[None, :]
    blocks = []
    for n in range(nl):
        w = L >> (n + 1)
        m = (t // (2 * w)) * 2 * w + w
        upper = t >= m
        blocks.append(np.where(upper, (r > m) & (r <= t), (r > t) & (r <= m)))
    blocks.append(r <= t)
    blocks.append(r > t)
    mall = np.concatenate(blocks, axis=0).astype(np.float32)
    tt = np.arange(L)[:, None]
    ss = np.arange(L)[None, :]
    lv = np.full((L, L), -1, np.int32)
    for n in range(nl):
        w = L >> (n + 1)
        same = (tt // (2 * w)) == (ss // (2 * w))
        cond = same & ((tt % (2 * w)) >= w) & ((ss % (2 * w)) < w)
        lv = np.where(cond, n, lv)
    lv = np.where(tt == ss, nl, lv)
    return mall, lv.astype(np.int32)


def _split3(x):
    hi = x.astype(BF16)
    r1 = x - hi.astype(F32)
    mid = r1.astype(BF16)
    lo = (r1 - mid.astype(F32)).astype(BF16)
    return hi, mid, lo


def _dot01(m01, parts):
    acc = jnp.dot(m01, parts[0], preferred_element_type=F32)
    acc = acc + jnp.dot(m01, parts[1], preferred_element_type=F32)
    return acc + jnp.dot(m01, parts[2], preferred_element_type=F32)


def _dot(a, b):
    return jnp.dot(a, b, preferred_element_type=F32)


def _dot_nt(a, b):
    return lax.dot_general(a, b, (((1,), (1,)), ((), ())), preferred_element_type=F32)


def _sigmoid(x):
    return 1.0 / (1.0 + jnp.exp(-x))


def _silu(x):
    return x * _sigmoid(x)


def _log_sigmoid(x):
    return jnp.minimum(x, 0.0) - jnp.log1p(jnp.exp(-jnp.abs(x)))


def _lane(shape):
    return lax.broadcasted_iota(jnp.int32, shape, len(shape) - 1)


def _hgrn_lower_bounds(hlb, layer_axis):
    def take(a, l):
        return a[l:l + 1] if layer_axis == 0 else a[:, l:l + 1]
    mx = jnp.max(hlb, axis=layer_axis, keepdims=True)
    e = jnp.exp(hlb - mx)
    p = e / jnp.sum(e, axis=layer_axis, keepdims=True)
    out = []
    acc = None
    for l in range(DEPTH):
        acc = take(p, l) if acc is None else acc + take(p, l)
        out.append(acc - take(p, 0))
    return out


def _hgrn_gates(fpre, lb):
    a = jnp.log(lb)
    c = jnp.log1p(-lb) + _log_sigmoid(fpre)
    mx = jnp.maximum(a, c)
    lf = mx + jnp.log(jnp.exp(a - mx) + jnp.exp(c - mx))
    kh = (1.0 - lb) * _sigmoid(-fpre)
    return lf, kh


def _layer_norm_rows(r, g, b):
    mu = jnp.mean(r, axis=-1, keepdims=True)
    d = r - mu
    var = jnp.mean(d * d, axis=-1, keepdims=True)
    return d * lax.rsqrt(var + LN_EPS) * g + b


def _head_layer_norm_padded(h, valid):
    dh = M_DH
    hz = jnp.where(valid, h, 0.0)
    mu = jnp.sum(hz, axis=-1, keepdims=True) * (1.0 / dh)
    d = jnp.where(valid, h - mu, 0.0)
    var = jnp.sum(d * d, axis=-1, keepdims=True) * (1.0 / dh)
    return d * lax.rsqrt(var + HEAD_EPS)


def _pair_rms_norm(h, low):
    sq = h * h
    m0 = jnp.sum(jnp.where(low, sq, 0.0), axis=-1, keepdims=True) * (1.0 / H_DK)
    m1 = jnp.sum(jnp.where(low, 0.0, sq), axis=-1, keepdims=True) * (1.0 / H_DK)
    return h * jnp.where(low, lax.rsqrt(m0 + HEAD_EPS), lax.rsqrt(m1 + HEAD_EPS))


def _mix_and_project(p_ref, hm, hr, hh, mnw, rnw, hnw, x, wout, lng, lnb):
    T = x.shape[0]
    valid = _lane((T, LANES)) < M_DH
    low = _lane((T, LANES)) < H_DK
    parts = []
    for h in range(N_HEADS):
        sl = slice(h * LANES, (h + 1) * LANES)
        hn = _head_layer_norm_padded(hm[:, sl], valid)
        o = p_ref[:, C_MO + h * LANES:C_MO + (h + 1) * LANES]
        z = p_ref[:, C_MZ + h * LANES:C_MZ + (h + 1) * LANES]
        parts.append((hn * mnw[:, sl] * _sigmoid(o) * _silu(z)).astype(BF16))
    for h in range(N_HEADS):
        sl = slice(h * LANES, (h + 1) * LANES)
        hn = _head_layer_norm_padded(hr[:, sl], valid)
        g = p_ref[:, C_RG + h * LANES:C_RG + (h + 1) * LANES]
        parts.append((hn * rnw[:, sl] * _silu(g)).astype(BF16))
    for p in range(N_HEADS // 2):
        sl = slice(p * LANES, (p + 1) * LANES)
        hn = _pair_rms_norm(hh[:, sl], low)
        g = p_ref[:, C_HG + p * LANES:C_HG + (p + 1) * LANES]
        parts.append((hn * hnw[:, sl] * _silu(g)).astype(BF16))
    mix = jnp.concatenate(parts, axis=-1)
    return _layer_norm_rows(ALPHA * x + _dot(mix, wout), lng, lnb)


def _prefill_kernel(layer, TB,
                    x_ref, win_ref, wout_ref, cos_ref, sin_ref, convw_ref, convb_ref, bif_ref,
                    mnw_ref, rnw_ref, hnw_ref, hlb_ref, lng_ref, lnb_ref, mall_ref, lv_ref,
                    y_ref, cout_ref, nout_ref, mout_ref, convout_ref, rout_ref, sout_ref,
                    p_ref, pre_ref, ho_ref, qt_ref, kt_ref, c_ref, r_ref, s_ref, m_ref, dm_ref, dec_ref):
    L = CHUNK
    NL = N_LEVELS
    j = pl.program_id(1)
    n_tb = pl.num_programs(1)

    @pl.when(j == 0)
    def _init():
        c_ref[...] = jnp.zeros_like(c_ref)
        r_ref[...] = jnp.zeros_like(r_ref)
        s_ref[...] = jnp.zeros_like(s_ref)
        m_ref[...] = jnp.zeros_like(m_ref)
        pre_ref[0:SUBLANES, :] = jnp.zeros((SUBLANES, 2 * HP), F32)
        ti = lax.broadcasted_iota(jnp.int32, (L, L), 0)
        si = lax.broadcasted_iota(jnp.int32, (L, L), 1)
        dist = (ti - si).astype(F32)
        row = lax.broadcasted_iota(jnp.int32, (L, LANES), 0).astype(F32)
        for h in range(N_HEADS):
            dm_ref[h] = jnp.where(ti >= si, jnp.exp(dist * LG[h]), 0.0)
            dec_ref[0:L, h * LANES:(h + 1) * LANES] = jnp.exp((row + 1.0) * LG[h])
            dec_ref[L:2 * L, h * LANES:(h + 1) * LANES] = jnp.exp((L - 1.0 - row) * LG[h])

    p_ref[:, 0:NP_IN] = _dot_nt(x_ref[0].astype(BF16), win_ref[...])

    pre_ref[SUBLANES:SUBLANES + TB, :] = p_ref[:, C_MQ:C_MQ + 2 * HP]
    acc = convb_ref[...] + convw_ref[CONV_W - 1:CONV_W, :] * pre_ref[SUBLANES:SUBLANES + TB, :]
    for i in range(1, CONV_W):
        acc = acc + convw_ref[CONV_W - 1 - i:CONV_W - i, :] * pre_ref[SUBLANES - i:SUBLANES - i + TB, :]
    qk = _silu(acc)
    p_ref[:, C_MQ:C_MQ + HP] = qk[:, 0:HP]
    p_ref[:, C_MK:C_MK + HP] = qk[:, HP:2 * HP] * (M_DH ** -0.5)
    tail = pre_ref[TB + SUBLANES - (CONV_W - 1):TB + SUBLANES, :]
    pre_ref[SUBLANES - (CONV_W - 1):SUBLANES, :] = tail

    @pl.when(j == n_tb - 1)
    def _conv_state():
        convout_ref[0] = tail

    g = p_ref[:, C_IF:C_IF + LANES] + bif_ref[...]
    p_ref[:, C_IF:C_IF + LANES] = jnp.where(_lane((TB, LANES)) < N_HEADS, g, _log_sigmoid(g))
    one96 = jnp.where(_lane((TB, LANES)) == M_DH, 1.0, 0.0)
    for h in range(N_HEADS):
        sl = slice(C_MV + h * LANES, C_MV + (h + 1) * LANES)
        p_ref[:, sl] = p_ref[:, sl] + one96

    cos = cos_ref[...]
    sin = sin_ref[...]
    for h in range(N_HEADS):
        sl = slice(C_RQ + h * LANES, C_RQ + (h + 1) * LANES)
        v = p_ref[:, sl]
        p_ref[:, sl] = v * cos + pltpu.roll(v, LANES // 2, 1) * sin
        sl = slice(C_RK + h * LANES, C_RK + (h + 1) * LANES)
        v = p_ref[:, sl]
        p_ref[:, sl] = (v * cos + pltpu.roll(v, LANES // 2, 1) * sin) * (R_DH ** -0.5)

    lb = _hgrn_lower_bounds(hlb_ref[...], 0)[layer]
    lf, kh = _hgrn_gates(p_ref[:, C_HF:C_HF + H_WIDTH], lb)
    p_ref[:, C_HF:C_HF + H_WIDTH] = lf
    p_ref[:, C_HK:C_HK + H_WIDTH] = kh

    tril = lax.broadcasted_iota(jnp.int32, (L, L), 0) >= lax.broadcasted_iota(jnp.int32, (L, L), 1)
    bd = (lax.broadcasted_iota(jnp.int32, (LANES, LANES), 0) < H_DK) == (_lane((LANES, LANES)) < H_DK)
    low = _lane((L, LANES)) < H_DK

    def chunk_body(c, carry):
        r0 = pl.multiple_of(c * L, L)
        rows = pl.ds(r0, L)

        ga = p_ref[rows, C_IF:C_IF + LANES]
        cum = _dot01(mall_ref[NL * L:(NL + 1) * L, :], _split3(ga))
        a_all = ga - pltpu.roll(cum, LANES - N_HEADS, 1)
        a_t = a_all.T
        for h in range(N_HEADS):
            b_col = cum[:, N_HEADS + h:N_HEADS + h + 1]
            a_row = a_t[h:h + 1, :]
            log_d = jnp.where(tril, b_col + a_row, -jnp.inf)
            m_prev = m_ref[h:h + 1, 0:1]
            m_t = jnp.maximum(jnp.max(log_d, axis=1, keepdims=True), b_col + m_prev)
            pmat = jnp.exp(log_d - m_t)
            q = p_ref[rows, C_MQ + h * LANES:C_MQ + (h + 1) * LANES].astype(BF16)
            k = p_ref[rows, C_MK + h * LANES:C_MK + (h + 1) * LANES]
            v = p_ref[rows, C_MV + h * LANES:C_MV + (h + 1) * LANES].astype(BF16)
            s = (_dot_nt(q, k.astype(BF16)) * pmat).astype(BF16)
            w_prior = jnp.exp(b_col + m_prev - m_t)
            c_aug = c_ref[h]
            num = _dot(s, v) + _dot(q, c_aug.astype(BF16)) * w_prior
            den = num[:, M_DH:M_DH + 1]
            denom = jnp.maximum(jnp.abs(den), jnp.exp(-m_t))
            ho_ref[rows, h * LANES:(h + 1) * LANES] = num / denom
            m_new = m_t[L - 1:L, :]
            b_last = b_col[L - 1:L, :]
            dec = jnp.exp(b_last + m_prev - m_new)
            w_s = jnp.exp(b_last + a_all[:, h:h + 1] - m_new)
            kw_t = (k * w_s).T.astype(BF16)
            c_ref[h] = dec * c_aug + _dot(kw_t, v)
            m_ref[h:h + 1, :] = jnp.broadcast_to(m_new, (1, LANES))

        for h in range(N_HEADS):
            q = p_ref[rows, C_RQ + h * LANES:C_RQ + (h + 1) * LANES]
            k = p_ref[rows, C_RK + h * LANES:C_RK + (h + 1) * LANES]
            v = p_ref[rows, C_RV + h * LANES:C_RV + (h + 1) * LANES].astype(BF16)
            s = (_dot_nt(q.astype(BF16), k.astype(BF16)) * dm_ref[h]).astype(BF16)
            qd = (q * dec_ref[0:L, h * LANES:(h + 1) * LANES]).astype(BF16)
            kd_t = (k * dec_ref[L:2 * L, h * LANES:(h + 1) * LANES]).T.astype(BF16)
            r_old = r_ref[h]
            ho_ref[rows, HP + h * LANES:HP + (h + 1) * LANES] = _dot(s, v) + _dot(qd, r_old.astype(BF16))
            r_ref[h] = math.exp(L * LG[h]) * r_old + _dot(kd_t, v)

        lf3 = _split3(p_ref[rows, C_HF:C_HF + H_WIDTH])
        qh = p_ref[rows, C_HQ:C_HQ + H_WIDTH]
        khh = p_ref[rows, C_HK:C_HK + H_WIDTH]
        ri = lax.broadcasted_iota(jnp.int32, (L, H_WIDTH), 0)
        even_head = (_lane((L, H_WIDTH)) & H_DK) == 0
        for n in range(NL + 1):
            if n < NL:
                fac = jnp.exp(_dot01(mall_ref[n * L:(n + 1) * L, :], lf3))
                upper = (ri & (L >> (n + 1))) != 0
                qn = jnp.where(upper, qh * fac, 0.0)
                kn = jnp.where(upper, 0.0, khh * fac)
            else:
                qn, kn = qh, khh
            qt_ref[n, 0] = jnp.where(even_head, qn, 0.0).astype(BF16)
            qt_ref[n, 1] = jnp.where(even_head, 0.0, qn).astype(BF16)
            kt_ref[n] = kn.astype(BF16)
        e_pre = jnp.exp(_dot01(mall_ref[NL * L:(NL + 1) * L, :], lf3))
        e_suf = jnp.exp(_dot01(mall_ref[(NL + 1) * L:(NL + 2) * L, :], lf3))
        q_int = (qh * e_pre).astype(BF16)
        k_suf = khh * e_suf
        d_col = e_pre[L - 1:L, :].T
        lv = lv_ref[...]
        for p in range(N_HEADS // 2):
            psl = slice(p * LANES, (p + 1) * LANES)
            vp = p_ref[rows, C_HI + p * LANES:C_HI + (p + 1) * LANES]
            o_pair = _dot(q_int[:, psl], s_ref[p].astype(BF16))
            for hh in range(2):
                in_head = low if hh == 0 else jnp.logical_not(low)
                s_mat = jnp.zeros((L, L), F32)
                for n in range(NL + 1):
                    s_mat = jnp.where(lv == n, _dot_nt(qt_ref[n, hh, :, psl], kt_ref[n, :, psl]), s_mat)
                v_h = jnp.where(in_head, vp, 0.0).astype(BF16)
                o_pair = o_pair + _dot(s_mat.astype(BF16), v_h)
            ho_ref[rows, 2 * HP + p * LANES:2 * HP + (p + 1) * LANES] = o_pair
            upd = _dot(k_suf[:, psl].T.astype(BF16), vp.astype(BF16))
            s_ref[p] = jnp.where(bd, d_col[p * LANES:(p + 1) * LANES, :] * s_ref[p] + upd, 0.0)
        return carry

    lax.fori_loop(0, TB // L, chunk_body, 0)

    y_ref[0] = _mix_and_project(p_ref, ho_ref[:, 0:HP], ho_ref[:, HP:2 * HP], ho_ref[:, 2 * HP:MIXP],
                                mnw_ref[...], rnw_ref[...], hnw_ref[...], x_ref[0], wout_ref[...],
                                lng_ref[...], lnb_ref[...])

    @pl.when(j == n_tb - 1)
    def _final_states():
        for h in range(N_HEADS):
            c_aug = c_ref[h]
            cout_ref[0, h] = c_aug[0:M_DH, 0:M_DH]
            nout_ref[0, h:h + 1, :] = c_aug.T[M_DH:M_DH + 1, 0:M_DH]
            r_full = r_ref[h]
            rout_ref[0, h] = jnp.concatenate(
                [r_full[0:ROPE_HALF, 0:R_DH], r_full[LANES // 2:LANES // 2 + ROPE_HALF, 0:R_DH]], axis=0)
            p, hh = divmod(h, 2)
            sout_ref[0, h] = s_ref[p][hh * H_DK:(hh + 1) * H_DK, hh * H_DK:(hh + 1) * H_DK]
        mout_ref[0] = jnp.concatenate([m_ref[h:h + 1, 0:1] for h in range(N_HEADS)], axis=1)


def _full_spec(shape):
    nd = len(shape)
    return pl.BlockSpec(shape, lambda *_: (0,) * nd)


def _prefill_layer(layer, x, win, wout, cos, sin, convw, convb, bif, mnw, rnw, hnw, hlb, lng, lnb, mall, lv, TB):
    B, T, _ = x.shape
    n_tb = T // TB
    kern = functools.partial(_prefill_kernel, layer, TB)
    in_specs = [
        pl.BlockSpec((1, TB, D_MODEL), lambda b, j: (b, j, 0)),
        _full_spec(win.shape), _full_spec(wout.shape),
        pl.BlockSpec((TB, LANES), lambda b, j: (j, 0)),
        pl.BlockSpec((TB, LANES), lambda b, j: (j, 0)),
        _full_spec(convw.shape), _full_spec(convb.shape), _full_spec(bif.shape),
        _full_spec(mnw.shape), _full_spec(rnw.shape), _full_spec(hnw.shape), _full_spec(hlb.shape),
        _full_spec(lng.shape), _full_spec(lnb.shape), _full_spec(mall.shape), _full_spec(lv.shape),
    ]
    out_shape = (
        jax.ShapeDtypeStruct((B, T, D_MODEL), F32),
        jax.ShapeDtypeStruct((B, N_HEADS, M_DH, M_DH), F32),
        jax.ShapeDtypeStruct((B, N_HEADS, M_DH), F32),
        jax.ShapeDtypeStruct((B, 1, N_HEADS), F32),
        jax.ShapeDtypeStruct((B, CONV_W - 1, 2 * HP), F32),
        jax.ShapeDtypeStruct((B, N_HEADS, R_DH, R_DH), F32),
        jax.ShapeDtypeStruct((B, N_HEADS, H_DK, H_DK), F32),
    )
    out_specs = (
        pl.BlockSpec((1, TB, D_MODEL), lambda b, j: (b, j, 0)),
        pl.BlockSpec((1, N_HEADS, M_DH, M_DH), lambda b, j: (b, 0, 0, 0)),
        pl.BlockSpec((1, N_HEADS, M_DH), lambda b, j: (b, 0, 0)),
        pl.BlockSpec((1, 1, N_HEADS), lambda b, j: (b, 0, 0)),
        pl.BlockSpec((1, CONV_W - 1, 2 * HP), lambda b, j: (b, 0, 0)),
        pl.BlockSpec((1, N_HEADS, R_DH, R_DH), lambda b, j: (b, 0, 0, 0)),
        pl.BlockSpec((1, N_HEADS, H_DK, H_DK), lambda b, j: (b, 0, 0, 0)),
    )
    L = CHUNK
    scratch = [
        pltpu.VMEM((TB, NP_SCR), F32),
        pltpu.VMEM((TB + SUBLANES, 2 * HP), F32),
        pltpu.VMEM((TB, MIXP), F32),
        pltpu.VMEM((N_LEVELS + 1, 2, L, H_WIDTH), BF16),
        pltpu.VMEM((N_LEVELS + 1, L, H_WIDTH), BF16),
        pltpu.VMEM((N_HEADS, LANES, LANES), F32),
        pltpu.VMEM((N_HEADS, LANES, LANES), F32),
        pltpu.VMEM((N_HEADS // 2, LANES, LANES), F32),
        pltpu.VMEM((SUBLANES, LANES), F32),
        pltpu.VMEM((N_HEADS, L, L), F32),
        pltpu.VMEM((2 * L, HP), F32),
    ]
    return pl.pallas_call(
        kern, grid=(B, n_tb), in_specs=in_specs, out_specs=out_specs, out_shape=out_shape,
        scratch_shapes=scratch,
        compiler_params=pltpu.CompilerParams(dimension_semantics=("arbitrary", "arbitrary"),
                                             vmem_limit_bytes=VMEM_LIMIT),
        name=f"prefill_layer{layer}",
    )(x, win, wout, cos, sin, convw, convb, bif, mnw, rnw, hnw, hlb, lng, lnb, mall, lv)


DEC_KB = 4
KR_M = M_DH // DEC_KB
KR_H = H_DK // DEC_KB
G_DEC, G_WS, G_EN, G_GAM, G_SM, G_QN, G_SR, G_SH = (i * N_HEADS for i in range(8))


def _stream_state(q_blk, dec, kw_blk, v, in_ref, out_ref, nrows):
    acc = jnp.zeros_like(v)
    for kk in range(nrows):
        st = in_ref[0, 0, kk]
        d = dec if dec.shape[0] == 1 else dec[kk:kk + 1, :]
        acc = acc + q_blk[kk:kk + 1, :] * st
        out_ref[0, 0, kk] = d * st + kw_blk[kk:kk + 1, :] * v
    return acc


def _decode_kernel(x_ref, wt_ref, wout_ref, cos_ref, sin_ref, convw_ref, convb_ref, bif_ref,
                   mnw_ref, rnw_ref, hnw_ref, hlbt_ref, lng_ref, lnb_ref,
                   cin_ref, nin_ref, min_ref, convin_ref, rin_ref, sin_st_ref,
                   y_ref, cout_ref, nout_ref, mout_ref, convout_ref, rout_ref, sout_ref,
                   p_ref, xs_ref, ht_ref, mix_ref, g_ref):
    l = pl.program_id(0)
    h = pl.program_id(1)
    kb = pl.program_id(2)
    first = jnp.logical_and(h == 0, kb == 0)
    last = jnp.logical_and(h == N_HEADS - 1, kb == DEC_KB - 1)
    B = x_ref.shape[0]

    @pl.when(jnp.logical_and(first, l == 0))
    def _load_x():
        xs_ref[...] = x_ref[...]

    @pl.when(first)
    def _project():
        p_ref[0:N_IN, :] = _dot_nt(wt_ref[0], xs_ref[...].astype(BF16))
        ht_ref[...] = jnp.zeros_like(ht_ref)

        pre = p_ref[O_MQ:O_MQ + 2 * M_WIDTH, :].T
        cw = convw_ref[0]
        acc = convb_ref[0] + cw[CONV_W - 1:CONV_W, :] * pre
        for jj in range(CONV_W - 1):
            acc = acc + cw[jj:jj + 1, :] * convin_ref[0, jj]
        for jj in range(CONV_W - 2):
            convout_ref[0, jj] = convin_ref[0, jj + 1]
        convout_ref[0, CONV_W - 2] = pre
        p_ref[O_MQ:O_MQ + 2 * M_WIDTH, :] = _silu(acc).T

        gt = p_ref[O_IF:O_IF + 2 * N_HEADS, :] + bif_ref[0]
        ig = gt[0:N_HEADS]
        lf = _log_sigmoid(gt[N_HEADS:2 * N_HEADS])
        m_prev = min_ref[0]
        m_t = jnp.maximum(ig, lf + m_prev)
        dec = jnp.exp(lf + m_prev - m_t)
        w_s = jnp.exp(ig - m_t)
        mout_ref[0] = m_t
        g_ref[G_DEC:G_DEC + N_HEADS, :] = dec
        g_ref[G_WS:G_WS + N_HEADS, :] = w_s
        g_ref[G_EN:G_EN + N_HEADS, :] = jnp.exp(-m_t)
        for hh in range(N_HEADS):
            q = p_ref[O_MQ + hh * M_DH:O_MQ + (hh + 1) * M_DH, :]
            k = p_ref[O_MK + hh * M_DH:O_MK + (hh + 1) * M_DH, :] * (M_DH ** -0.5)
            n_old = nin_ref[0, hh]
            d_h = dec[hh:hh + 1]
            kw = k * w_s[hh:hh + 1]
            g_ref[G_SM + hh:G_SM + hh + 1, :] = jnp.sum(q * k, axis=0, keepdims=True) * w_s[hh:hh + 1]
            g_ref[G_QN + hh:G_QN + hh + 1, :] = jnp.sum(q * n_old, axis=0, keepdims=True)
            nout_ref[0, hh] = d_h * n_old + kw
            p_ref[O_MK + hh * M_DH:O_MK + (hh + 1) * M_DH, :] = kw

        cos = cos_ref[...]
        sin = sin_ref[...]
        for hh in range(N_HEADS):
            g_ref[G_GAM + hh:G_GAM + hh + 1, :] = jnp.full((1, B), math.exp(LG[hh]), F32)
            rot = []
            for off, scale in ((O_RQ, 1.0), (O_RK, R_DH ** -0.5)):
                x1 = p_ref[off + hh * R_DH:off + hh * R_DH + ROPE_HALF, :]
                x2 = p_ref[off + hh * R_DH + ROPE_HALF:off + (hh + 1) * R_DH, :]
                r1 = (x1 * cos - x2 * sin) * scale
                r2 = (x1 * sin + x2 * cos) * scale
                p_ref[off + hh * R_DH:off + hh * R_DH + ROPE_HALF, :] = r1
                p_ref[off + hh * R_DH + ROPE_HALF:off + (hh + 1) * R_DH, :] = r2
                rot.append((r1, r2))
            (q1, q2), (k1, k2) = rot
            g_ref[G_SR + hh:G_SR + hh + 1, :] = (jnp.sum(q1 * k1, axis=0, keepdims=True)
                                                 + jnp.sum(q2 * k2, axis=0, keepdims=True))

        lbs = _hgrn_lower_bounds(hlbt_ref[...], 1)
        lb = lbs[0]
        for i in range(1, DEPTH):
            lb = jnp.where(l == i, lbs[i], lb)
        lfh, kh = _hgrn_gates(p_ref[O_HF:O_HF + H_WIDTH, :], lb)
        eh = jnp.exp(lfh)
        qh = p_ref[O_HQ:O_HQ + H_WIDTH, :]
        p_ref[O_HF:O_HF + H_WIDTH, :] = eh
        p_ref[O_HK:O_HK + H_WIDTH, :] = kh
        p_ref[O_HQ:O_HQ + H_WIDTH, :] = qh * eh
        qk = qh * kh
        for hh in range(N_HEADS):
            g_ref[G_SH + hh:G_SH + hh + 1, :] = jnp.sum(qk[hh * H_DK:(hh + 1) * H_DK], axis=0, keepdims=True)

    def rows(base, width, n):
        return pl.ds(pl.multiple_of(base + h * width + kb * n, SUBLANES), n)

    dec = g_ref[pl.ds(G_DEC + h, 1), :]
    acc = _stream_state(p_ref[rows(O_MQ, M_DH, KR_M), :], dec, p_ref[rows(O_MK, M_DH, KR_M), :],
                        p_ref[pl.ds(pl.multiple_of(O_MV + h * M_DH, SUBLANES), M_DH), :], cin_ref, cout_ref, KR_M)
    ht_ref[pl.ds(pl.multiple_of(h * M_DH, SUBLANES), M_DH), :] += acc

    gam = g_ref[pl.ds(G_GAM + h, 1), :]
    acc = _stream_state(p_ref[rows(O_RQ, R_DH, KR_M), :], gam, p_ref[rows(O_RK, R_DH, KR_M), :],
                        p_ref[pl.ds(pl.multiple_of(O_RV + h * R_DH, SUBLANES), R_DH), :], rin_ref, rout_ref, KR_M)
    ht_ref[pl.ds(pl.multiple_of(M_WIDTH + h * R_DH, SUBLANES), R_DH), :] += acc

    acc = _stream_state(p_ref[rows(O_HQ, H_DK, KR_H), :], p_ref[rows(O_HF, H_DK, KR_H), :],
                        p_ref[rows(O_HK, H_DK, KR_H), :],
                        p_ref[pl.ds(pl.multiple_of(O_HI + h * H_DK, SUBLANES), H_DK), :], sin_st_ref, sout_ref, KR_H)
    ht_ref[pl.ds(pl.multiple_of(M_WIDTH + R_WIDTH + h * H_DK, SUBLANES), H_DK), :] += acc

    @pl.when(last)
    def _finish():
        def head_ln(a):
            mu = jnp.mean(a, axis=0, keepdims=True)
            d = a - mu
            return d * lax.rsqrt(jnp.mean(d * d, axis=0, keepdims=True) + HEAD_EPS)

        for hh in range(N_HEADS):
            sl = slice(hh * M_DH, (hh + 1) * M_DH)
            v = p_ref[O_MV + hh * M_DH:O_MV + (hh + 1) * M_DH, :]
            d_h = g_ref[G_DEC + hh:G_DEC + hh + 1, :]
            s = g_ref[G_SM + hh:G_SM + hh + 1, :]
            num = s * v + ht_ref[sl, :] * d_h
            den = s + g_ref[G_QN + hh:G_QN + hh + 1, :] * d_h
            hm = num / jnp.maximum(jnp.abs(den), g_ref[G_EN + hh:G_EN + hh + 1, :])
            o = p_ref[O_MO + hh * M_DH:O_MO + (hh + 1) * M_DH, :]
            z = p_ref[O_MZ + hh * M_DH:O_MZ + (hh + 1) * M_DH, :]
            mix_ref[sl, :] = head_ln(hm) * mnw_ref[0, sl, :] * _sigmoid(o) * _silu(z)
        for hh in range(N_HEADS):
            sl = slice(hh * R_DH, (hh + 1) * R_DH)
            v = p_ref[O_RV + hh * R_DH:O_RV + (hh + 1) * R_DH, :]
            hr = (g_ref[G_SR + hh:G_SR + hh + 1, :] * v
                  + ht_ref[M_WIDTH + hh * R_DH:M_WIDTH + (hh + 1) * R_DH, :] * math.exp(LG[hh]))
            gg = p_ref[O_RG + hh * R_DH:O_RG + (hh + 1) * R_DH, :]
            mix_ref[M_WIDTH + hh * R_DH:M_WIDTH + (hh + 1) * R_DH, :] = head_ln(hr) * rnw_ref[0, sl, :] * _silu(gg)
        for hh in range(N_HEADS):
            sl = slice(hh * H_DK, (hh + 1) * H_DK)
            base = M_WIDTH + R_WIDTH
            v = p_ref[O_HI + hh * H_DK:O_HI + (hh + 1) * H_DK, :]
            ho = g_ref[G_SH + hh:G_SH + hh + 1, :] * v + ht_ref[base + hh * H_DK:base + (hh + 1) * H_DK, :]
            hn = ho * lax.rsqrt(jnp.mean(ho * ho, axis=0, keepdims=True) + HEAD_EPS)
            gg = p_ref[O_HG + hh * H_DK:O_HG + (hh + 1) * H_DK, :]
            mix_ref[base + hh * H_DK:base + (hh + 1) * H_DK, :] = hn * hnw_ref[0, sl, :] * _silu(gg)
        x = xs_ref[...]
        out = _dot(mix_ref[...].T.astype(BF16), wout_ref[0])
        y = _layer_norm_rows(ALPHA * x + out, lng_ref[0], lnb_ref[0])
        xs_ref[...] = y
        y_ref[0] = y


def _decode(x, wt, wout, cos, sin, convw, convb, bif, mnw, rnw, hnw, hlbt, lng, lnb, st_c, st_n, st_m, st_conv,
            st_r, st_s):
    B = x.shape[0]

    def lspec(a):
        nd = a.ndim
        return pl.BlockSpec((1,) + a.shape[1:], lambda l, h, kb: (l,) + (0,) * (nd - 1))

    def kv_spec(a, n):
        return pl.BlockSpec((1, 1, n) + a.shape[3:], lambda l, h, kb: (l, h, kb, 0, 0))

    in_specs = [
        _full_spec(x.shape), lspec(wt), lspec(wout), _full_spec(cos.shape), _full_spec(sin.shape),
        lspec(convw), lspec(convb), lspec(bif), lspec(mnw), lspec(rnw), lspec(hnw), _full_spec(hlbt.shape),
        lspec(lng), lspec(lnb),
        kv_spec(st_c, KR_M), lspec(st_n), lspec(st_m), lspec(st_conv), kv_spec(st_r, KR_M), kv_spec(st_s, KR_H),
    ]
    out_shape = (
        jax.ShapeDtypeStruct((DEPTH, B, D_MODEL), F32),
        jax.ShapeDtypeStruct(st_c.shape, F32), jax.ShapeDtypeStruct(st_n.shape, F32),
        jax.ShapeDtypeStruct(st_m.shape, F32), jax.ShapeDtypeStruct(st_conv.shape, F32),
        jax.ShapeDtypeStruct(st_r.shape, F32), jax.ShapeDtypeStruct(st_s.shape, F32),
    )
    out_specs = (
        pl.BlockSpec((1, B, D_MODEL), lambda l, h, kb: (l, 0, 0)),
        kv_spec(st_c, KR_M), lspec(st_n), lspec(st_m), lspec(st_conv), kv_spec(st_r, KR_M), kv_spec(st_s, KR_H),
    )
    scratch = [
        pltpu.VMEM((N_IN + H_WIDTH, B), F32),
        pltpu.VMEM((B, D_MODEL), F32),
        pltpu.VMEM((D_MIX, B), F32),
        pltpu.VMEM((D_MIX, B), F32),
        pltpu.VMEM((8 * N_HEADS, B), F32),
    ]
    return pl.pallas_call(
        _decode_kernel, grid=(DEPTH, N_HEADS, DEC_KB), in_specs=in_specs, out_specs=out_specs,
        out_shape=out_shape, scratch_shapes=scratch,
        compiler_params=pltpu.CompilerParams(dimension_semantics=("arbitrary", "arbitrary", "arbitrary"),
                                             vmem_limit_bytes=VMEM_LIMIT),
        name="decode_step",
    )(x, wt, wout, cos, sin, convw, convb, bif, mnw, rnw, hnw, hlbt, lng, lnb,
      st_c, st_n, st_m, st_conv, st_r, st_s)


def _rope_angles(pos):
    inv = ROPE_BASE ** (-jnp.arange(ROPE_HALF, dtype=F32) / ROPE_HALF)
    ang = pos.astype(F32)[:, None] * inv[None, :]
    return jnp.cos(ang), jnp.sin(ang)


PREFILL_BLOCK = 256


def kernel(x_prompt, x_sample, state_mlstm_C, state_mlstm_n, state_mlstm_m, state_mlstm_conv, state_ret, state_hgrn, w_in, conv_w, conv_b, b_mgate, m_norm_w, r_norm_w, h_norm_w, hgrn_lb, w_out, ln_g, ln_b):
    B, T, _ = x_prompt.shape

    wt = jnp.swapaxes(w_in, 1, 2).astype(BF16)
    sec = lambda o, n: wt[:, o:o + n]
    win_p = jnp.concatenate(
        [_pad_heads(sec(o, M_WIDTH), 1) for o in (O_MQ, O_MK, O_MV, O_MO, O_MZ)]
        + [_pad_rope_heads(sec(O_RQ, R_WIDTH), 1), _pad_rope_heads(sec(O_RK, R_WIDTH), 1),
           _pad_heads(sec(O_RV, R_WIDTH), 1), _pad_heads(sec(O_RG, R_WIDTH), 1),
           sec(O_HF, 4 * H_WIDTH),
           jnp.pad(sec(O_IF, 2 * N_HEADS), ((0, 0), (0, LANES - 2 * N_HEADS), (0, 0)))], axis=1)
    wout_b = w_out.astype(BF16)
    wout_p = jnp.concatenate([_pad_heads(wout_b[:, 0:M_WIDTH], 1), _pad_heads(wout_b[:, M_WIDTH:M_WIDTH + R_WIDTH], 1),
                              wout_b[:, M_WIDTH + R_WIDTH:]], axis=1)
    convw_p = jnp.concatenate([_pad_heads(conv_w[..., :M_WIDTH], 2), _pad_heads(conv_w[..., M_WIDTH:], 2)], axis=-1)
    convb_p = jnp.concatenate([_pad_heads(conv_b[..., :M_WIDTH], 1), _pad_heads(conv_b[..., M_WIDTH:], 1)],
                              axis=-1)[:, None, :]
    bif_p = jnp.pad(b_mgate, ((0, 0), (0, LANES - 2 * N_HEADS)))[:, None, :]
    mnw_p = _pad_heads(m_norm_w, 1)[:, None, :]
    rnw_p = _pad_heads(r_norm_w, 1)[:, None, :]
    hnw_p = h_norm_w[:, None, :]
    lng_p = ln_g[:, None, :]
    lnb_p = ln_b[:, None, :]
    hlb = hgrn_lb.astype(F32)

    mall_np, lv_np = _level_tables(CHUNK)
    mall = jnp.asarray(mall_np, BF16)
    lv = jnp.asarray(lv_np)

    c, s = _rope_angles(jnp.arange(T, dtype=jnp.int32))
    z = jnp.zeros((T, LANES // 2 - ROPE_HALF), F32)
    cos_p = jnp.concatenate([c, z, c, z], axis=1)
    sin_p = jnp.concatenate([-s, z, s, z], axis=1)

    hp = x_prompt
    st = [[] for _ in range(6)]
    for l in range(DEPTH):
        outs = _prefill_layer(l, hp, win_p[l], wout_p[l], cos_p, sin_p, convw_p[l], convb_p[l], bif_p[l],
                              mnw_p[l], rnw_p[l], hnw_p[l], hlb, lng_p[l], lnb_p[l], mall, lv, PREFILL_BLOCK)
        hp = outs[0]
        for k in range(6):
            st[k].append(outs[1 + k])
    mC_p = jnp.stack(st[0])
    mn_p = jnp.stack(st[1])
    mm_p = jnp.stack(st[2])[:, :, 0, :]
    conv_p = _unpad_heads(jnp.stack(st[3]))
    ret_p = jnp.stack(st[4])
    hgrn_p = jnp.stack(st[5])

    n_s = x_sample.shape[0]
    cs, ss = _rope_angles(PAST_LEN + jnp.arange(x_sample.shape[1], dtype=jnp.int32))
    cos_s = jnp.broadcast_to(cs[0][:, None], (ROPE_HALF, n_s))
    sin_s = jnp.broadcast_to(ss[0][:, None], (ROPE_HALF, n_s))
    ys, c_t, n_t, m_t, conv_t, r_t, s_t = _decode(
        x_sample[:, 0, :], wt, wout_b, cos_s, sin_s, conv_w, conv_b[:, None, :], b_mgate[:, :, None],
        m_norm_w[:, :, None], r_norm_w[:, :, None], h_norm_w[:, :, None], hlb.T, lng_p, lnb_p,
        jnp.transpose(state_mlstm_C, (0, 2, 3, 4, 1)), jnp.transpose(state_mlstm_n, (0, 2, 3, 1)),
        jnp.transpose(state_mlstm_m, (0, 2, 1)), jnp.transpose(state_mlstm_conv, (0, 2, 1, 3)),
        jnp.transpose(state_ret, (0, 2, 3, 4, 1)), jnp.transpose(state_hgrn, (0, 2, 3, 4, 1)))
    hs = ys[DEPTH - 1][:, None, :]
    mC_s = jnp.transpose(c_t, (0, 4, 1, 2, 3))
    mn_s = jnp.transpose(n_t, (0, 3, 1, 2))
    mm_s = jnp.transpose(m_t, (0, 2, 1))
    conv_s = jnp.transpose(conv_t, (0, 2, 1, 3))
    ret_s = jnp.transpose(r_t, (0, 4, 1, 2, 3))
    hgrn_s = jnp.transpose(s_t, (0, 4, 1, 2, 3))

    return (hp, hs, mC_p, mn_p, mm_p, conv_p, ret_p, hgrn_p, mC_s, mn_s, mm_s, conv_s, ret_s, hgrn_s)
```

```python
import functools
import math

import numpy as np
import jax
import jax.numpy as jnp
from jax import lax
from jax.experimental import pallas as pl
from jax.experimental.pallas import tpu as pltpu

F32 = jnp.float32
BF16 = jnp.bfloat16

D_MODEL = 1024
DEPTH = 2
PAST_LEN = 16384
N_HEADS = 4
M_DH = 96
R_DH = 96
H_DK = 64
M_WIDTH = N_HEADS * M_DH
R_WIDTH = N_HEADS * R_DH
H_WIDTH = N_HEADS * H_DK
D_MIX = M_WIDTH + R_WIDTH + H_WIDTH
CONV_W = 4
ROPE_BASE = 10000.0
LN_EPS = 1e-5
HEAD_EPS = 1e-6
ALPHA = (2 * DEPTH) ** 0.25

LANES = 128
SUBLANES = 8
HP = N_HEADS * LANES
ROPE_HALF = R_DH // 2
VMEM_LIMIT = 56 * 1024 * 1024

O_MQ, O_MK, O_MV, O_MO, O_MZ = 0, M_WIDTH, 2 * M_WIDTH, 3 * M_WIDTH, 4 * M_WIDTH
O_IF = 5 * M_WIDTH
O_RQ = O_IF + 2 * N_HEADS
O_RK, O_RV, O_RG = O_RQ + R_WIDTH, O_RQ + 2 * R_WIDTH, O_RQ + 3 * R_WIDTH
O_HF = O_RQ + 4 * R_WIDTH
O_HI, O_HQ, O_HG = O_HF + H_WIDTH, O_HF + 2 * H_WIDTH, O_HF + 3 * H_WIDTH
N_IN = O_HF + 4 * H_WIDTH
O_HK = N_IN

C_MQ, C_MK, C_MV, C_MO, C_MZ = 0, HP, 2 * HP, 3 * HP, 4 * HP
C_RQ, C_RK, C_RV, C_RG = 5 * HP, 6 * HP, 7 * HP, 8 * HP
C_HF = 9 * HP
C_HI = C_HF + H_WIDTH
C_HQ = C_HI + H_WIDTH
C_HG = C_HQ + H_WIDTH
C_IF = C_HG + H_WIDTH
NP_IN = C_IF + LANES
C_HK = NP_IN
NP_SCR = NP_IN + H_WIDTH
MIXP = 2 * HP + H_WIDTH

CHUNK = 128
N_LEVELS = int(math.log2(CHUNK))
LG = [math.log1p(-2.0 ** (-5.0 - h)) for h in range(N_HEADS)]


def _pad_heads(a, axis, dh=M_DH):
    shp = a.shape
    a = a.reshape(shp[:axis] + (N_HEADS, dh) + shp[axis + 1:])
    pad = [(0, 0)] * a.ndim
    pad[axis + 1] = (0, LANES - dh)
    return jnp.pad(a, pad).reshape(shp[:axis] + (HP,) + shp[axis + 1:])


def _pad_rope_heads(a, axis):
    shp = a.shape
    a = a.reshape(shp[:axis] + (N_HEADS, 2, ROPE_HALF) + shp[axis + 1:])
    pad = [(0, 0)] * a.ndim
    pad[axis + 2] = (0, LANES // 2 - ROPE_HALF)
    return jnp.pad(a, pad).reshape(shp[:axis] + (HP,) + shp[axis + 1:])


def _unpad_heads(a, dh=M_DH):
    shp = a.shape
    return a.reshape(shp[:-1] + (shp[-1] // LANES, LANES))[..., :dh].reshape(shp[:-1] + (shp[-1] // LANES * dh,))


def _level_tables(L):
    nl = int(math.log2(L))
    tt = np.arange(L)[:, None]
    ss = np.arange(L)[None, :]
    lv = np.full((L, L), -1, np.int32)
    for n in range(nl):
        w = L >> (n + 1)
        same = (tt // (2 * w)) == (ss // (2 * w))
        cond = same & ((tt % (2 * w)) >= w) & ((ss % (2 * w)) < w)
        lv = np.where(cond, n, lv)
    lv = np.where(tt == ss, nl, lv)
    return (ss <= tt).astype(np.float32), lv.astype(np.int32)


def _split3(x):
    hi = x.astype(BF16)
    r1 = x - hi.astype(F32)
    mid = r1.astype(BF16)
    lo = (r1 - mid.astype(F32)).astype(BF16)
    return hi, mid, lo


def _dot01(m01, parts):
    acc = jnp.dot(m01, parts[0], preferred_element_type=F32)
    acc = acc + jnp.dot(m01, parts[1], preferred_element_type=F32)
    return acc + jnp.dot(m01, parts[2], preferred_element_type=F32)


def _dot(a, b):
    return jnp.dot(a, b, preferred_element_type=F32)


def _dot_nt(a, b):
    return lax.dot_general(a, b, (((1,), (1,)), ((), ())), preferred_element_type=F32)


def _sigmoid(x):
    return 1.0 / (1.0 + jnp.exp(-x))


def _silu(x):
    return x * _sigmoid(x)


def _log_sigmoid(x):
    return jnp.minimum(x, 0.0) - jnp.log1p(jnp.exp(-jnp.abs(x)))


def _lane(shape):
    return lax.broadcasted_iota(jnp.int32, shape, len(shape) - 1)


def _hgrn_lower_bounds(hlb, layer_axis):
    def take(a, l):
        return a[l:l + 1] if layer_axis == 0 else a[:, l:l + 1]
    mx = jnp.max(hlb, axis=layer_axis, keepdims=True)
    e = jnp.exp(hlb - mx)
    p = e / jnp.sum(e, axis=layer_axis, keepdims=True)
    out = []
    acc = None
    for l in range(DEPTH):
        acc = take(p, l) if acc is None else acc + take(p, l)
        out.append(acc - take(p, 0))
    return out


def _hgrn_gates(fpre, lb):
    a = jnp.log(lb)
    c = jnp.log1p(-lb) + _log_sigmoid(fpre)
    mx = jnp.maximum(a, c)
    lf = mx + jnp.log(jnp.exp(a - mx) + jnp.exp(c - mx))
    kh = (1.0 - lb) * _sigmoid(-fpre)
    return lf, kh


def _mid_row_broadcast(bc, w):
    L, W = bc.shape
    if 2 * w >= SUBLANES:
        blocks = [jnp.broadcast_to(bc[b * 2 * w + w:b * 2 * w + w + 1, :], (2 * w, W)) for b in range(L // (2 * w))]
        return blocks[0] if len(blocks) == 1 else jnp.concatenate(blocks, axis=0)
    x3 = bc.reshape(L // SUBLANES, SUBLANES, W)
    sub = lax.broadcasted_iota(jnp.int32, x3.shape, 1)
    out = None
    for b in range(SUBLANES // (2 * w)):
        piece = jnp.broadcast_to(x3[:, b * 2 * w + w:b * 2 * w + w + 1, :], x3.shape)
        out = piece if out is None else jnp.where(sub >= b * 2 * w, piece, out)
    return out.reshape(L, W)


def _layer_norm_rows(r, g, b):
    mu = jnp.mean(r, axis=-1, keepdims=True)
    d = r - mu
    var = jnp.mean(d * d, axis=-1, keepdims=True)
    return d * lax.rsqrt(var + LN_EPS) * g + b


def _head_layer_norm_padded(h, valid):
    dh = M_DH
    hz = jnp.where(valid, h, 0.0)
    mu = jnp.sum(hz, axis=-1, keepdims=True) * (1.0 / dh)
    d = jnp.where(valid, h - mu, 0.0)
    var = jnp.sum(d * d, axis=-1, keepdims=True) * (1.0 / dh)
    return d * lax.rsqrt(var + HEAD_EPS)


def _pair_rms_norm(h, low):
    sq = h * h
    m0 = jnp.sum(jnp.where(low, sq, 0.0), axis=-1, keepdims=True) * (1.0 / H_DK)
    m1 = jnp.sum(jnp.where(low, 0.0, sq), axis=-1, keepdims=True) * (1.0 / H_DK)
    return h * jnp.where(low, lax.rsqrt(m0 + HEAD_EPS), lax.rsqrt(m1 + HEAD_EPS))


def _mix_and_project(p_ref, ho_ref, rows, mnw, rnw, hnw, x, wout, lng, lnb):
    T = x.shape[0]
    valid = _lane((T, LANES)) < M_DH
    low = _lane((T, LANES)) < H_DK
    parts = []
    for h in range(N_HEADS):
        sl = slice(h * LANES, (h + 1) * LANES)
        hn = _head_layer_norm_padded(ho_ref[rows, sl], valid)
        o = p_ref[rows, C_MO + h * LANES:C_MO + (h + 1) * LANES]
        z = p_ref[rows, C_MZ + h * LANES:C_MZ + (h + 1) * LANES]
        parts.append((hn * mnw[:, sl] * _sigmoid(o) * _silu(z)).astype(BF16))
    for h in range(N_HEADS):
        sl = slice(h * LANES, (h + 1) * LANES)
        hn = _head_layer_norm_padded(ho_ref[rows, HP + h * LANES:HP + (h + 1) * LANES], valid)
        g = p_ref[rows, C_RG + h * LANES:C_RG + (h + 1) * LANES]
        parts.append((hn * rnw[:, sl] * _silu(g)).astype(BF16))
    for p in range(N_HEADS // 2):
        sl = slice(p * LANES, (p + 1) * LANES)
        hn = _pair_rms_norm(ho_ref[rows, 2 * HP + p * LANES:2 * HP + (p + 1) * LANES], low)
        g = p_ref[rows, C_HG + p * LANES:C_HG + (p + 1) * LANES]
        parts.append((hn * hnw[:, sl] * _silu(g)).astype(BF16))
    mix = jnp.concatenate(parts, axis=-1)
    return _layer_norm_rows(ALPHA * x + _dot(mix, wout), lng, lnb)


def _prefill_kernel(layer, TB,
                    x_ref, win_ref, wout_ref, cos_ref, sin_ref, convw_ref, convb_ref, bif_ref,
                    mnw_ref, rnw_ref, hnw_ref, hlb_ref, lng_ref, lnb_ref, tril_ref, lv_ref,
                    y_ref, cout_ref, nout_ref, mout_ref, convout_ref, rout_ref, sout_ref,
                    p_ref, pre_ref, ho_ref, cum_ref, ml_ref, pm_ref, kw_ref, qt_ref, kt_ref, qi_ref, dh_ref,
                    u_ref, ur_ref, uh_ref, c_ref, r_ref, s_ref, m_ref, dm_ref, dec_ref):
    L = CHUNK
    NL = N_LEVELS
    n_chunks = TB // L
    j = pl.program_id(1)
    n_tb = pl.num_programs(1)

    @pl.when(j == 0)
    def _init():
        c_ref[...] = jnp.zeros_like(c_ref)
        r_ref[...] = jnp.zeros_like(r_ref)
        s_ref[...] = jnp.zeros_like(s_ref)
        m_ref[...] = jnp.zeros_like(m_ref)
        pre_ref[0:SUBLANES, :] = jnp.zeros((SUBLANES, 2 * HP), F32)
        ti = lax.broadcasted_iota(jnp.int32, (L, L), 0)
        si = lax.broadcasted_iota(jnp.int32, (L, L), 1)
        dist = (ti - si).astype(F32)
        row = lax.broadcasted_iota(jnp.int32, (L, LANES), 0).astype(F32)
        for h in range(N_HEADS):
            dm_ref[h] = jnp.where(ti >= si, jnp.exp(dist * LG[h]), 0.0)
            dec_ref[0:L, h * LANES:(h + 1) * LANES] = jnp.exp((row + 1.0) * LG[h])
            dec_ref[L:2 * L, h * LANES:(h + 1) * LANES] = jnp.exp((L - 1.0 - row) * LG[h])

    p_ref[:, 0:NP_IN] = _dot_nt(x_ref[0].astype(BF16), win_ref[...])

    pre_ref[SUBLANES:SUBLANES + TB, :] = p_ref[:, C_MQ:C_MQ + 2 * HP]
    acc = convb_ref[...] + convw_ref[CONV_W - 1:CONV_W, :] * pre_ref[SUBLANES:SUBLANES + TB, :]
    for i in range(1, CONV_W):
        acc = acc + convw_ref[CONV_W - 1 - i:CONV_W - i, :] * pre_ref[SUBLANES - i:SUBLANES - i + TB, :]
    qk = _silu(acc)
    p_ref[:, C_MQ:C_MQ + HP] = qk[:, 0:HP]
    p_ref[:, C_MK:C_MK + HP] = qk[:, HP:2 * HP] * (M_DH ** -0.5)
    pre_ref[SUBLANES - (CONV_W - 1):SUBLANES, :] = pre_ref[TB + SUBLANES - (CONV_W - 1):TB + SUBLANES, :]

    g = p_ref[:, C_IF:C_IF + LANES] + bif_ref[...]
    p_ref[:, C_IF:C_IF + LANES] = jnp.where(_lane((TB, LANES)) < N_HEADS, g, _log_sigmoid(g))
    one96 = jnp.where(_lane((TB, LANES)) == M_DH, 1.0, 0.0)
    for h in range(N_HEADS):
        sl = slice(C_MV + h * LANES, C_MV + (h + 1) * LANES)
        p_ref[:, sl] = p_ref[:, sl] + one96

    cos = cos_ref[...]
    sin = sin_ref[...]
    for h in range(N_HEADS):
        sl = slice(C_RQ + h * LANES, C_RQ + (h + 1) * LANES)
        v = p_ref[:, sl]
        p_ref[:, sl] = v * cos + pltpu.roll(v, LANES // 2, 1) * sin
        sl = slice(C_RK + h * LANES, C_RK + (h + 1) * LANES)
        v = p_ref[:, sl]
        p_ref[:, sl] = (v * cos + pltpu.roll(v, LANES // 2, 1) * sin) * (R_DH ** -0.5)

    lb = _hgrn_lower_bounds(hlb_ref[...], 0)[layer]
    lf, kh = _hgrn_gates(p_ref[:, C_HF:C_HF + H_WIDTH], lb)
    p_ref[:, C_HF:C_HF + H_WIDTH] = lf
    p_ref[:, C_HK:C_HK + H_WIDTH] = kh

    tril = lax.broadcasted_iota(jnp.int32, (L, L), 0) >= lax.broadcasted_iota(jnp.int32, (L, L), 1)
    bd = (lax.broadcasted_iota(jnp.int32, (LANES, LANES), 0) < H_DK) == (_lane((LANES, LANES)) < H_DK)
    low = _lane((L, LANES)) < H_DK
    lane = _lane((L, LANES))

    def intra(c, carry):
        rows = pl.ds(pl.multiple_of(c * L, L), L)

        ga = p_ref[rows, C_IF:C_IF + LANES]
        cum = _dot01(tril_ref[...], _split3(ga))
        bc = _dot01(tril_ref[...], _split3(p_ref[rows, C_HF:C_HF + H_WIDTH]))

        def head(col, h, dt=None):
            a = p_ref[rows, col + h * LANES:col + (h + 1) * LANES]
            return a if dt is None else a.astype(dt)

        qk_r = [_dot_nt(head(C_RQ, h, BF16), head(C_RK, h, BF16)) for h in range(N_HEADS)]
        qk_m = [_dot_nt(head(C_MQ, h, BF16), head(C_MK, h, BF16)) for h in range(N_HEADS)]

        for h in range(N_HEADS):
            v = head(C_RV, h, BF16)
            s = (qk_r[h] * dm_ref[h]).astype(BF16)
            ho_ref[rows, HP + h * LANES:HP + (h + 1) * LANES] = _dot(s, v)
            kd_t = (head(C_RK, h) * dec_ref[L:2 * L, h * LANES:(h + 1) * LANES]).T.astype(BF16)
            ur_ref[c, h] = _dot(kd_t, v)

        cum_ref[rows, :] = cum
        a_all = ga - pltpu.roll(cum, LANES - N_HEADS, 1)
        a_t = a_all.T
        ml = jnp.zeros((L, LANES), F32)
        for h in range(N_HEADS):
            b_col = cum[:, N_HEADS + h:N_HEADS + h + 1]
            log_d = jnp.where(tril, b_col + a_t[h:h + 1, :], -jnp.inf)
            m_loc = jnp.max(log_d, axis=1, keepdims=True)
            pm_ref[h] = (qk_m[h] * jnp.exp(log_d - m_loc)).astype(BF16)
            ml = jnp.where(lane == h, m_loc, ml)
            w_loc = jnp.exp(b_col[L - 1:L, :] + a_all[:, h:h + 1] - m_loc[L - 1:L, :])
            kw_ref[h] = (head(C_MK, h) * w_loc).T.astype(BF16)
        ml_ref[rows, :] = ml

        qh = p_ref[rows, C_HQ:C_HQ + H_WIDTH]
        khh = p_ref[rows, C_HK:C_HK + H_WIDTH]
        ri = lax.broadcasted_iota(jnp.int32, (L, H_WIDTH), 0)
        even_head = (_lane((L, H_WIDTH)) & H_DK) == 0
        for n in range(NL + 1):
            if n < NL:
                w = L >> (n + 1)
                fac = jnp.exp(-jnp.abs(bc - _mid_row_broadcast(bc, w)))
                upper = (ri & w) != 0
                qn = jnp.where(upper, qh * fac, 0.0)
                kn = jnp.where(upper, 0.0, khh * fac)
            else:
                qn, kn = qh, khh
            qt_ref[n, 0] = jnp.where(even_head, qn, 0.0).astype(BF16)
            qt_ref[n, 1] = jnp.where(even_head, 0.0, qn).astype(BF16)
            kt_ref[n] = kn.astype(BF16)
        e_pre = jnp.exp(bc)
        qi_ref[rows, :] = (qh * e_pre).astype(BF16)
        dh_ref[pl.ds(c, 1), :] = e_pre[L - 1:L, :]
        k_suf = khh * jnp.exp(bc[L - 1:L, :] - bc)

        for h in range(N_HEADS):
            v = head(C_MV, h, BF16)
            ho_ref[rows, h * LANES:(h + 1) * LANES] = _dot(pm_ref[h], v)
            u_ref[c, h] = _dot(kw_ref[h], v)

        lv = lv_ref[...]
        heads = [(p, hh) for p in range(N_HEADS // 2) for hh in range(2)]

        def level_scores(p, hh):
            psl = slice(p * LANES, (p + 1) * LANES)
            return [_dot_nt(qt_ref[n, hh, :, psl], kt_ref[n, :, psl]) for n in range(NL + 1)]

        def fold(scores):
            s_mat = jnp.zeros((L, L), F32)
            for n, sc in enumerate(scores):
                s_mat = jnp.where(lv == n, sc, s_mat)
            return s_mat.astype(BF16)

        pending = level_scores(*heads[0])
        s_bf = []
        for i in range(len(heads)):
            nxt = level_scores(*heads[i + 1]) if i + 1 < len(heads) else None
            s_bf.append(fold(pending))
            pending = nxt
        for p in range(N_HEADS // 2):
            psl = slice(p * LANES, (p + 1) * LANES)
            vp = p_ref[rows, C_HI + p * LANES:C_HI + (p + 1) * LANES]
            ho_ref[rows, 2 * HP + p * LANES:2 * HP + (p + 1) * LANES] = (
                _dot(s_bf[2 * p], jnp.where(low, vp, 0.0).astype(BF16))
                + _dot(s_bf[2 * p + 1], jnp.where(low, 0.0, vp).astype(BF16)))
            uh_ref[c, p] = jnp.where(bd, _dot(k_suf[:, psl].T.astype(BF16), vp.astype(BF16)), 0.0)
        return carry

    lax.fori_loop(0, n_chunks, intra, 0)

    for c in range(n_chunks):
        rows = slice(c * L, (c + 1) * L)

        ml = ml_ref[rows, :]
        b_all = pltpu.roll(cum_ref[rows, :], LANES - N_HEADS, 1)
        m_prev = m_ref[0:1, :]
        m_t = jnp.maximum(ml, b_all + m_prev)
        r_all = jnp.exp(ml - m_t)
        wp_all = jnp.exp(b_all + m_prev - m_t)
        en_all = jnp.exp(-m_t)
        m_new = m_t[L - 1:L, :]
        dec_row = jnp.exp(b_all[L - 1:L, :] + m_prev - m_new)
        g_row = jnp.exp(ml[L - 1:L, :] - m_new)
        m_ref[0:1, :] = m_new
        for h in range(N_HEADS):
            sl = slice(h * LANES, (h + 1) * LANES)
            q = p_ref[rows, C_MQ + h * LANES:C_MQ + (h + 1) * LANES].astype(BF16)
            c_aug = c_ref[h]
            num = r_all[:, h:h + 1] * ho_ref[rows, sl] + wp_all[:, h:h + 1] * _dot(q, c_aug.astype(BF16))
            den = num[:, M_DH:M_DH + 1]
            ho_ref[rows, sl] = num / jnp.maximum(jnp.abs(den), en_all[:, h:h + 1])
            c_ref[h] = dec_row[:, h:h + 1] * c_aug + g_row[:, h:h + 1] * u_ref[c, h]

        for h in range(N_HEADS):
            sl = slice(HP + h * LANES, HP + (h + 1) * LANES)
            q = p_ref[rows, C_RQ + h * LANES:C_RQ + (h + 1) * LANES]
            qd = (q * dec_ref[0:L, h * LANES:(h + 1) * LANES]).astype(BF16)
            r_old = r_ref[h]
            ho_ref[rows, sl] = ho_ref[rows, sl] + _dot(qd, r_old.astype(BF16))
            r_ref[h] = math.exp(L * LG[h]) * r_old + ur_ref[c, h]

        d_col = dh_ref[c:c + 1, :].T
        for p in range(N_HEADS // 2):
            psl = slice(p * LANES, (p + 1) * LANES)
            sl = slice(2 * HP + p * LANES, 2 * HP + (p + 1) * LANES)
            s_old = s_ref[p]
            ho_ref[rows, sl] = ho_ref[rows, sl] + _dot(qi_ref[rows, psl], s_old.astype(BF16))
            s_ref[p] = d_col[psl, :] * s_old + uh_ref[c, p]

    y_ref[0] = _mix_and_project(p_ref, ho_ref, slice(None), mnw_ref[...], rnw_ref[...], hnw_ref[...],
                                x_ref[0], wout_ref[...], lng_ref[...], lnb_ref[...])

    @pl.when(j == n_tb - 1)
    def _final_states():
        convout_ref[0] = pre_ref[SUBLANES - (CONV_W - 1):SUBLANES, :]
        for h in range(N_HEADS):
            c_aug = c_ref[h]
            cout_ref[0, h] = c_aug[0:M_DH, 0:M_DH]
            nout_ref[0, h:h + 1, :] = c_aug.T[M_DH:M_DH + 1, 0:M_DH]
            r_full = r_ref[h]
            rout_ref[0, h] = jnp.concatenate(
                [r_full[0:ROPE_HALF, 0:R_DH], r_full[LANES // 2:LANES // 2 + ROPE_HALF, 0:R_DH]], axis=0)
            p, hh = divmod(h, 2)
            sout_ref[0, h] = s_ref[p][hh * H_DK:(hh + 1) * H_DK, hh * H_DK:(hh + 1) * H_DK]
        mout_ref[0] = m_ref[0:1, 0:N_HEADS]


def _full_spec(shape, single_buffer=False):
    nd = len(shape)
    if single_buffer:
        return pl.BlockSpec(shape, lambda *_: (0,) * nd, pipeline_mode=pl.Buffered(1))
    return pl.BlockSpec(shape, lambda *_: (0,) * nd)


def _prefill_layer(layer, x, win, wout, cos, sin, convw, convb, bif, mnw, rnw, hnw, hlb, lng, lnb, tril, lv, TB):
    B, T, _ = x.shape
    n_tb = T // TB
    n_chunks = TB // CHUNK
    kern = functools.partial(_prefill_kernel, layer, TB)
    in_specs = [
        pl.BlockSpec((1, TB, D_MODEL), lambda b, j: (b, j, 0)),
        _full_spec(win.shape, True), _full_spec(wout.shape, True),
        pl.BlockSpec((TB, LANES), lambda b, j: (j, 0)),
        pl.BlockSpec((TB, LANES), lambda b, j: (j, 0)),
        _full_spec(convw.shape), _full_spec(convb.shape), _full_spec(bif.shape),
        _full_spec(mnw.shape), _full_spec(rnw.shape), _full_spec(hnw.shape), _full_spec(hlb.shape),
        _full_spec(lng.shape), _full_spec(lnb.shape), _full_spec(tril.shape), _full_spec(lv.shape),
    ]
    out_shape = (
        jax.ShapeDtypeStruct((B, T, D_MODEL), F32),
        jax.ShapeDtypeStruct((B, N_HEADS, M_DH, M_DH), F32),
        jax.ShapeDtypeStruct((B, N_HEADS, M_DH), F32),
        jax.ShapeDtypeStruct((B, 1, N_HEADS), F32),
        jax.ShapeDtypeStruct((B, CONV_W - 1, 2 * HP), F32),
        jax.ShapeDtypeStruct((B, N_HEADS, R_DH, R_DH), F32),
        jax.ShapeDtypeStruct((B, N_HEADS, H_DK, H_DK), F32),
    )
    out_specs = (
        pl.BlockSpec((1, TB, D_MODEL), lambda b, j: (b, j, 0)),
        pl.BlockSpec((1, N_HEADS, M_DH, M_DH), lambda b, j: (b, 0, 0, 0)),
        pl.BlockSpec((1, N_HEADS, M_DH), lambda b, j: (b, 0, 0)),
        pl.BlockSpec((1, 1, N_HEADS), lambda b, j: (b, 0, 0)),
        pl.BlockSpec((1, CONV_W - 1, 2 * HP), lambda b, j: (b, 0, 0)),
        pl.BlockSpec((1, N_HEADS, R_DH, R_DH), lambda b, j: (b, 0, 0, 0)),
        pl.BlockSpec((1, N_HEADS, H_DK, H_DK), lambda b, j: (b, 0, 0, 0)),
    )
    L = CHUNK
    scratch = [
        pltpu.VMEM((TB, NP_SCR), F32),
        pltpu.VMEM((TB + SUBLANES, 2 * HP), F32),
        pltpu.VMEM((TB, MIXP), F32),
        pltpu.VMEM((TB, LANES), F32),
        pltpu.VMEM((TB, LANES), F32),
        pltpu.VMEM((N_HEADS, L, L), BF16),
        pltpu.VMEM((N_HEADS, LANES, L), BF16),
        pltpu.VMEM((N_LEVELS + 1, 2, L, H_WIDTH), BF16),
        pltpu.VMEM((N_LEVELS + 1, L, H_WIDTH), BF16),
        pltpu.VMEM((TB, H_WIDTH), BF16),
        pltpu.VMEM((max(n_chunks, SUBLANES), H_WIDTH), F32),
        pltpu.VMEM((n_chunks, N_HEADS, LANES, LANES), F32),
        pltpu.VMEM((n_chunks, N_HEADS, LANES, LANES), F32),
        pltpu.VMEM((n_chunks, N_HEADS // 2, LANES, LANES), F32),
        pltpu.VMEM((N_HEADS, LANES, LANES), F32),
        pltpu.VMEM((N_HEADS, LANES, LANES), F32),
        pltpu.VMEM((N_HEADS // 2, LANES, LANES), F32),
        pltpu.VMEM((SUBLANES, LANES), F32),
        pltpu.VMEM((N_HEADS, L, L), F32),
        pltpu.VMEM((2 * L, HP), F32),
    ]
    return pl.pallas_call(
        kern, grid=(B, n_tb), in_specs=in_specs, out_specs=out_specs, out_shape=out_shape,
        scratch_shapes=scratch,
        compiler_params=pltpu.CompilerParams(dimension_semantics=("arbitrary", "arbitrary"),
                                             vmem_limit_bytes=VMEM_LIMIT),
        name=f"prefill_layer{layer}",
    )(x, win, wout, cos, sin, convw, convb, bif, mnw, rnw, hnw, hlb, lng, lnb, tril, lv)


DEC_KB = 4
KR_M = M_DH // DEC_KB
KR_H = H_DK // DEC_KB
G_DEC, G_WS, G_EN, G_GAM, G_SM, G_QN, G_SR, G_SH = (i * N_HEADS for i in range(8))


def _stream_state(q_blk, dec, kw_blk, v, in_ref, out_ref, nrows):
    acc = jnp.zeros_like(v)
    for kk in range(nrows):
        st = in_ref[0, 0, kk]
        d = dec if dec.shape[0] == 1 else dec[kk:kk + 1, :]
        acc = acc + q_blk[kk:kk + 1, :] * st
        out_ref[0, 0, kk] = d * st + kw_blk[kk:kk + 1, :] * v
    return acc


def _decode_kernel(x_ref, wt_ref, wout_ref, cos_ref, sin_ref, convw_ref, convb_ref, bif_ref,
                   mnw_ref, rnw_ref, hnw_ref, hlbt_ref, lng_ref, lnb_ref,
                   cin_ref, nin_ref, min_ref, convin_ref, rin_ref, sin_st_ref,
                   y_ref, cout_ref, nout_ref, mout_ref, convout_ref, rout_ref, sout_ref,
                   p_ref, xs_ref, ht_ref, mix_ref, g_ref):
    l = pl.program_id(0)
    h = pl.program_id(1)
    kb = pl.program_id(2)
    first = jnp.logical_and(h == 0, kb == 0)
    last = jnp.logical_and(h == N_HEADS - 1, kb == DEC_KB - 1)
    B = x_ref.shape[0]

    @pl.when(jnp.logical_and(first, l == 0))
    def _load_x():
        xs_ref[...] = x_ref[...]

    @pl.when(first)
    def _project():
        p_ref[0:N_IN, :] = _dot_nt(wt_ref[0], xs_ref[...].astype(BF16))
        ht_ref[...] = jnp.zeros_like(ht_ref)

        pre = p_ref[O_MQ:O_MQ + 2 * M_WIDTH, :].T
        cw = convw_ref[0]
        acc = convb_ref[0] + cw[CONV_W - 1:CONV_W, :] * pre
        for jj in range(CONV_W - 1):
            acc = acc + cw[jj:jj + 1, :] * convin_ref[0, jj]
        for jj in range(CONV_W - 2):
            convout_ref[0, jj] = convin_ref[0, jj + 1]
        convout_ref[0, CONV_W - 2] = pre
        p_ref[O_MQ:O_MQ + 2 * M_WIDTH, :] = _silu(acc).T

        gt = p_ref[O_IF:O_IF + 2 * N_HEADS, :] + bif_ref[0]
        ig = gt[0:N_HEADS]
        lf = _log_sigmoid(gt[N_HEADS:2 * N_HEADS])
        m_prev = min_ref[0]
        m_t = jnp.maximum(ig, lf + m_prev)
        dec = jnp.exp(lf + m_prev - m_t)
        w_s = jnp.exp(ig - m_t)
        mout_ref[0] = m_t
        g_ref[G_DEC:G_DEC + N_HEADS, :] = dec
        g_ref[G_WS:G_WS + N_HEADS, :] = w_s
        g_ref[G_EN:G_EN + N_HEADS, :] = jnp.exp(-m_t)
        for hh in range(N_HEADS):
            q = p_ref[O_MQ + hh * M_DH:O_MQ + (hh + 1) * M_DH, :]
            k = p_ref[O_MK + hh * M_DH:O_MK + (hh + 1) * M_DH, :] * (M_DH ** -0.5)
            n_old = nin_ref[0, hh]
            d_h = dec[hh:hh + 1]
            kw = k * w_s[hh:hh + 1]
            g_ref[G_SM + hh:G_SM + hh + 1, :] = jnp.sum(q * k, axis=0, keepdims=True) * w_s[hh:hh + 1]
            g_ref[G_QN + hh:G_QN + hh + 1, :] = jnp.sum(q * n_old, axis=0, keepdims=True)
            nout_ref[0, hh] = d_h * n_old + kw
            p_ref[O_MK + hh * M_DH:O_MK + (hh + 1) * M_DH, :] = kw

        cos = cos_ref[...]
        sin = sin_ref[...]
        for hh in range(N_HEADS):
            g_ref[G_GAM + hh:G_GAM + hh + 1, :] = jnp.full((1, B), math.exp(LG[hh]), F32)
            rot = []
            for off, scale in ((O_RQ, 1.0), (O_RK, R_DH ** -0.5)):
                x1 = p_ref[off + hh * R_DH:off + hh * R_DH + ROPE_HALF, :]
                x2 = p_ref[off + hh * R_DH + ROPE_HALF:off + (hh + 1) * R_DH, :]
                r1 = (x1 * cos - x2 * sin) * scale
                r2 = (x1 * sin + x2 * cos) * scale
                p_ref[off + hh * R_DH:off + hh * R_DH + ROPE_HALF, :] = r1
                p_ref[off + hh * R_DH + ROPE_HALF:off + (hh + 1) * R_DH, :] = r2
                rot.append((r1, r2))
            (q1, q2), (k1, k2) = rot
            g_ref[G_SR + hh:G_SR + hh + 1, :] = (jnp.sum(q1 * k1, axis=0, keepdims=True)
                                                 + jnp.sum(q2 * k2, axis=0, keepdims=True))

        lbs = _hgrn_lower_bounds(hlbt_ref[...], 1)
        lb = lbs[0]
        for i in range(1, DEPTH):
            lb = jnp.where(l == i, lbs[i], lb)
        lfh, kh = _hgrn_gates(p_ref[O_HF:O_HF + H_WIDTH, :], lb)
        eh = jnp.exp(lfh)
        qh = p_ref[O_HQ:O_HQ + H_WIDTH, :]
        p_ref[O_HF:O_HF + H_WIDTH, :] = eh
        p_ref[O_HK:O_HK + H_WIDTH, :] = kh
        p_ref[O_HQ:O_HQ + H_WIDTH, :] = qh * eh
        qk = qh * kh
        for hh in range(N_HEADS):
            g_ref[G_SH + hh:G_SH + hh + 1, :] = jnp.sum(qk[hh * H_DK:(hh + 1) * H_DK], axis=0, keepdims=True)

    def rows(base, width, n):
        return pl.ds(pl.multiple_of(base + h * width + kb * n, SUBLANES), n)

    dec = g_ref[pl.ds(G_DEC + h, 1), :]
    acc = _stream_state(p_ref[rows(O_MQ, M_DH, KR_M), :], dec, p_ref[rows(O_MK, M_DH, KR_M), :],
                        p_ref[pl.ds(pl.multiple_of(O_MV + h * M_DH, SUBLANES), M_DH), :], cin_ref, cout_ref, KR_M)
    ht_ref[pl.ds(pl.multiple_of(h * M_DH, SUBLANES), M_DH), :] += acc

    gam = g_ref[pl.ds(G_GAM + h, 1), :]
    acc = _stream_state(p_ref[rows(O_RQ, R_DH, KR_M), :], gam, p_ref[rows(O_RK, R_DH, KR_M), :],
                        p_ref[pl.ds(pl.multiple_of(O_RV + h * R_DH, SUBLANES), R_DH), :], rin_ref, rout_ref, KR_M)
    ht_ref[pl.ds(pl.multiple_of(M_WIDTH + h * R_DH, SUBLANES), R_DH), :] += acc

    acc = _stream_state(p_ref[rows(O_HQ, H_DK, KR_H), :], p_ref[rows(O_HF, H_DK, KR_H), :],
                        p_ref[rows(O_HK, H_DK, KR_H), :],
                        p_ref[pl.ds(pl.multiple_of(O_HI + h * H_DK, SUBLANES), H_DK), :], sin_st_ref, sout_ref, KR_H)
    ht_ref[pl.ds(pl.multiple_of(M_WIDTH + R_WIDTH + h * H_DK, SUBLANES), H_DK), :] += acc

    @pl.when(last)
    def _finish():
        def head_ln(a):
            mu = jnp.mean(a, axis=0, keepdims=True)
            d = a - mu
            return d * lax.rsqrt(jnp.mean(d * d, axis=0, keepdims=True) + HEAD_EPS)

        for hh in range(N_HEADS):
            sl = slice(hh * M_DH, (hh + 1) * M_DH)
            v = p_ref[O_MV + hh * M_DH:O_MV + (hh + 1) * M_DH, :]
            d_h = g_ref[G_DEC + hh:G_DEC + hh + 1, :]
            s = g_ref[G_SM + hh:G_SM + hh + 1, :]
            num = s * v + ht_ref[sl, :] * d_h
            den = s + g_ref[G_QN + hh:G_QN + hh + 1, :] * d_h
            hm = num / jnp.maximum(jnp.abs(den), g_ref[G_EN + hh:G_EN + hh + 1, :])
            o = p_ref[O_MO + hh * M_DH:O_MO + (hh + 1) * M_DH, :]
            z = p_ref[O_MZ + hh * M_DH:O_MZ + (hh + 1) * M_DH, :]
            mix_ref[sl, :] = head_ln(hm) * mnw_ref[0, sl, :] * _sigmoid(o) * _silu(z)
        for hh in range(N_HEADS):
            sl = slice(hh * R_DH, (hh + 1) * R_DH)
            v = p_ref[O_RV + hh * R_DH:O_RV + (hh + 1) * R_DH, :]
            hr = (g_ref[G_SR + hh:G_SR + hh + 1, :] * v
                  + ht_ref[M_WIDTH + hh * R_DH:M_WIDTH + (hh + 1) * R_DH, :] * math.exp(LG[hh]))
            gg = p_ref[O_RG + hh * R_DH:O_RG + (hh + 1) * R_DH, :]
            mix_ref[M_WIDTH + hh * R_DH:M_WIDTH + (hh + 1) * R_DH, :] = head_ln(hr) * rnw_ref[0, sl, :] * _silu(gg)
        for hh in range(N_HEADS):
            sl = slice(hh * H_DK, (hh + 1) * H_DK)
            base = M_WIDTH + R_WIDTH
            v = p_ref[O_HI + hh * H_DK:O_HI + (hh + 1) * H_DK, :]
            ho = g_ref[G_SH + hh:G_SH + hh + 1, :] * v + ht_ref[base + hh * H_DK:base + (hh + 1) * H_DK, :]
            hn = ho * lax.rsqrt(jnp.mean(ho * ho, axis=0, keepdims=True) + HEAD_EPS)
            gg = p_ref[O_HG + hh * H_DK:O_HG + (hh + 1) * H_DK, :]
            mix_ref[base + hh * H_DK:base + (hh + 1) * H_DK, :] = hn * hnw_ref[0, sl, :] * _silu(gg)
        x = xs_ref[...]
        out = _dot(mix_ref[...].T.astype(BF16), wout_ref[0])
        y = _layer_norm_rows(ALPHA * x + out, lng_ref[0], lnb_ref[0])
        xs_ref[...] = y
        y_ref[0] = y


def _decode(x, wt, wout, cos, sin, convw, convb, bif, mnw, rnw, hnw, hlbt, lng, lnb, st_c, st_n, st_m, st_conv,
            st_r, st_s):
    B = x.shape[0]

    def lspec(a):
        nd = a.ndim
        return pl.BlockSpec((1,) + a.shape[1:], lambda l, h, kb: (l,) + (0,) * (nd - 1))

    def kv_spec(a, n):
        return pl.BlockSpec((1, 1, n) + a.shape[3:], lambda l, h, kb: (l, h, kb, 0, 0))

    in_specs = [
        _full_spec(x.shape), lspec(wt), lspec(wout), _full_spec(cos.shape), _full_spec(sin.shape),
        lspec(convw), lspec(convb), lspec(bif), lspec(mnw), lspec(rnw), lspec(hnw), _full_spec(hlbt.shape),
        lspec(lng), lspec(lnb),
        kv_spec(st_c, KR_M), lspec(st_n), lspec(st_m), lspec(st_conv), kv_spec(st_r, KR_M), kv_spec(st_s, KR_H),
    ]
    out_shape = (
        jax.ShapeDtypeStruct((DEPTH, B, D_MODEL), F32),
        jax.ShapeDtypeStruct(st_c.shape, F32), jax.ShapeDtypeStruct(st_n.shape, F32),
        jax.ShapeDtypeStruct(st_m.shape, F32), jax.ShapeDtypeStruct(st_conv.shape, F32),
        jax.ShapeDtypeStruct(st_r.shape, F32), jax.ShapeDtypeStruct(st_s.shape, F32),
    )
    out_specs = (
        pl.BlockSpec((1, B, D_MODEL), lambda l, h, kb: (l, 0, 0)),
        kv_spec(st_c, KR_M), lspec(st_n), lspec(st_m), lspec(st_conv), kv_spec(st_r, KR_M), kv_spec(st_s, KR_H),
    )
    scratch = [
        pltpu.VMEM((N_IN + H_WIDTH, B), F32),
        pltpu.VMEM((B, D_MODEL), F32),
        pltpu.VMEM((D_MIX, B), F32),
        pltpu.VMEM((D_MIX, B), F32),
        pltpu.VMEM((8 * N_HEADS, B), F32),
    ]
    return pl.pallas_call(
        _decode_kernel, grid=(DEPTH, N_HEADS, DEC_KB), in_specs=in_specs, out_specs=out_specs,
        out_shape=out_shape, scratch_shapes=scratch,
        compiler_params=pltpu.CompilerParams(dimension_semantics=("arbitrary", "arbitrary", "arbitrary"),
                                             vmem_limit_bytes=VMEM_LIMIT),
        name="decode_step",
    )(x, wt, wout, cos, sin, convw, convb, bif, mnw, rnw, hnw, hlbt, lng, lnb,
      st_c, st_n, st_m, st_conv, st_r, st_s)


def _rope_angles(pos):
    inv = ROPE_BASE ** (-jnp.arange(ROPE_HALF, dtype=F32) / ROPE_HALF)
    ang = pos.astype(F32)[:, None] * inv[None, :]
    return jnp.cos(ang), jnp.sin(ang)


PREFILL_BLOCK = 512


def kernel(x_prompt, x_sample, state_mlstm_C, state_mlstm_n, state_mlstm_m, state_mlstm_conv, state_ret, state_hgrn, w_in, conv_w, conv_b, b_mgate, m_norm_w, r_norm_w, h_norm_w, hgrn_lb, w_out, ln_g, ln_b):
    B, T, _ = x_prompt.shape

    wt = jnp.swapaxes(w_in, 1, 2).astype(BF16)
    sec = lambda o, n: wt[:, o:o + n]
    win_p = jnp.concatenate(
        [_pad_heads(sec(o, M_WIDTH), 1) for o in (O_MQ, O_MK, O_MV, O_MO, O_MZ)]
        + [_pad_rope_heads(sec(O_RQ, R_WIDTH), 1), _pad_rope_heads(sec(O_RK, R_WIDTH), 1),
           _pad_heads(sec(O_RV, R_WIDTH), 1), _pad_heads(sec(O_RG, R_WIDTH), 1),
           sec(O_HF, 4 * H_WIDTH),
           jnp.pad(sec(O_IF, 2 * N_HEADS), ((0, 0), (0, LANES - 2 * N_HEADS), (0, 0)))], axis=1)
    wout_b = w_out.astype(BF16)
    wout_p = jnp.concatenate([_pad_heads(wout_b[:, 0:M_WIDTH], 1), _pad_heads(wout_b[:, M_WIDTH:M_WIDTH + R_WIDTH], 1),
                              wout_b[:, M_WIDTH + R_WIDTH:]], axis=1)
    convw_p = jnp.concatenate([_pad_heads(conv_w[..., :M_WIDTH], 2), _pad_heads(conv_w[..., M_WIDTH:], 2)], axis=-1)
    convb_p = jnp.concatenate([_pad_heads(conv_b[..., :M_WIDTH], 1), _pad_heads(conv_b[..., M_WIDTH:], 1)],
                              axis=-1)[:, None, :]
    bif_p = jnp.pad(b_mgate, ((0, 0), (0, LANES - 2 * N_HEADS)))[:, None, :]
    mnw_p = _pad_heads(m_norm_w, 1)[:, None, :]
    rnw_p = _pad_heads(r_norm_w, 1)[:, None, :]
    hnw_p = h_norm_w[:, None, :]
    lng_p = ln_g[:, None, :]
    lnb_p = ln_b[:, None, :]
    hlb = hgrn_lb.astype(F32)

    tril_np, lv_np = _level_tables(CHUNK)
    tril = jnp.asarray(tril_np, BF16)
    lv = jnp.asarray(lv_np)

    c, s = _rope_angles(jnp.arange(T, dtype=jnp.int32))
    z = jnp.zeros((T, LANES // 2 - ROPE_HALF), F32)
    cos_p = jnp.concatenate([c, z, c, z], axis=1)
    sin_p = jnp.concatenate([-s, z, s, z], axis=1)

    hp = x_prompt
    st = [[] for _ in range(6)]
    for l in range(DEPTH):
        outs = _prefill_layer(l, hp, win_p[l], wout_p[l], cos_p, sin_p, convw_p[l], convb_p[l], bif_p[l],
                              mnw_p[l], rnw_p[l], hnw_p[l], hlb, lng_p[l], lnb_p[l], tril, lv, PREFILL_BLOCK)
        hp = outs[0]
        for k in range(6):
            st[k].append(outs[1 + k])
    mC_p = jnp.stack(st[0])
    mn_p = jnp.stack(st[1])
    mm_p = jnp.stack(st[2])[:, :, 0, :]
    conv_p = _unpad_heads(jnp.stack(st[3]))
    ret_p = jnp.stack(st[4])
    hgrn_p = jnp.stack(st[5])

    n_s = x_sample.shape[0]
    cs, ss = _rope_angles(PAST_LEN + jnp.arange(x_sample.shape[1], dtype=jnp.int32))
    cos_s = jnp.broadcast_to(cs[0][:, None], (ROPE_HALF, n_s))
    sin_s = jnp.broadcast_to(ss[0][:, None], (ROPE_HALF, n_s))
    ys, c_t, n_t, m_t, conv_t, r_t, s_t = _decode(
        x_sample[:, 0, :], wt, wout_b, cos_s, sin_s, conv_w, conv_b[:, None, :], b_mgate[:, :, None],
        m_norm_w[:, :, None], r_norm_w[:, :, None], h_norm_w[:, :, None], hlb.T, lng_p, lnb_p,
        jnp.transpose(state_mlstm_C, (0, 2, 3, 4, 1)), jnp.transpose(state_mlstm_n, (0, 2, 3, 1)),
        jnp.transpose(state_mlstm_m, (0, 2, 1)), jnp.transpose(state_mlstm_conv, (0, 2, 1, 3)),
        jnp.transpose(state_ret, (0, 2, 3, 4, 1)), jnp.transpose(state_hgrn, (0, 2, 3, 4, 1)))
    hs = ys[DEPTH - 1][:, None, :]
    mC_s = jnp.transpose(c_t, (0, 4, 1, 2, 3))
    mn_s = jnp.transpose(n_t, (0, 3, 1, 2))
    mm_s = jnp.transpose(m_t, (0, 2, 1))
    conv_s = jnp.transpose(conv_t, (0, 2, 1, 3))
    ret_s = jnp.transpose(r_t, (0, 4, 1, 2, 3))
    hgrn_s = jnp.transpose(s_t, (0, 4, 1, 2, 3))

    return (hp, hs, mC_p, mn_p, mm_p, conv_p, ret_p, hgrn_p, mC_s, mn_s, mm_s, conv_s, ret_s, hgrn_s)
```

```python
import functools
import math

import numpy as np
import jax
import jax.numpy as jnp
from jax import lax
from jax.experimental import pallas as pl
from jax.experimental.pallas import tpu as pltpu

F32 = jnp.float32
BF16 = jnp.bfloat16

D_MODEL = 1024
DEPTH = 2
PAST_LEN = 16384
N_HEADS = 4
M_DH = 96
R_DH = 96
H_DK = 64
M_WIDTH = N_HEADS * M_DH
R_WIDTH = N_HEADS * R_DH
H_WIDTH = N_HEADS * H_DK
D_MIX = M_WIDTH + R_WIDTH + H_WIDTH
CONV_W = 4
ROPE_BASE = 10000.0
LN_EPS = 1e-5
HEAD_EPS = 1e-6
ALPHA = (2 * DEPTH) ** 0.25

LANES = 128
SUBLANES = 8
HP = N_HEADS * LANES
ROPE_HALF = R_DH // 2
VMEM_LIMIT = 56 * 1024 * 1024

O_MQ, O_MK, O_MV, O_MO, O_MZ = 0, M_WIDTH, 2 * M_WIDTH, 3 * M_WIDTH, 4 * M_WIDTH
O_IF = 5 * M_WIDTH
O_RQ = O_IF + 2 * N_HEADS
O_RK, O_RV, O_RG = O_RQ + R_WIDTH, O_RQ + 2 * R_WIDTH, O_RQ + 3 * R_WIDTH
O_HF = O_RQ + 4 * R_WIDTH
O_HI, O_HQ, O_HG = O_HF + H_WIDTH, O_HF + 2 * H_WIDTH, O_HF + 3 * H_WIDTH
N_IN = O_HF + 4 * H_WIDTH
O_HK = N_IN

C_MQ, C_MK, C_MV, C_MO, C_MZ = 0, HP, 2 * HP, 3 * HP, 4 * HP
C_RQ, C_RK, C_RV, C_RG = 5 * HP, 6 * HP, 7 * HP, 8 * HP
C_HF = 9 * HP
C_HI = C_HF + H_WIDTH
C_HQ = C_HI + H_WIDTH
C_HG = C_HQ + H_WIDTH
C_IF = C_HG + H_WIDTH
NP_IN = C_IF + LANES
C_HK = NP_IN
NP_SCR = NP_IN + H_WIDTH
MIXP = 2 * HP + H_WIDTH

CHUNK = 128
N_LEVELS = int(math.log2(CHUNK))
LG = [math.log1p(-2.0 ** (-5.0 - h)) for h in range(N_HEADS)]


def _pad_heads(a, axis, dh=M_DH):
    shp = a.shape
    a = a.reshape(shp[:axis] + (N_HEADS, dh) + shp[axis + 1:])
    pad = [(0, 0)] * a.ndim
    pad[axis + 1] = (0, LANES - dh)
    return jnp.pad(a, pad).reshape(shp[:axis] + (HP,) + shp[axis + 1:])


def _pad_rope_heads(a, axis):
    shp = a.shape
    a = a.reshape(shp[:axis] + (N_HEADS, 2, ROPE_HALF) + shp[axis + 1:])
    pad = [(0, 0)] * a.ndim
    pad[axis + 2] = (0, LANES // 2 - ROPE_HALF)
    return jnp.pad(a, pad).reshape(shp[:axis] + (HP,) + shp[axis + 1:])


def _unpad_heads(a, dh=M_DH):
    shp = a.shape
    return a.reshape(shp[:-1] + (shp[-1] // LANES, LANES))[..., :dh].reshape(shp[:-1] + (shp[-1] // LANES * dh,))


def _level_table(L):
    nl = int(math.log2(L))
    tt = np.arange(L)[:, None]
    ss = np.arange(L)[None, :]
    lv = np.full((L, L), -1, np.int32)
    for n in range(nl):
        w = L >> (n + 1)
        same = (tt // (2 * w)) == (ss // (2 * w))
        cond = same & ((tt % (2 * w)) >= w) & ((ss % (2 * w)) < w)
        lv = np.where(cond, n, lv)
    lv = np.where(tt == ss, nl, lv)
    return lv.astype(np.int32)


def _dot(a, b):
    return jnp.dot(a, b, preferred_element_type=F32)


def _dot_nt(a, b):
    return lax.dot_general(a, b, (((1,), (1,)), ((), ())), preferred_element_type=F32)


def _sigmoid(x):
    return 1.0 / (1.0 + jnp.exp(-x))


def _silu(x):
    return x * _sigmoid(x)


def _log_sigmoid(x):
    return jnp.minimum(x, 0.0) - jnp.log1p(jnp.exp(-jnp.abs(x)))


def _lane(shape):
    return lax.broadcasted_iota(jnp.int32, shape, len(shape) - 1)


def _hgrn_lower_bounds(hlb, layer_axis):
    def take(a, l):
        return a[l:l + 1] if layer_axis == 0 else a[:, l:l + 1]
    mx = jnp.max(hlb, axis=layer_axis, keepdims=True)
    e = jnp.exp(hlb - mx)
    p = e / jnp.sum(e, axis=layer_axis, keepdims=True)
    out = []
    acc = None
    for l in range(DEPTH):
        acc = take(p, l) if acc is None else acc + take(p, l)
        out.append(acc - take(p, 0))
    return out


def _hgrn_gates(fpre, lb):
    a = jnp.log(lb)
    c = jnp.log1p(-lb) + _log_sigmoid(fpre)
    mx = jnp.maximum(a, c)
    lf = mx + jnp.log(jnp.exp(a - mx) + jnp.exp(c - mx))
    kh = (1.0 - lb) * _sigmoid(-fpre)
    return lf, kh


def _cumsum_rows(x):
    L, W = x.shape
    sub = lax.broadcasted_iota(jnp.int32, (SUBLANES, W), 0)
    out = []
    carry = None
    for r in range(L // SUBLANES):
        t = x[r * SUBLANES:(r + 1) * SUBLANES, :]
        s = 1
        while s < SUBLANES:
            t = t + jnp.where(sub >= s, pltpu.roll(t, s, 0), 0.0)
            s *= 2
        if carry is not None:
            t = t + carry
        carry = t[SUBLANES - 1:SUBLANES, :]
        out.append(t)
    return jnp.concatenate(out, axis=0)


def _mid_row_broadcast(bc, w):
    L, W = bc.shape
    if 2 * w >= SUBLANES:
        blocks = [jnp.broadcast_to(bc[b * 2 * w + w:b * 2 * w + w + 1, :], (2 * w, W)) for b in range(L // (2 * w))]
        return blocks[0] if len(blocks) == 1 else jnp.concatenate(blocks, axis=0)
    x3 = bc.reshape(L // SUBLANES, SUBLANES, W)
    sub = lax.broadcasted_iota(jnp.int32, x3.shape, 1)
    out = None
    for b in range(SUBLANES // (2 * w)):
        piece = jnp.broadcast_to(x3[:, b * 2 * w + w:b * 2 * w + w + 1, :], x3.shape)
        out = piece if out is None else jnp.where(sub >= b * 2 * w, piece, out)
    return out.reshape(L, W)


def _layer_norm_rows(r, g, b):
    mu = jnp.mean(r, axis=-1, keepdims=True)
    d = r - mu
    var = jnp.mean(d * d, axis=-1, keepdims=True)
    return d * lax.rsqrt(var + LN_EPS) * g + b


def _head_layer_norm_padded(h, valid):
    dh = M_DH
    hz = jnp.where(valid, h, 0.0)
    mu = jnp.sum(hz, axis=-1, keepdims=True) * (1.0 / dh)
    d = jnp.where(valid, h - mu, 0.0)
    var = jnp.sum(d * d, axis=-1, keepdims=True) * (1.0 / dh)
    return d * lax.rsqrt(var + HEAD_EPS)


def _pair_rms_norm(h, low):
    sq = h * h
    m0 = jnp.sum(jnp.where(low, sq, 0.0), axis=-1, keepdims=True) * (1.0 / H_DK)
    m1 = jnp.sum(jnp.where(low, 0.0, sq), axis=-1, keepdims=True) * (1.0 / H_DK)
    return h * jnp.where(low, lax.rsqrt(m0 + HEAD_EPS), lax.rsqrt(m1 + HEAD_EPS))


def _prefill_kernel(layer, TB,
                    x_ref, win_ref, wout_ref, cos_ref, sin_ref, convw_ref, convb_ref, bif_ref,
                    mnw_ref, rnw_ref, hnw_ref, hlb_ref, lng_ref, lnb_ref, lv_ref,
                    y_ref, cout_ref, nout_ref, mout_ref, convout_ref, rout_ref, sout_ref,
                    p_ref, pre_ref, ho_ref, cum_ref, ml_ref, pm_ref, kw_ref, qt_ref, kt_ref, qi_ref, dh_ref,
                    u_ref, ur_ref, uh_ref, c_ref, r_ref, s_ref, m_ref, dm_ref, dec_ref,
                    xb_ref, it_ref, gr_ref, gw_ref, ge_ref):
    L = CHUNK
    NL = N_LEVELS
    n_chunks = TB // L
    j = pl.program_id(1)
    n_tb = pl.num_programs(1)

    @pl.when(j == 0)
    def _init():
        c_ref[...] = jnp.zeros_like(c_ref)
        r_ref[...] = jnp.zeros_like(r_ref)
        s_ref[...] = jnp.zeros_like(s_ref)
        m_ref[...] = jnp.zeros_like(m_ref)
        pre_ref[0:SUBLANES, :] = jnp.zeros((SUBLANES, 2 * HP), F32)
        ti = lax.broadcasted_iota(jnp.int32, (L, L), 0)
        si = lax.broadcasted_iota(jnp.int32, (L, L), 1)
        dist = (ti - si).astype(F32)
        row = lax.broadcasted_iota(jnp.int32, (L, LANES), 0).astype(F32)
        for h in range(N_HEADS):
            dm_ref[h] = jnp.where(ti >= si, jnp.exp(dist * LG[h]), 0.0)
            dec_ref[0:L, h * LANES:(h + 1) * LANES] = jnp.exp((row + 1.0) * LG[h])
            dec_ref[L:2 * L, h * LANES:(h + 1) * LANES] = jnp.exp((L - 1.0 - row) * LG[h])

    xb_ref[...] = x_ref[0].astype(BF16)
    lb = _hgrn_lower_bounds(hlb_ref[...], 0)[layer]
    PIECE = 2 * LANES

    def project(c0):
        c1 = min(c0 + PIECE, NP_IN)
        res = _dot_nt(xb_ref[...], win_ref[c0:c1, :])
        if c0 < C_MV:
            pre_ref[SUBLANES:SUBLANES + TB, c0:c1] = res
        else:
            p_ref[:, c0:c1] = res

    def gate_math(r0, n):
        rows = slice(r0, r0 + n)
        lf, kh = _hgrn_gates(p_ref[rows, C_HF:C_HF + H_WIDTH], lb)
        p_ref[rows, C_HF:C_HF + H_WIDTH] = lf
        p_ref[rows, C_HK:C_HK + H_WIDTH] = kh
        g = p_ref[rows, C_IF:C_IF + LANES] + bif_ref[...]
        p_ref[rows, C_IF:C_IF + LANES] = jnp.where(_lane((n, LANES)) < N_HEADS, g, _log_sigmoid(g))

    def conv_math(half, r0, n):
        cols = slice(half * HP, (half + 1) * HP)
        lo = SUBLANES + r0
        acc = convb_ref[:, cols] + convw_ref[CONV_W - 1:CONV_W, cols] * pre_ref[lo:lo + n, cols]
        for t in range(1, CONV_W):
            acc = acc + convw_ref[CONV_W - 1 - t:CONV_W - t, cols] * pre_ref[lo - t:lo - t + n, cols]
        scale = 1.0 if half == 0 else M_DH ** -0.5
        p_ref[r0:r0 + n, C_MQ + half * HP:C_MQ + (half + 1) * HP] = _silu(acc) * scale

    def rope_math(col, scale, r0, n):
        rows = slice(r0, r0 + n)
        cos = cos_ref[rows, :]
        sin = sin_ref[rows, :]
        for h in range(N_HEADS):
            sl = slice(col + h * LANES, col + (h + 1) * LANES)
            v = p_ref[rows, sl]
            p_ref[rows, sl] = (v * cos + pltpu.roll(v, LANES // 2, 1) * sin) * scale

    def value_math(r0, n):
        one96 = jnp.where(_lane((n, LANES)) == M_DH, 1.0, 0.0)
        for h in range(N_HEADS):
            sl = slice(C_MV + h * LANES, C_MV + (h + 1) * LANES)
            p_ref[r0:r0 + n, sl] = p_ref[r0:r0 + n, sl] + one96

    order = ([C_HF, C_HF + 4 * H_WIDTH] + list(range(C_MQ, C_MV, PIECE)) + list(range(C_RQ, C_RV, PIECE))
             + list(range(C_MV, C_MO, PIECE)) + list(range(C_HF + PIECE, C_HF + 4 * H_WIDTH, PIECE))
             + list(range(C_MO, C_RQ, PIECE)) + list(range(C_RV, C_HF, PIECE)))
    assert sorted(order) == list(range(0, NP_IN, PIECE))
    pos = {c0: i for i, c0 in enumerate(order)}
    vec = []
    for r0 in range(0, TB, 64):
        vec.append((pos[C_HF + 4 * H_WIDTH], functools.partial(gate_math, r0, 64)))
    for half in range(2):
        for r0 in range(0, TB, 32):
            vec.append((pos[C_MQ + half * HP + HP - PIECE], functools.partial(conv_math, half, r0, 32)))
    for col, scale in ((C_RQ, 1.0), (C_RK, R_DH ** -0.5)):
        for r0 in range(0, TB, 64):
            vec.append((pos[col + HP - PIECE], functools.partial(rope_math, col, scale, r0, 64)))
    for r0 in range(0, TB, 128):
        vec.append((pos[C_MV + HP - PIECE], functools.partial(value_math, r0, 128)))
    per_piece = -(-len(vec) // (len(order) - 6))
    vi = 0
    for i, c0 in enumerate(order):
        project(c0)
        done = 0
        while vi < len(vec) and vec[vi][0] < i and done < per_piece:
            vec[vi][1]()
            vi += 1
            done += 1
    while vi < len(vec):
        vec[vi][1]()
        vi += 1
    pre_ref[SUBLANES - (CONV_W - 1):SUBLANES, :] = pre_ref[TB + SUBLANES - (CONV_W - 1):TB + SUBLANES, :]

    tril = lax.broadcasted_iota(jnp.int32, (L, L), 0) >= lax.broadcasted_iota(jnp.int32, (L, L), 1)
    bd = (lax.broadcasted_iota(jnp.int32, (LANES, LANES), 0) < H_DK) == (_lane((LANES, LANES)) < H_DK)
    low = _lane((L, LANES)) < H_DK
    lane = _lane((L, LANES))

    def intra(c, carry):
        rows = pl.ds(pl.multiple_of(c * L, L), L)

        ga = p_ref[rows, C_IF:C_IF + LANES]
        cum = _cumsum_rows(ga)
        bc = _cumsum_rows(p_ref[rows, C_HF:C_HF + H_WIDTH])

        def head(col, h, dt=None):
            a = p_ref[rows, col + h * LANES:col + (h + 1) * LANES]
            return a if dt is None else a.astype(dt)

        qk_r = [_dot_nt(head(C_RQ, h, BF16), head(C_RK, h, BF16)) for h in range(N_HEADS)]
        qk_m = [_dot_nt(head(C_MQ, h, BF16), head(C_MK, h, BF16)) for h in range(N_HEADS)]

        qh = p_ref[rows, C_HQ:C_HQ + H_WIDTH]
        khh = p_ref[rows, C_HK:C_HK + H_WIDTH]
        head_sel = [jnp.where((_lane((1, H_WIDTH)) & H_DK) == hh * H_DK, 1.0, 0.0).astype(BF16) for hh in range(2)]
        for n in range(NL + 1):
            if n < NL:
                fac = jnp.exp(-jnp.abs(bc - _mid_row_broadcast(bc, L >> (n + 1))))
                qn, kn = (qh * fac).astype(BF16), (khh * fac).astype(BF16)
            else:
                qn, kn = qh.astype(BF16), khh.astype(BF16)
            qt_ref[n, 0] = qn * head_sel[0]
            qt_ref[n, 1] = qn * head_sel[1]
            kt_ref[n] = kn
        e_pre = jnp.exp(bc)
        qi_ref[rows, :] = (qh * e_pre).astype(BF16)
        dh_ref[pl.ds(c, 1), :] = e_pre[L - 1:L, :]
        k_suf = khh * jnp.exp(bc[L - 1:L, :] - bc)

        for h in range(N_HEADS):
            v = head(C_RV, h, BF16)
            s = (qk_r[h] * dm_ref[h]).astype(BF16)
            ho_ref[rows, HP + h * LANES:HP + (h + 1) * LANES] = _dot(s, v)
            kd_t = (head(C_RK, h) * dec_ref[L:2 * L, h * LANES:(h + 1) * LANES]).T.astype(BF16)
            ur_ref[c, h] = _dot(kd_t, v)

        lv = lv_ref[...]
        heads = [(p, hh) for p in range(N_HEADS // 2) for hh in range(2)]

        def level_scores(p, hh):
            psl = slice(p * LANES, (p + 1) * LANES)
            return [_dot_nt(qt_ref[n, hh, :, psl], kt_ref[n, :, psl]) for n in range(NL + 1)]

        def fold(scores):
            s_mat = jnp.zeros((L, L), F32)
            for n, sc in enumerate(scores):
                s_mat = jnp.where(lv == n, sc, s_mat)
            return s_mat.astype(BF16)

        pending = level_scores(*heads[0])

        cum_ref[rows, :] = cum
        a_all = ga - pltpu.roll(cum, LANES - N_HEADS, 1)
        a_t = a_all.T
        ml = jnp.zeros((L, LANES), F32)
        for h in range(N_HEADS):
            b_col = cum[:, N_HEADS + h:N_HEADS + h + 1]
            log_d = jnp.where(tril, b_col + a_t[h:h + 1, :], -jnp.inf)
            m_loc = jnp.max(log_d, axis=1, keepdims=True)
            pm_ref[h] = (qk_m[h] * jnp.exp(log_d - m_loc)).astype(BF16)
            ml = jnp.where(lane == h, m_loc, ml)
            w_loc = jnp.exp(b_col[L - 1:L, :] + a_all[:, h:h + 1] - m_loc[L - 1:L, :])
            kw_ref[h] = (head(C_MK, h) * w_loc).T.astype(BF16)
        ml_ref[rows, :] = ml

        s_bf = []
        for i in range(len(heads)):
            nxt = level_scores(*heads[i + 1]) if i + 1 < len(heads) else None
            s_bf.append(fold(pending))
            pending = nxt
            if i == 0:
                for h in range(N_HEADS):
                    v = head(C_MV, h, BF16)
                    ho_ref[rows, h * LANES:(h + 1) * LANES] = _dot(pm_ref[h], v)
                    u_ref[c, h] = _dot(kw_ref[h], v)
        for p in range(N_HEADS // 2):
            psl = slice(p * LANES, (p + 1) * LANES)
            vp = p_ref[rows, C_HI + p * LANES:C_HI + (p + 1) * LANES]
            ho_ref[rows, 2 * HP + p * LANES:2 * HP + (p + 1) * LANES] = (
                _dot(s_bf[2 * p], jnp.where(low, vp, 0.0).astype(BF16))
                + _dot(s_bf[2 * p + 1], jnp.where(low, 0.0, vp).astype(BF16)))
            uh_ref[c, p] = jnp.where(bd, _dot(k_suf[:, psl].T.astype(BF16), vp.astype(BF16)), 0.0)
        return carry

    lax.fori_loop(0, n_chunks, intra, 0)

    for c in range(n_chunks):
        rows = slice(c * L, (c + 1) * L)

        ml = ml_ref[rows, :]
        b_all = pltpu.roll(cum_ref[rows, :], LANES - N_HEADS, 1)
        m_prev = m_ref[0:1, :]
        m_t = jnp.maximum(ml, b_all + m_prev)
        gr_ref[rows, :] = jnp.exp(ml - m_t)
        gw_ref[rows, :] = jnp.exp(b_all + m_prev - m_t)
        ge_ref[rows, :] = jnp.exp(-m_t)
        m_new = m_t[L - 1:L, :]
        dec_row = jnp.exp(b_all[L - 1:L, :] + m_prev - m_new)
        g_row = jnp.exp(ml[L - 1:L, :] - m_new)
        m_ref[0:1, :] = m_new

        for h in range(N_HEADS):
            q = p_ref[rows, C_MQ + h * LANES:C_MQ + (h + 1) * LANES].astype(BF16)
            it_ref[rows, h * LANES:(h + 1) * LANES] = _dot(q, c_ref[h].astype(BF16))
        for h in range(N_HEADS):
            q = p_ref[rows, C_RQ + h * LANES:C_RQ + (h + 1) * LANES]
            qd = (q * dec_ref[0:L, h * LANES:(h + 1) * LANES]).astype(BF16)
            it_ref[rows, HP + h * LANES:HP + (h + 1) * LANES] = _dot(qd, r_ref[h].astype(BF16))
        for p in range(N_HEADS // 2):
            psl = slice(p * LANES, (p + 1) * LANES)
            it_ref[rows, 2 * HP + p * LANES:2 * HP + (p + 1) * LANES] = _dot(qi_ref[rows, psl], s_ref[p].astype(BF16))

        d_col = dh_ref[c:c + 1, :].T
        for h in range(N_HEADS):
            c_ref[h] = dec_row[:, h:h + 1] * c_ref[h] + g_row[:, h:h + 1] * u_ref[c, h]
            r_ref[h] = math.exp(L * LG[h]) * r_ref[h] + ur_ref[c, h]
        for p in range(N_HEADS // 2):
            s_ref[p] = d_col[p * LANES:(p + 1) * LANES, :] * s_ref[p] + uh_ref[c, p]

    def mix_rows(rows, T):
        valid = _lane((T, LANES)) < M_DH
        low_t = _lane((T, LANES)) < H_DK
        parts = []
        for h in range(N_HEADS):
            sl = slice(h * LANES, (h + 1) * LANES)
            num = gr_ref[rows, h:h + 1] * ho_ref[rows, sl] + gw_ref[rows, h:h + 1] * it_ref[rows, sl]
            den = num[:, M_DH:M_DH + 1]
            hm = num / jnp.maximum(jnp.abs(den), ge_ref[rows, h:h + 1])
            o = p_ref[rows, C_MO + h * LANES:C_MO + (h + 1) * LANES]
            z = p_ref[rows, C_MZ + h * LANES:C_MZ + (h + 1) * LANES]
            parts.append((_head_layer_norm_padded(hm, valid) * mnw_ref[:, sl] * _sigmoid(o) * _silu(z)).astype(BF16))
        for h in range(N_HEADS):
            sl = slice(HP + h * LANES, HP + (h + 1) * LANES)
            hn = _head_layer_norm_padded(ho_ref[rows, sl] + it_ref[rows, sl], valid)
            g = p_ref[rows, C_RG + h * LANES:C_RG + (h + 1) * LANES]
            parts.append((hn * rnw_ref[:, h * LANES:(h + 1) * LANES] * _silu(g)).astype(BF16))
        for p in range(N_HEADS // 2):
            sl = slice(2 * HP + p * LANES, 2 * HP + (p + 1) * LANES)
            hn = _pair_rms_norm(ho_ref[rows, sl] + it_ref[rows, sl], low_t)
            g = p_ref[rows, C_HG + p * LANES:C_HG + (p + 1) * LANES]
            parts.append((hn * hnw_ref[:, p * LANES:(p + 1) * LANES] * _silu(g)).astype(BF16))
        return jnp.concatenate(parts, axis=-1)

    def out_rows(rows, proj):
        y_ref[0, rows, :] = _layer_norm_rows(ALPHA * x_ref[0, rows, :] + proj, lng_ref[...], lnb_ref[...])

    half = TB // 2
    rows0, rows1 = slice(0, half), slice(half, TB)
    proj0 = _dot(mix_rows(rows0, half), wout_ref[...])
    mix1 = mix_rows(rows1, half)
    out_rows(rows0, proj0)
    out_rows(rows1, _dot(mix1, wout_ref[...]))

    @pl.when(j == n_tb - 1)
    def _final_states():
        convout_ref[0] = pre_ref[SUBLANES - (CONV_W - 1):SUBLANES, :]
        for h in range(N_HEADS):
            c_aug = c_ref[h]
            cout_ref[0, h] = c_aug[0:M_DH, 0:M_DH]
            nout_ref[0, h:h + 1, :] = c_aug.T[M_DH:M_DH + 1, 0:M_DH]
            r_full = r_ref[h]
            rout_ref[0, h] = jnp.concatenate(
                [r_full[0:ROPE_HALF, 0:R_DH], r_full[LANES // 2:LANES // 2 + ROPE_HALF, 0:R_DH]], axis=0)
            p, hh = divmod(h, 2)
            sout_ref[0, h] = s_ref[p][hh * H_DK:(hh + 1) * H_DK, hh * H_DK:(hh + 1) * H_DK]
        mout_ref[0] = m_ref[0:1, 0:N_HEADS]


def _full_spec(shape, single_buffer=False):
    nd = len(shape)
    if single_buffer:
        return pl.BlockSpec(shape, lambda *_: (0,) * nd, pipeline_mode=pl.Buffered(1))
    return pl.BlockSpec(shape, lambda *_: (0,) * nd)


def _prefill_layer(layer, x, win, wout, cos, sin, convw, convb, bif, mnw, rnw, hnw, hlb, lng, lnb, lv, TB):
    B, T, _ = x.shape
    n_tb = T // TB
    n_chunks = TB // CHUNK
    kern = functools.partial(_prefill_kernel, layer, TB)
    in_specs = [
        pl.BlockSpec((1, TB, D_MODEL), lambda b, j: (b, j, 0)),
        _full_spec(win.shape, True), _full_spec(wout.shape, True),
        pl.BlockSpec((TB, LANES), lambda b, j: (j, 0)),
        pl.BlockSpec((TB, LANES), lambda b, j: (j, 0)),
        _full_spec(convw.shape), _full_spec(convb.shape), _full_spec(bif.shape),
        _full_spec(mnw.shape), _full_spec(rnw.shape), _full_spec(hnw.shape), _full_spec(hlb.shape),
        _full_spec(lng.shape), _full_spec(lnb.shape), _full_spec(lv.shape),
    ]
    out_shape = (
        jax.ShapeDtypeStruct((B, T, D_MODEL), F32),
        jax.ShapeDtypeStruct((B, N_HEADS, M_DH, M_DH), F32),
        jax.ShapeDtypeStruct((B, N_HEADS, M_DH), F32),
        jax.ShapeDtypeStruct((B, 1, N_HEADS), F32),
        jax.ShapeDtypeStruct((B, CONV_W - 1, 2 * HP), F32),
        jax.ShapeDtypeStruct((B, N_HEADS, R_DH, R_DH), F32),
        jax.ShapeDtypeStruct((B, N_HEADS, H_DK, H_DK), F32),
    )
    out_specs = (
        pl.BlockSpec((1, TB, D_MODEL), lambda b, j: (b, j, 0)),
        pl.BlockSpec((1, N_HEADS, M_DH, M_DH), lambda b, j: (b, 0, 0, 0)),
        pl.BlockSpec((1, N_HEADS, M_DH), lambda b, j: (b, 0, 0)),
        pl.BlockSpec((1, 1, N_HEADS), lambda b, j: (b, 0, 0)),
        pl.BlockSpec((1, CONV_W - 1, 2 * HP), lambda b, j: (b, 0, 0)),
        pl.BlockSpec((1, N_HEADS, R_DH, R_DH), lambda b, j: (b, 0, 0, 0)),
        pl.BlockSpec((1, N_HEADS, H_DK, H_DK), lambda b, j: (b, 0, 0, 0)),
    )
    L = CHUNK
    scratch = [
        pltpu.VMEM((TB, NP_SCR), F32),
        pltpu.VMEM((TB + SUBLANES, 2 * HP), F32),
        pltpu.VMEM((TB, MIXP), F32),
        pltpu.VMEM((TB, LANES), F32),
        pltpu.VMEM((TB, LANES), F32),
        pltpu.VMEM((N_HEADS, L, L), BF16),
        pltpu.VMEM((N_HEADS, LANES, L), BF16),
        pltpu.VMEM((N_LEVELS + 1, 2, L, H_WIDTH), BF16),
        pltpu.VMEM((N_LEVELS + 1, L, H_WIDTH), BF16),
        pltpu.VMEM((TB, H_WIDTH), BF16),
        pltpu.VMEM((max(n_chunks, SUBLANES), H_WIDTH), F32),
        pltpu.VMEM((n_chunks, N_HEADS, LANES, LANES), F32),
        pltpu.VMEM((n_chunks, N_HEADS, LANES, LANES), F32),
        pltpu.VMEM((n_chunks, N_HEADS // 2, LANES, LANES), F32),
        pltpu.VMEM((N_HEADS, LANES, LANES), F32),
        pltpu.VMEM((N_HEADS, LANES, LANES), F32),
        pltpu.VMEM((N_HEADS // 2, LANES, LANES), F32),
        pltpu.VMEM((SUBLANES, LANES), F32),
        pltpu.VMEM((N_HEADS, L, L), F32),
        pltpu.VMEM((2 * L, HP), F32),
        pltpu.VMEM((TB, D_MODEL), BF16),
        pltpu.VMEM((TB, MIXP), F32),
        pltpu.VMEM((TB, LANES), F32),
        pltpu.VMEM((TB, LANES), F32),
        pltpu.VMEM((TB, LANES), F32),
    ]
    return pl.pallas_call(
        kern, grid=(B, n_tb), in_specs=in_specs, out_specs=out_specs, out_shape=out_shape,
        scratch_shapes=scratch,
        compiler_params=pltpu.CompilerParams(dimension_semantics=("arbitrary", "arbitrary"),
                                             vmem_limit_bytes=VMEM_LIMIT),
        name=f"prefill_layer{layer}",
    )(x, win, wout, cos, sin, convw, convb, bif, mnw, rnw, hnw, hlb, lng, lnb, lv)


DEC_KB = 4
KR_M = M_DH // DEC_KB
KR_H = H_DK // DEC_KB
G_DEC, G_WS, G_EN, G_GAM, G_SM, G_QN, G_SR, G_SH = (i * N_HEADS for i in range(8))


def _stream_state(q_blk, dec, kw_blk, v, in_ref, out_ref, nrows):
    acc = jnp.zeros_like(v)
    for kk in range(nrows):
        st = in_ref[0, 0, kk]
        d = dec if dec.shape[0] == 1 else dec[kk:kk + 1, :]
        acc = acc + q_blk[kk:kk + 1, :] * st
        out_ref[0, 0, kk] = d * st + kw_blk[kk:kk + 1, :] * v
    return acc


def _decode_kernel(x_ref, wt_ref, wout_ref, cos_ref, sin_ref, convw_ref, convb_ref, bif_ref,
                   mnw_ref, rnw_ref, hnw_ref, hlbt_ref, lng_ref, lnb_ref,
                   cin_ref, nin_ref, min_ref, convin_ref, rin_ref, sin_st_ref,
                   y_ref, cout_ref, nout_ref, mout_ref, convout_ref, rout_ref, sout_ref,
                   p_ref, xs_ref, ht_ref, mix_ref, g_ref):
    l = pl.program_id(0)
    h = pl.program_id(1)
    kb = pl.program_id(2)
    first = jnp.logical_and(h == 0, kb == 0)
    last = jnp.logical_and(h == N_HEADS - 1, kb == DEC_KB - 1)
    B = x_ref.shape[0]

    @pl.when(jnp.logical_and(first, l == 0))
    def _load_x():
        xs_ref[...] = x_ref[...]

    @pl.when(first)
    def _project():
        p_ref[0:N_IN, :] = _dot_nt(wt_ref[0], xs_ref[...].astype(BF16))
        ht_ref[...] = jnp.zeros_like(ht_ref)

        pre = p_ref[O_MQ:O_MQ + 2 * M_WIDTH, :].T
        cw = convw_ref[0]
        acc = convb_ref[0] + cw[CONV_W - 1:CONV_W, :] * pre
        for jj in range(CONV_W - 1):
            acc = acc + cw[jj:jj + 1, :] * convin_ref[0, jj]
        for jj in range(CONV_W - 2):
            convout_ref[0, jj] = convin_ref[0, jj + 1]
        convout_ref[0, CONV_W - 2] = pre
        p_ref[O_MQ:O_MQ + 2 * M_WIDTH, :] = _silu(acc).T

        gt = p_ref[O_IF:O_IF + 2 * N_HEADS, :] + bif_ref[0]
        ig = gt[0:N_HEADS]
        lf = _log_sigmoid(gt[N_HEADS:2 * N_HEADS])
        m_prev = min_ref[0]
        m_t = jnp.maximum(ig, lf + m_prev)
        dec = jnp.exp(lf + m_prev - m_t)
        w_s = jnp.exp(ig - m_t)
        mout_ref[0] = m_t
        g_ref[G_DEC:G_DEC + N_HEADS, :] = dec
        g_ref[G_WS:G_WS + N_HEADS, :] = w_s
        g_ref[G_EN:G_EN + N_HEADS, :] = jnp.exp(-m_t)
        for hh in range(N_HEADS):
            q = p_ref[O_MQ + hh * M_DH:O_MQ + (hh + 1) * M_DH, :]
            k = p_ref[O_MK + hh * M_DH:O_MK + (hh + 1) * M_DH, :] * (M_DH ** -0.5)
            n_old = nin_ref[0, hh]
            d_h = dec[hh:hh + 1]
            kw = k * w_s[hh:hh + 1]
            g_ref[G_SM + hh:G_SM + hh + 1, :] = jnp.sum(q * k, axis=0, keepdims=True) * w_s[hh:hh + 1]
            g_ref[G_QN + hh:G_QN + hh + 1, :] = jnp.sum(q * n_old, axis=0, keepdims=True)
            nout_ref[0, hh] = d_h * n_old + kw
            p_ref[O_MK + hh * M_DH:O_MK + (hh + 1) * M_DH, :] = kw

        cos = cos_ref[...]
        sin = sin_ref[...]
        for hh in range(N_HEADS):
            g_ref[G_GAM + hh:G_GAM + hh + 1, :] = jnp.full((1, B), math.exp(LG[hh]), F32)
            rot = []
            for off, scale in ((O_RQ, 1.0), (O_RK, R_DH ** -0.5)):
                x1 = p_ref[off + hh * R_DH:off + hh * R_DH + ROPE_HALF, :]
                x2 = p_ref[off + hh * R_DH + ROPE_HALF:off + (hh + 1) * R_DH, :]
                r1 = (x1 * cos - x2 * sin) * scale
                r2 = (x1 * sin + x2 * cos) * scale
                p_ref[off + hh * R_DH:off + hh * R_DH + ROPE_HALF, :] = r1
                p_ref[off + hh * R_DH + ROPE_HALF:off + (hh + 1) * R_DH, :] = r2
                rot.append((r1, r2))
            (q1, q2), (k1, k2) = rot
            g_ref[G_SR + hh:G_SR + hh + 1, :] = (jnp.sum(q1 * k1, axis=0, keepdims=True)
                                                 + jnp.sum(q2 * k2, axis=0, keepdims=True))

        lbs = _hgrn_lower_bounds(hlbt_ref[...], 1)
        lb = lbs[0]
        for i in range(1, DEPTH):
            lb = jnp.where(l == i, lbs[i], lb)
        lfh, kh = _hgrn_gates(p_ref[O_HF:O_HF + H_WIDTH, :], lb)
        eh = jnp.exp(lfh)
        qh = p_ref[O_HQ:O_HQ + H_WIDTH, :]
        p_ref[O_HF:O_HF + H_WIDTH, :] = eh
        p_ref[O_HK:O_HK + H_WIDTH, :] = kh
        p_ref[O_HQ:O_HQ + H_WIDTH, :] = qh * eh
        qk = qh * kh
        for hh in range(N_HEADS):
            g_ref[G_SH + hh:G_SH + hh + 1, :] = jnp.sum(qk[hh * H_DK:(hh + 1) * H_DK], axis=0, keepdims=True)

    def rows(base, width, n):
        return pl.ds(pl.multiple_of(base + h * width + kb * n, SUBLANES), n)

    dec = g_ref[pl.ds(G_DEC + h, 1), :]
    acc = _stream_state(p_ref[rows(O_MQ, M_DH, KR_M), :], dec, p_ref[rows(O_MK, M_DH, KR_M), :],
                        p_ref[pl.ds(pl.multiple_of(O_MV + h * M_DH, SUBLANES), M_DH), :], cin_ref, cout_ref, KR_M)
    ht_ref[pl.ds(pl.multiple_of(h * M_DH, SUBLANES), M_DH), :] += acc

    gam = g_ref[pl.ds(G_GAM + h, 1), :]
    acc = _stream_state(p_ref[rows(O_RQ, R_DH, KR_M), :], gam, p_ref[rows(O_RK, R_DH, KR_M), :],
                        p_ref[pl.ds(pl.multiple_of(O_RV + h * R_DH, SUBLANES), R_DH), :], rin_ref, rout_ref, KR_M)
    ht_ref[pl.ds(pl.multiple_of(M_WIDTH + h * R_DH, SUBLANES), R_DH), :] += acc

    acc = _stream_state(p_ref[rows(O_HQ, H_DK, KR_H), :], p_ref[rows(O_HF, H_DK, KR_H), :],
                        p_ref[rows(O_HK, H_DK, KR_H), :],
                        p_ref[pl.ds(pl.multiple_of(O_HI + h * H_DK, SUBLANES), H_DK), :], sin_st_ref, sout_ref, KR_H)
    ht_ref[pl.ds(pl.multiple_of(M_WIDTH + R_WIDTH + h * H_DK, SUBLANES), H_DK), :] += acc

    @pl.when(last)
    def _finish():
        def head_ln(a):
            mu = jnp.mean(a, axis=0, keepdims=True)
            d = a - mu
            return d * lax.rsqrt(jnp.mean(d * d, axis=0, keepdims=True) + HEAD_EPS)

        for hh in range(N_HEADS):
            sl = slice(hh * M_DH, (hh + 1) * M_DH)
            v = p_ref[O_MV + hh * M_DH:O_MV + (hh + 1) * M_DH, :]
            d_h = g_ref[G_DEC + hh:G_DEC + hh + 1, :]
            s = g_ref[G_SM + hh:G_SM + hh + 1, :]
            num = s * v + ht_ref[sl, :] * d_h
            den = s + g_ref[G_QN + hh:G_QN + hh + 1, :] * d_h
            hm = num / jnp.maximum(jnp.abs(den), g_ref[G_EN + hh:G_EN + hh + 1, :])
            o = p_ref[O_MO + hh * M_DH:O_MO + (hh + 1) * M_DH, :]
            z = p_ref[O_MZ + hh * M_DH:O_MZ + (hh + 1) * M_DH, :]
            mix_ref[sl, :] = head_ln(hm) * mnw_ref[0, sl, :] * _sigmoid(o) * _silu(z)
        for hh in range(N_HEADS):
            sl = slice(hh * R_DH, (hh + 1) * R_DH)
            v = p_ref[O_RV + hh * R_DH:O_RV + (hh + 1) * R_DH, :]
            hr = (g_ref[G_SR + hh:G_SR + hh + 1, :] * v
                  + ht_ref[M_WIDTH + hh * R_DH:M_WIDTH + (hh + 1) * R_DH, :] * math.exp(LG[hh]))
            gg = p_ref[O_RG + hh * R_DH:O_RG + (hh + 1) * R_DH, :]
            mix_ref[M_WIDTH + hh * R_DH:M_WIDTH + (hh + 1) * R_DH, :] = head_ln(hr) * rnw_ref[0, sl, :] * _silu(gg)
        for hh in range(N_HEADS):
            sl = slice(hh * H_DK, (hh + 1) * H_DK)
            base = M_WIDTH + R_WIDTH
            v = p_ref[O_HI + hh * H_DK:O_HI + (hh + 1) * H_DK, :]
            ho = g_ref[G_SH + hh:G_SH + hh + 1, :] * v + ht_ref[base + hh * H_DK:base + (hh + 1) * H_DK, :]
            hn = ho * lax.rsqrt(jnp.mean(ho * ho, axis=0, keepdims=True) + HEAD_EPS)
            gg = p_ref[O_HG + hh * H_DK:O_HG + (hh + 1) * H_DK, :]
            mix_ref[base + hh * H_DK:base + (hh + 1) * H_DK, :] = hn * hnw_ref[0, sl, :] * _silu(gg)
        x = xs_ref[...]
        out = _dot(mix_ref[...].T.astype(BF16), wout_ref[0])
        y = _layer_norm_rows(ALPHA * x + out, lng_ref[0], lnb_ref[0])
        xs_ref[...] = y
        y_ref[0] = y


def _decode(x, wt, wout, cos, sin, convw, convb, bif, mnw, rnw, hnw, hlbt, lng, lnb, st_c, st_n, st_m, st_conv,
            st_r, st_s):
    B = x.shape[0]

    def lspec(a):
        nd = a.ndim
        return pl.BlockSpec((1,) + a.shape[1:], lambda l, h, kb: (l,) + (0,) * (nd - 1))

    def kv_spec(a, n):
        return pl.BlockSpec((1, 1, n) + a.shape[3:], lambda l, h, kb: (l, h, kb, 0, 0))

    in_specs = [
        _full_spec(x.shape), lspec(wt), lspec(wout), _full_spec(cos.shape), _full_spec(sin.shape),
        lspec(convw), lspec(convb), lspec(bif), lspec(mnw), lspec(rnw), lspec(hnw), _full_spec(hlbt.shape),
        lspec(lng), lspec(lnb),
        kv_spec(st_c, KR_M), lspec(st_n), lspec(st_m), lspec(st_conv), kv_spec(st_r, KR_M), kv_spec(st_s, KR_H),
    ]
    out_shape = (
        jax.ShapeDtypeStruct((DEPTH, B, D_MODEL), F32),
        jax.ShapeDtypeStruct(st_c.shape, F32), jax.ShapeDtypeStruct(st_n.shape, F32),
        jax.ShapeDtypeStruct(st_m.shape, F32), jax.ShapeDtypeStruct(st_conv.shape, F32),
        jax.ShapeDtypeStruct(st_r.shape, F32), jax.ShapeDtypeStruct(st_s.shape, F32),
    )
    out_specs = (
        pl.BlockSpec((1, B, D_MODEL), lambda l, h, kb: (l, 0, 0)),
        kv_spec(st_c, KR_M), lspec(st_n), lspec(st_m), lspec(st_conv), kv_spec(st_r, KR_M), kv_spec(st_s, KR_H),
    )
    scratch = [
        pltpu.VMEM((N_IN + H_WIDTH, B), F32),
        pltpu.VMEM((B, D_MODEL), F32),
        pltpu.VMEM((D_MIX, B), F32),
        pltpu.VMEM((D_MIX, B), F32),
        pltpu.VMEM((8 * N_HEADS, B), F32),
    ]
    return pl.pallas_call(
        _decode_kernel, grid=(DEPTH, N_HEADS, DEC_KB), in_specs=in_specs, out_specs=out_specs,
        out_shape=out_shape, scratch_shapes=scratch,
        compiler_params=pltpu.CompilerParams(dimension_semantics=("arbitrary", "arbitrary", "arbitrary"),
                                             vmem_limit_bytes=VMEM_LIMIT),
        name="decode_step",
    )(x, wt, wout, cos, sin, convw, convb, bif, mnw, rnw, hnw, hlbt, lng, lnb,
      st_c, st_n, st_m, st_conv, st_r, st_s)


def _rope_angles(pos):
    inv = ROPE_BASE ** (-jnp.arange(ROPE_HALF, dtype=F32) / ROPE_HALF)
    ang = pos.astype(F32)[:, None] * inv[None, :]
    return jnp.cos(ang), jnp.sin(ang)


PREFILL_BLOCK = 512


def kernel(x_prompt, x_sample, state_mlstm_C, state_mlstm_n, state_mlstm_m, state_mlstm_conv, state_ret, state_hgrn, w_in, conv_w, conv_b, b_mgate, m_norm_w, r_norm_w, h_norm_w, hgrn_lb, w_out, ln_g, ln_b):
    B, T, _ = x_prompt.shape

    wt = jnp.swapaxes(w_in, 1, 2).astype(BF16)
    sec = lambda o, n: wt[:, o:o + n]
    win_p = jnp.concatenate(
        [_pad_heads(sec(o, M_WIDTH), 1) for o in (O_MQ, O_MK, O_MV, O_MO, O_MZ)]
        + [_pad_rope_heads(sec(O_RQ, R_WIDTH), 1), _pad_rope_heads(sec(O_RK, R_WIDTH), 1),
           _pad_heads(sec(O_RV, R_WIDTH), 1), _pad_heads(sec(O_RG, R_WIDTH), 1),
           sec(O_HF, 4 * H_WIDTH),
           jnp.pad(sec(O_IF, 2 * N_HEADS), ((0, 0), (0, LANES - 2 * N_HEADS), (0, 0)))], axis=1)
    wout_b = w_out.astype(BF16)
    wout_p = jnp.concatenate([_pad_heads(wout_b[:, 0:M_WIDTH], 1), _pad_heads(wout_b[:, M_WIDTH:M_WIDTH + R_WIDTH], 1),
                              wout_b[:, M_WIDTH + R_WIDTH:]], axis=1)
    convw_p = jnp.concatenate([_pad_heads(conv_w[..., :M_WIDTH], 2), _pad_heads(conv_w[..., M_WIDTH:], 2)], axis=-1)
    convb_p = jnp.concatenate([_pad_heads(conv_b[..., :M_WIDTH], 1), _pad_heads(conv_b[..., M_WIDTH:], 1)],
                              axis=-1)[:, None, :]
    bif_p = jnp.pad(b_mgate, ((0, 0), (0, LANES - 2 * N_HEADS)))[:, None, :]
    mnw_p = _pad_heads(m_norm_w, 1)[:, None, :]
    rnw_p = _pad_heads(r_norm_w, 1)[:, None, :]
    hnw_p = h_norm_w[:, None, :]
    lng_p = ln_g[:, None, :]
    lnb_p = ln_b[:, None, :]
    hlb = hgrn_lb.astype(F32)

    lv = jnp.asarray(_level_table(CHUNK))

    c, s = _rope_angles(jnp.arange(T, dtype=jnp.int32))
    z = jnp.zeros((T, LANES // 2 - ROPE_HALF), F32)
    cos_p = jnp.concatenate([c, z, c, z], axis=1)
    sin_p = jnp.concatenate([-s, z, s, z], axis=1)

    hp = x_prompt
    st = [[] for _ in range(6)]
    for l in range(DEPTH):
        outs = _prefill_layer(l, hp, win_p[l], wout_p[l], cos_p, sin_p, convw_p[l], convb_p[l], bif_p[l],
                              mnw_p[l], rnw_p[l], hnw_p[l], hlb, lng_p[l], lnb_p[l], lv, PREFILL_BLOCK)
        hp = outs[0]
        for k in range(6):
            st[k].append(outs[1 + k])
    mC_p = jnp.stack(st[0])
    mn_p = jnp.stack(st[1])
    mm_p = jnp.stack(st[2])[:, :, 0, :]
    conv_p = _unpad_heads(jnp.stack(st[3]))
    ret_p = jnp.stack(st[4])
    hgrn_p = jnp.stack(st[5])

    n_s = x_sample.shape[0]
    cs, ss = _rope_angles(PAST_LEN + jnp.arange(x_sample.shape[1], dtype=jnp.int32))
    cos_s = jnp.broadcast_to(cs[0][:, None], (ROPE_HALF, n_s))
    sin_s = jnp.broadcast_to(ss[0][:, None], (ROPE_HALF, n_s))
    ys, c_t, n_t, m_t, conv_t, r_t, s_t = _decode(
        x_sample[:, 0, :], wt, wout_b, cos_s, sin_s, conv_w, conv_b[:, None, :], b_mgate[:, :, None],
        m_norm_w[:, :, None], r_norm_w[:, :, None], h_norm_w[:, :, None], hlb.T, lng_p, lnb_p,
        jnp.transpose(state_mlstm_C, (0, 2, 3, 4, 1)), jnp.transpose(state_mlstm_n, (0, 2, 3, 1)),
        jnp.transpose(state_mlstm_m, (0, 2, 1)), jnp.transpose(state_mlstm_conv, (0, 2, 1, 3)),
        jnp.transpose(state_ret, (0, 2, 3, 4, 1)), jnp.transpose(state_hgrn, (0, 2, 3, 4, 1)))
    hs = ys[DEPTH - 1][:, None, :]
    mC_s = jnp.transpose(c_t, (0, 4, 1, 2, 3))
    mn_s = jnp.transpose(n_t, (0, 3, 1, 2))
    mm_s = jnp.transpose(m_t, (0, 2, 1))
    conv_s = jnp.transpose(conv_t, (0, 2, 1, 3))
    ret_s = jnp.transpose(r_t, (0, 4, 1, 2, 3))
    hgrn_s = jnp.transpose(s_t, (0, 4, 1, 2, 3))

    return (hp, hs, mC_p, mn_p, mm_p, conv_p, ret_p, hgrn_p, mC_s, mn_s, mm_s, conv_s, ret_s, hgrn_s)
```

```python
import functools
import math

import numpy as np
import jax
import jax.numpy as jnp
from jax import lax
from jax.experimental import pallas as pl
from jax.experimental.pallas import tpu as pltpu

F32 = jnp.float32
BF16 = jnp.bfloat16

D_MODEL = 1024
DEPTH = 2
PAST_LEN = 16384
N_HEADS = 4
M_DH = 96
R_DH = 96
H_DK = 64
M_WIDTH = N_HEADS * M_DH
R_WIDTH = N_HEADS * R_DH
H_WIDTH = N_HEADS * H_DK
D_MIX = M_WIDTH + R_WIDTH + H_WIDTH
CONV_W = 4
ROPE_BASE = 10000.0
LN_EPS = 1e-5
HEAD_EPS = 1e-6
ALPHA = (2 * DEPTH) ** 0.25

LANES = 128
SUBLANES = 8
HP = N_HEADS * LANES
ROPE_HALF = R_DH // 2
VMEM_LIMIT = 56 * 1024 * 1024

O_MQ, O_MK, O_MV, O_MO, O_MZ = 0, M_WIDTH, 2 * M_WIDTH, 3 * M_WIDTH, 4 * M_WIDTH
O_IF = 5 * M_WIDTH
O_RQ = O_IF + 2 * N_HEADS
O_RK, O_RV, O_RG = O_RQ + R_WIDTH, O_RQ + 2 * R_WIDTH, O_RQ + 3 * R_WIDTH
O_HF = O_RQ + 4 * R_WIDTH
O_HI, O_HQ, O_HG = O_HF + H_WIDTH, O_HF + 2 * H_WIDTH, O_HF + 3 * H_WIDTH
N_IN = O_HF + 4 * H_WIDTH
O_HK = N_IN

C_MQ, C_MK, C_MV, C_MO, C_MZ = 0, HP, 2 * HP, 3 * HP, 4 * HP
C_RQ, C_RK, C_RV, C_RG = 5 * HP, 6 * HP, 7 * HP, 8 * HP
C_HF = 9 * HP
C_HI = C_HF + H_WIDTH
C_HQ = C_HI + H_WIDTH
C_HG = C_HQ + H_WIDTH
C_IF = C_HG + H_WIDTH
NP_IN = C_IF + LANES
C_HK = NP_IN
NP_SCR = NP_IN + H_WIDTH
MIXP = 2 * HP + H_WIDTH

V_CONVW, V_CONVB, V_BIF, V_MNW, V_RNW, V_HNW, V_LNG, V_LNB = 0, CONV_W, CONV_W + 1, CONV_W + 2, CONV_W + 3, CONV_W + 4, CONV_W + 5, CONV_W + 6
V_ROWS = 2 * SUBLANES

CHUNK = 128
N_LEVELS = int(math.log2(CHUNK))
LG = [math.log1p(-2.0 ** (-5.0 - h)) for h in range(N_HEADS)]


def _pad_groups(a, axis, group, padded):
    shp = a.shape
    n = shp[axis] // group
    a = a.reshape(shp[:axis] + (n, group) + shp[axis + 1:])
    pad = [(0, 0)] * a.ndim
    pad[axis + 1] = (0, padded - group)
    return jnp.pad(a, pad).reshape(shp[:axis] + (n * padded,) + shp[axis + 1:])


def _pad_heads(a, axis):
    return _pad_groups(a, axis, M_DH, LANES)


def _pad_rope_heads(a, axis):
    return _pad_groups(a, axis, ROPE_HALF, LANES // 2)


def _unpad_heads(a, dh=M_DH):
    shp = a.shape
    return a.reshape(shp[:-1] + (shp[-1] // LANES, LANES))[..., :dh].reshape(shp[:-1] + (shp[-1] // LANES * dh,))


def _level_table(L):
    nl = int(math.log2(L))
    tt = np.arange(L)[:, None]
    ss = np.arange(L)[None, :]
    lv = np.full((L, L), -1, np.int32)
    for n in range(nl):
        w = L >> (n + 1)
        same = (tt // (2 * w)) == (ss // (2 * w))
        cond = same & ((tt % (2 * w)) >= w) & ((ss % (2 * w)) < w)
        lv = np.where(cond, n, lv)
    lv = np.where(tt == ss, nl, lv)
    return lv.astype(np.int32)


def _dot(a, b):
    return jnp.dot(a, b, preferred_element_type=F32)


def _dot_nt(a, b):
    return lax.dot_general(a, b, (((1,), (1,)), ((), ())), preferred_element_type=F32)


def _sigmoid(x):
    return 1.0 / (1.0 + jnp.exp(-x))


def _silu(x):
    return x * _sigmoid(x)


def _log_sigmoid(x):
    return jnp.minimum(x, 0.0) - jnp.log1p(jnp.exp(-jnp.abs(x)))


def _lane(shape):
    return lax.broadcasted_iota(jnp.int32, shape, len(shape) - 1)


def _hgrn_lower_bounds(hlb, layer_axis):
    def take(a, l):
        return a[l:l + 1] if layer_axis == 0 else a[:, l:l + 1]
    mx = jnp.max(hlb, axis=layer_axis, keepdims=True)
    e = jnp.exp(hlb - mx)
    p = e / jnp.sum(e, axis=layer_axis, keepdims=True)
    out = []
    acc = None
    for l in range(DEPTH):
        acc = take(p, l) if acc is None else acc + take(p, l)
        out.append(acc - take(p, 0))
    return out


def _hgrn_gates(fpre, lb):
    a = jnp.log(lb)
    c = jnp.log1p(-lb) + _log_sigmoid(fpre)
    mx = jnp.maximum(a, c)
    lf = mx + jnp.log(jnp.exp(a - mx) + jnp.exp(c - mx))
    kh = (1.0 - lb) * _sigmoid(-fpre)
    return lf, kh


def _cumsum_rows(x):
    L, W = x.shape
    sub = lax.broadcasted_iota(jnp.int32, (SUBLANES, W), 0)
    out = []
    carry = None
    for r in range(L // SUBLANES):
        t = x[r * SUBLANES:(r + 1) * SUBLANES, :]
        s = 1
        while s < SUBLANES:
            t = t + jnp.where(sub >= s, pltpu.roll(t, s, 0), 0.0)
            s *= 2
        if carry is not None:
            t = t + carry
        carry = t[SUBLANES - 1:SUBLANES, :]
        out.append(t)
    return jnp.concatenate(out, axis=0)


def _mid_row_broadcast(bc, w):
    L, W = bc.shape
    if 2 * w >= SUBLANES:
        blocks = [jnp.broadcast_to(bc[b * 2 * w + w:b * 2 * w + w + 1, :], (2 * w, W)) for b in range(L // (2 * w))]
        return blocks[0] if len(blocks) == 1 else jnp.concatenate(blocks, axis=0)
    x3 = bc.reshape(L // SUBLANES, SUBLANES, W)
    sub = lax.broadcasted_iota(jnp.int32, x3.shape, 1)
    out = None
    for b in range(SUBLANES // (2 * w)):
        piece = jnp.broadcast_to(x3[:, b * 2 * w + w:b * 2 * w + w + 1, :], x3.shape)
        out = piece if out is None else jnp.where(sub >= b * 2 * w, piece, out)
    return out.reshape(L, W)


def _layer_norm_rows(r, g, b):
    mu = jnp.mean(r, axis=-1, keepdims=True)
    d = r - mu
    var = jnp.mean(d * d, axis=-1, keepdims=True)
    return d * lax.rsqrt(var + LN_EPS) * g + b


def _head_layer_norm_padded(h, valid):
    dh = M_DH
    hz = jnp.where(valid, h, 0.0)
    mu = jnp.sum(hz, axis=-1, keepdims=True) * (1.0 / dh)
    d = jnp.where(valid, h - mu, 0.0)
    var = jnp.sum(d * d, axis=-1, keepdims=True) * (1.0 / dh)
    return d * lax.rsqrt(var + HEAD_EPS)


def _pair_rms_norm(h, low):
    sq = h * h
    m0 = jnp.sum(jnp.where(low, sq, 0.0), axis=-1, keepdims=True) * (1.0 / H_DK)
    m1 = jnp.sum(jnp.where(low, 0.0, sq), axis=-1, keepdims=True) * (1.0 / H_DK)
    return h * jnp.where(low, lax.rsqrt(m0 + HEAD_EPS), lax.rsqrt(m1 + HEAD_EPS))


def _prefill_kernel(layer, TB,
                    x_ref, win_ref, wout_ref, cos_ref, sin_ref, vec_ref, hlb_ref, lv_ref,
                    y_ref, cout_ref, nout_ref, mout_ref, convout_ref, rout_ref, sout_ref,
                    p_ref, pre_ref, ho_ref, cum_ref, ml_ref, pm_ref, kw_ref, qt_ref, kt_ref, qi_ref, dh_ref,
                    u_ref, ur_ref, uh_ref, c_ref, r_ref, s_ref, m_ref, dm_ref, dec_ref,
                    xb_ref, it_ref, gr_ref, gw_ref, ge_ref):
    L = CHUNK
    NL = N_LEVELS
    n_chunks = TB // L
    j = pl.program_id(1)
    n_tb = pl.num_programs(1)

    @pl.when(j == 0)
    def _init():
        c_ref[...] = jnp.zeros_like(c_ref)
        r_ref[...] = jnp.zeros_like(r_ref)
        s_ref[...] = jnp.zeros_like(s_ref)
        m_ref[...] = jnp.zeros_like(m_ref)
        pre_ref[0:SUBLANES, :] = jnp.zeros((SUBLANES, 2 * HP), F32)
        ti = lax.broadcasted_iota(jnp.int32, (L, L), 0)
        si = lax.broadcasted_iota(jnp.int32, (L, L), 1)
        dist = (ti - si).astype(F32)
        row = lax.broadcasted_iota(jnp.int32, (L, LANES), 0).astype(F32)
        for h in range(N_HEADS):
            dm_ref[h] = jnp.where(ti >= si, jnp.exp(dist * LG[h]), 0.0)
            dec_ref[0:L, h * LANES:(h + 1) * LANES] = jnp.exp((row + 1.0) * LG[h])
            dec_ref[L:2 * L, h * LANES:(h + 1) * LANES] = jnp.exp((L - 1.0 - row) * LG[h])

    xb_ref[...] = x_ref[0].astype(BF16)
    lb = _hgrn_lower_bounds(hlb_ref[...], 0)[layer]
    PIECE = 2 * LANES

    def project(c0):
        c1 = min(c0 + PIECE, NP_IN)
        res = _dot_nt(xb_ref[...], win_ref[c0:c1, :])
        if c0 < C_MV:
            pre_ref[SUBLANES:SUBLANES + TB, c0:c1] = res
        else:
            p_ref[:, c0:c1] = res

    def gate_math(r0, n):
        rows = slice(r0, r0 + n)
        lf, kh = _hgrn_gates(p_ref[rows, C_HF:C_HF + H_WIDTH], lb)
        p_ref[rows, C_HF:C_HF + H_WIDTH] = lf
        p_ref[rows, C_HK:C_HK + H_WIDTH] = kh
        g = p_ref[rows, C_IF:C_IF + LANES] + vec_ref[V_BIF:V_BIF + 1, 0:LANES]
        p_ref[rows, C_IF:C_IF + LANES] = jnp.where(_lane((n, LANES)) < N_HEADS, g, _log_sigmoid(g))

    def conv_math(half, r0, n):
        cols = slice(half * HP, (half + 1) * HP)
        lo = SUBLANES + r0
        acc = vec_ref[V_CONVB:V_CONVB + 1, cols] + vec_ref[V_CONVW + CONV_W - 1:V_CONVW + CONV_W, cols] * pre_ref[lo:lo + n, cols]
        for t in range(1, CONV_W):
            acc = acc + vec_ref[V_CONVW + CONV_W - 1 - t:V_CONVW + CONV_W - t, cols] * pre_ref[lo - t:lo - t + n, cols]
        scale = 1.0 if half == 0 else M_DH ** -0.5
        p_ref[r0:r0 + n, C_MQ + half * HP:C_MQ + (half + 1) * HP] = _silu(acc) * scale

    def rope_math(col, scale, r0, n):
        rows = slice(r0, r0 + n)
        cos = cos_ref[rows, :]
        sin = sin_ref[rows, :]
        for h in range(N_HEADS):
            sl = slice(col + h * LANES, col + (h + 1) * LANES)
            v = p_ref[rows, sl]
            p_ref[rows, sl] = (v * cos + pltpu.roll(v, LANES // 2, 1) * sin) * scale

    def value_math(r0, n):
        one96 = jnp.where(_lane((n, LANES)) == M_DH, 1.0, 0.0)
        for h in range(N_HEADS):
            sl = slice(C_MV + h * LANES, C_MV + (h + 1) * LANES)
            p_ref[r0:r0 + n, sl] = p_ref[r0:r0 + n, sl] + one96

    order = ([C_HF, C_HF + 4 * H_WIDTH] + list(range(C_MQ, C_MV, PIECE)) + list(range(C_RQ, C_RV, PIECE))
             + list(range(C_MV, C_MO, PIECE)) + list(range(C_HF + PIECE, C_HF + 4 * H_WIDTH, PIECE))
             + list(range(C_MO, C_RQ, PIECE)) + list(range(C_RV, C_HF, PIECE)))
    assert sorted(order) == list(range(0, NP_IN, PIECE))
    pos = {c0: i for i, c0 in enumerate(order)}
    vec = []
    for r0 in range(0, TB, 64):
        vec.append((pos[C_HF + 4 * H_WIDTH], functools.partial(gate_math, r0, 64)))
    for half in range(2):
        for r0 in range(0, TB, 32):
            vec.append((pos[C_MQ + half * HP + HP - PIECE], functools.partial(conv_math, half, r0, 32)))
    for col, scale in ((C_RQ, 1.0), (C_RK, R_DH ** -0.5)):
        for r0 in range(0, TB, 64):
            vec.append((pos[col + HP - PIECE], functools.partial(rope_math, col, scale, r0, 64)))
    for r0 in range(0, TB, 128):
        vec.append((pos[C_MV + HP - PIECE], functools.partial(value_math, r0, 128)))
    per_piece = -(-len(vec) // (len(order) - 6))
    vi = 0
    for i, c0 in enumerate(order):
        project(c0)
        done = 0
        while vi < len(vec) and vec[vi][0] < i and done < per_piece:
            vec[vi][1]()
            vi += 1
            done += 1
    while vi < len(vec):
        vec[vi][1]()
        vi += 1
    pre_ref[SUBLANES - (CONV_W - 1):SUBLANES, :] = pre_ref[TB + SUBLANES - (CONV_W - 1):TB + SUBLANES, :]

    tril = lax.broadcasted_iota(jnp.int32, (L, L), 0) >= lax.broadcasted_iota(jnp.int32, (L, L), 1)
    bd = (lax.broadcasted_iota(jnp.int32, (LANES, LANES), 0) < H_DK) == (_lane((LANES, LANES)) < H_DK)
    low = _lane((L, LANES)) < H_DK
    lane = _lane((L, LANES))

    def intra(c, carry):
        rows = pl.ds(pl.multiple_of(c * L, L), L)

        ga = p_ref[rows, C_IF:C_IF + LANES]
        cum = _cumsum_rows(ga)
        bc = _cumsum_rows(p_ref[rows, C_HF:C_HF + H_WIDTH])

        def head(col, h, dt=None):
            a = p_ref[rows, col + h * LANES:col + (h + 1) * LANES]
            return a if dt is None else a.astype(dt)

        qk_r = [_dot_nt(head(C_RQ, h, BF16), head(C_RK, h, BF16)) for h in range(N_HEADS)]
        qk_m = [_dot_nt(head(C_MQ, h, BF16), head(C_MK, h, BF16)) for h in range(N_HEADS)]

        qh = p_ref[rows, C_HQ:C_HQ + H_WIDTH]
        khh = p_ref[rows, C_HK:C_HK + H_WIDTH]
        head_sel = [jnp.where((_lane((1, H_WIDTH)) & H_DK) == hh * H_DK, 1.0, 0.0).astype(BF16) for hh in range(2)]
        for n in range(NL + 1):
            if n < NL:
                fac = jnp.exp(-jnp.abs(bc - _mid_row_broadcast(bc, L >> (n + 1))))
                qn, kn = (qh * fac).astype(BF16), (khh * fac).astype(BF16)
            else:
                qn, kn = qh.astype(BF16), khh.astype(BF16)
            qt_ref[n, 0] = qn * head_sel[0]
            qt_ref[n, 1] = qn * head_sel[1]
            kt_ref[n] = kn
        e_pre = jnp.exp(bc)
        qi_ref[rows, :] = (qh * e_pre).astype(BF16)
        dh_ref[pl.ds(c, 1), :] = e_pre[L - 1:L, :]
        k_suf = khh * jnp.exp(bc[L - 1:L, :] - bc)

        for h in range(N_HEADS):
            v = head(C_RV, h, BF16)
            s = (qk_r[h] * dm_ref[h]).astype(BF16)
            ho_ref[rows, HP + h * LANES:HP + (h + 1) * LANES] = _dot(s, v)
            kd_t = (head(C_RK, h) * dec_ref[L:2 * L, h * LANES:(h + 1) * LANES]).T.astype(BF16)
            ur_ref[c, h] = _dot(kd_t, v)

        lv = lv_ref[...]
        heads = [(p, hh) for p in range(N_HEADS // 2) for hh in range(2)]

        def level_scores(p, hh):
            psl = slice(p * LANES, (p + 1) * LANES)
            return [_dot_nt(qt_ref[n, hh, :, psl], kt_ref[n, :, psl]) for n in range(NL + 1)]

        def fold(scores):
            s_mat = jnp.zeros((L, L), F32)
            for n, sc in enumerate(scores):
                s_mat = jnp.where(lv == n, sc, s_mat)
            return s_mat.astype(BF16)

        pending = level_scores(*heads[0])

        cum_ref[rows, :] = cum
        a_all = ga - pltpu.roll(cum, LANES - N_HEADS, 1)
        a_t = a_all.T
        ml = jnp.zeros((L, LANES), F32)
        for h in range(N_HEADS):
            b_col = cum[:, N_HEADS + h:N_HEADS + h + 1]
            log_d = jnp.where(tril, b_col + a_t[h:h + 1, :], -jnp.inf)
            m_loc = jnp.max(log_d, axis=1, keepdims=True)
            pm_ref[h] = (qk_m[h] * jnp.exp(log_d - m_loc)).astype(BF16)
            ml = jnp.where(lane == h, m_loc, ml)
            w_loc = jnp.exp(b_col[L - 1:L, :] + a_all[:, h:h + 1] - m_loc[L - 1:L, :])
            kw_ref[h] = (head(C_MK, h) * w_loc).T.astype(BF16)
        ml_ref[rows, :] = ml

        s_bf = []
        for i in range(len(heads)):
            nxt = level_scores(*heads[i + 1]) if i + 1 < len(heads) else None
            s_bf.append(fold(pending))
            pending = nxt
            if i == 0:
                for h in range(N_HEADS):
                    v = head(C_MV, h, BF16)
                    ho_ref[rows, h * LANES:(h + 1) * LANES] = _dot(pm_ref[h], v)
                    u_ref[c, h] = _dot(kw_ref[h], v)
        for p in range(N_HEADS // 2):
            psl = slice(p * LANES, (p + 1) * LANES)
            vp = p_ref[rows, C_HI + p * LANES:C_HI + (p + 1) * LANES]
            ho_ref[rows, 2 * HP + p * LANES:2 * HP + (p + 1) * LANES] = (
                _dot(s_bf[2 * p], jnp.where(low, vp, 0.0).astype(BF16))
                + _dot(s_bf[2 * p + 1], jnp.where(low, 0.0, vp).astype(BF16)))
            uh_ref[c, p] = jnp.where(bd, _dot(k_suf[:, psl].T.astype(BF16), vp.astype(BF16)), 0.0)
        return carry

    lax.fori_loop(0, n_chunks, intra, 0, unroll=2)

    for c in range(n_chunks):
        rows = slice(c * L, (c + 1) * L)

        ml = ml_ref[rows, :]
        b_all = pltpu.roll(cum_ref[rows, :], LANES - N_HEADS, 1)
        m_prev = m_ref[0:1, :]
        m_t = jnp.maximum(ml, b_all + m_prev)
        gr_ref[rows, :] = jnp.exp(ml - m_t)
        gw_ref[rows, :] = jnp.exp(b_all + m_prev - m_t)
        ge_ref[rows, :] = jnp.exp(-m_t)
        m_new = m_t[L - 1:L, :]
        dec_row = jnp.exp(b_all[L - 1:L, :] + m_prev - m_new)
        g_row = jnp.exp(ml[L - 1:L, :] - m_new)
        m_ref[0:1, :] = m_new

        for h in range(N_HEADS):
            q = p_ref[rows, C_MQ + h * LANES:C_MQ + (h + 1) * LANES].astype(BF16)
            it_ref[rows, h * LANES:(h + 1) * LANES] = _dot(q, c_ref[h].astype(BF16))
        for h in range(N_HEADS):
            q = p_ref[rows, C_RQ + h * LANES:C_RQ + (h + 1) * LANES]
            qd = (q * dec_ref[0:L, h * LANES:(h + 1) * LANES]).astype(BF16)
            it_ref[rows, HP + h * LANES:HP + (h + 1) * LANES] = _dot(qd, r_ref[h].astype(BF16))
        for p in range(N_HEADS // 2):
            psl = slice(p * LANES, (p + 1) * LANES)
            it_ref[rows, 2 * HP + p * LANES:2 * HP + (p + 1) * LANES] = _dot(qi_ref[rows, psl], s_ref[p].astype(BF16))

        d_col = dh_ref[c:c + 1, :].T
        for h in range(N_HEADS):
            c_ref[h] = dec_row[:, h:h + 1] * c_ref[h] + g_row[:, h:h + 1] * u_ref[c, h]
            r_ref[h] = math.exp(L * LG[h]) * r_ref[h] + ur_ref[c, h]
        for p in range(N_HEADS // 2):
            s_ref[p] = d_col[p * LANES:(p + 1) * LANES, :] * s_ref[p] + uh_ref[c, p]

    def mix_rows(rows, T):
        valid = _lane((T, LANES)) < M_DH
        low_t = _lane((T, LANES)) < H_DK
        parts = []
        for h in range(N_HEADS):
            sl = slice(h * LANES, (h + 1) * LANES)
            num = gr_ref[rows, h:h + 1] * ho_ref[rows, sl] + gw_ref[rows, h:h + 1] * it_ref[rows, sl]
            den = num[:, M_DH:M_DH + 1]
            hm = num / jnp.maximum(jnp.abs(den), ge_ref[rows, h:h + 1])
            o = p_ref[rows, C_MO + h * LANES:C_MO + (h + 1) * LANES]
            z = p_ref[rows, C_MZ + h * LANES:C_MZ + (h + 1) * LANES]
            gain = vec_ref[V_MNW:V_MNW + 1, sl]
            parts.append((_head_layer_norm_padded(hm, valid) * gain * _sigmoid(o) * _silu(z)).astype(BF16))
        for h in range(N_HEADS):
            sl = slice(HP + h * LANES, HP + (h + 1) * LANES)
            hn = _head_layer_norm_padded(ho_ref[rows, sl] + it_ref[rows, sl], valid)
            g = p_ref[rows, C_RG + h * LANES:C_RG + (h + 1) * LANES]
            parts.append((hn * vec_ref[V_RNW:V_RNW + 1, h * LANES:(h + 1) * LANES] * _silu(g)).astype(BF16))
        for p in range(N_HEADS // 2):
            sl = slice(2 * HP + p * LANES, 2 * HP + (p + 1) * LANES)
            hn = _pair_rms_norm(ho_ref[rows, sl] + it_ref[rows, sl], low_t)
            g = p_ref[rows, C_HG + p * LANES:C_HG + (p + 1) * LANES]
            parts.append((hn * vec_ref[V_HNW:V_HNW + 1, p * LANES:(p + 1) * LANES] * _silu(g)).astype(BF16))
        return jnp.concatenate(parts, axis=-1)

    def out_rows(rows, proj):
        y_ref[0, rows, :] = _layer_norm_rows(ALPHA * x_ref[0, rows, :] + proj,
                                             vec_ref[V_LNG:V_LNG + 1, :], vec_ref[V_LNB:V_LNB + 1, :])

    half = TB // 2
    rows0, rows1 = slice(0, half), slice(half, TB)
    proj0 = _dot(mix_rows(rows0, half), wout_ref[...])
    mix1 = mix_rows(rows1, half)
    out_rows(rows0, proj0)
    out_rows(rows1, _dot(mix1, wout_ref[...]))

    @pl.when(j == n_tb - 1)
    def _final_states():
        convout_ref[0] = pre_ref[SUBLANES - (CONV_W - 1):SUBLANES, :]
        for h in range(N_HEADS):
            c_aug = c_ref[h]
            cout_ref[0, h] = c_aug[0:M_DH, 0:M_DH]
            nout_ref[0, h:h + 1, :] = c_aug.T[M_DH:M_DH + 1, 0:M_DH]
            r_full = r_ref[h]
            rout_ref[0, h] = jnp.concatenate(
                [r_full[0:ROPE_HALF, 0:R_DH], r_full[LANES // 2:LANES // 2 + ROPE_HALF, 0:R_DH]], axis=0)
            p, hh = divmod(h, 2)
            sout_ref[0, h] = s_ref[p][hh * H_DK:(hh + 1) * H_DK, hh * H_DK:(hh + 1) * H_DK]
        mout_ref[0] = m_ref[0:1, 0:N_HEADS]


def _full_spec(shape):
    nd = len(shape)
    return pl.BlockSpec(shape, lambda *_: (0,) * nd)


def _prefill_layer(layer, x, win, wout, cos, sin, vec, hlb, lv, TB):
    B, T, _ = x.shape
    n_tb = T // TB
    n_chunks = TB // CHUNK
    kern = functools.partial(_prefill_kernel, layer, TB)

    def layer_spec(a, single_buffer=False):
        kw = dict(pipeline_mode=pl.Buffered(1)) if single_buffer else {}
        return pl.BlockSpec((None,) + a.shape[1:], lambda b, j: (layer,) + (0,) * (a.ndim - 1), **kw)

    in_specs = [
        pl.BlockSpec((1, TB, D_MODEL), lambda b, j: (b, j, 0)),
        layer_spec(win, True), layer_spec(wout, True),
        pl.BlockSpec((TB, LANES), lambda b, j: (j, 0)),
        pl.BlockSpec((TB, LANES), lambda b, j: (j, 0)),
        layer_spec(vec), _full_spec(hlb.shape), _full_spec(lv.shape),
    ]
    out_shape = (
        jax.ShapeDtypeStruct((B, T, D_MODEL), F32),
        jax.ShapeDtypeStruct((B, N_HEADS, M_DH, M_DH), F32),
        jax.ShapeDtypeStruct((B, N_HEADS, M_DH), F32),
        jax.ShapeDtypeStruct((B, 1, N_HEADS), F32),
        jax.ShapeDtypeStruct((B, CONV_W - 1, 2 * HP), F32),
        jax.ShapeDtypeStruct((B, N_HEADS, R_DH, R_DH), F32),
        jax.ShapeDtypeStruct((B, N_HEADS, H_DK, H_DK), F32),
    )
    out_specs = (
        pl.BlockSpec((1, TB, D_MODEL), lambda b, j: (b, j, 0)),
        pl.BlockSpec((1, N_HEADS, M_DH, M_DH), lambda b, j: (b, 0, 0, 0)),
        pl.BlockSpec((1, N_HEADS, M_DH), lambda b, j: (b, 0, 0)),
        pl.BlockSpec((1, 1, N_HEADS), lambda b, j: (b, 0, 0)),
        pl.BlockSpec((1, CONV_W - 1, 2 * HP), lambda b, j: (b, 0, 0)),
        pl.BlockSpec((1, N_HEADS, R_DH, R_DH), lambda b, j: (b, 0, 0, 0)),
        pl.BlockSpec((1, N_HEADS, H_DK, H_DK), lambda b, j: (b, 0, 0, 0)),
    )
    L = CHUNK
    scratch = [
        pltpu.VMEM((TB, NP_SCR), F32),
        pltpu.VMEM((TB + SUBLANES, 2 * HP), F32),
        pltpu.VMEM((TB, MIXP), F32),
        pltpu.VMEM((TB, LANES), F32),
        pltpu.VMEM((TB, LANES), F32),
        pltpu.VMEM((N_HEADS, L, L), BF16),
        pltpu.VMEM((N_HEADS, LANES, L), BF16),
        pltpu.VMEM((N_LEVELS + 1, 2, L, H_WIDTH), BF16),
        pltpu.VMEM((N_LEVELS + 1, L, H_WIDTH), BF16),
        pltpu.VMEM((TB, H_WIDTH), BF16),
        pltpu.VMEM((max(n_chunks, SUBLANES), H_WIDTH), F32),
        pltpu.VMEM((n_chunks, N_HEADS, LANES, LANES), F32),
        pltpu.VMEM((n_chunks, N_HEADS, LANES, LANES), F32),
        pltpu.VMEM((n_chunks, N_HEADS // 2, LANES, LANES), F32),
        pltpu.VMEM((N_HEADS, LANES, LANES), F32),
        pltpu.VMEM((N_HEADS, LANES, LANES), F32),
        pltpu.VMEM((N_HEADS // 2, LANES, LANES), F32),
        pltpu.VMEM((SUBLANES, LANES), F32),
        pltpu.VMEM((N_HEADS, L, L), F32),
        pltpu.VMEM((2 * L, HP), F32),
        pltpu.VMEM((TB, D_MODEL), BF16),
        pltpu.VMEM((TB, MIXP), F32),
        pltpu.VMEM((TB, LANES), F32),
        pltpu.VMEM((TB, LANES), F32),
        pltpu.VMEM((TB, LANES), F32),
    ]
    return pl.pallas_call(
        kern, grid=(B, n_tb), in_specs=in_specs, out_specs=out_specs, out_shape=out_shape,
        scratch_shapes=scratch,
        compiler_params=pltpu.CompilerParams(dimension_semantics=("arbitrary", "arbitrary"),
                                             vmem_limit_bytes=VMEM_LIMIT),
        name=f"prefill_layer{layer}",
    )(x, win, wout, cos, sin, vec, hlb, lv)


DEC_KB = 4
KR_M = M_DH // DEC_KB
KR_H = H_DK // DEC_KB
G_DEC, G_WS, G_EN, G_GAM, G_SM, G_QN, G_SR, G_SH = (i * N_HEADS for i in range(8))


def _stream_state(q_blk, dec, kw_blk, v, in_ref, out_ref, nrows):
    acc = jnp.zeros_like(v)
    for kk in range(nrows):
        st = in_ref[0, 0, kk]
        d = dec if dec.shape[0] == 1 else dec[kk:kk + 1, :]
        acc = acc + q_blk[kk:kk + 1, :] * st
        out_ref[0, 0, kk] = d * st + kw_blk[kk:kk + 1, :] * v
    return acc


def _decode_kernel(x_ref, wt_ref, wout_ref, cos_ref, sin_ref, dvec_ref, dcol_ref, hlbt_ref, vec_ref,
                   cin_ref, nin_ref, min_ref, convin_ref, rin_ref, sin_st_ref,
                   y_ref, cout_ref, nout_ref, mout_ref, convout_ref, rout_ref, sout_ref,
                   p_ref, xs_ref, ht_ref, mix_ref, g_ref):
    l = pl.program_id(0)
    h = pl.program_id(1)
    kb = pl.program_id(2)
    first = jnp.logical_and(h == 0, kb == 0)
    last = jnp.logical_and(h == N_HEADS - 1, kb == DEC_KB - 1)
    B = x_ref.shape[0]

    @pl.when(jnp.logical_and(first, l == 0))
    def _load_x():
        xs_ref[...] = x_ref[...]

    @pl.when(first)
    def _project():
        p_ref[0:N_IN, :] = _dot_nt(wt_ref[0], xs_ref[...].astype(BF16))
        ht_ref[...] = jnp.zeros_like(ht_ref)

        pre = p_ref[O_MQ:O_MQ + 2 * M_WIDTH, :].T
        cw = dvec_ref[0, 0:CONV_W, :]
        acc = dvec_ref[0, CONV_W:CONV_W + 1, :] + cw[CONV_W - 1:CONV_W, :] * pre
        for jj in range(CONV_W - 1):
            acc = acc + cw[jj:jj + 1, :] * convin_ref[0, jj]
        for jj in range(CONV_W - 2):
            convout_ref[0, jj] = convin_ref[0, jj + 1]
        convout_ref[0, CONV_W - 2] = pre
        p_ref[O_MQ:O_MQ + 2 * M_WIDTH, :] = _silu(acc).T

        gt = p_ref[O_IF:O_IF + 2 * N_HEADS, :] + dcol_ref[0, D_MIX:D_MIX + 2 * N_HEADS, :]
        ig = gt[0:N_HEADS]
        lf = _log_sigmoid(gt[N_HEADS:2 * N_HEADS])
        m_prev = min_ref[0]
        m_t = jnp.maximum(ig, lf + m_prev)
        dec = jnp.exp(lf + m_prev - m_t)
        w_s = jnp.exp(ig - m_t)
        mout_ref[0] = m_t
        g_ref[G_DEC:G_DEC + N_HEADS, :] = dec
        g_ref[G_WS:G_WS + N_HEADS, :] = w_s
        g_ref[G_EN:G_EN + N_HEADS, :] = jnp.exp(-m_t)
        for hh in range(N_HEADS):
            q = p_ref[O_MQ + hh * M_DH:O_MQ + (hh + 1) * M_DH, :]
            k = p_ref[O_MK + hh * M_DH:O_MK + (hh + 1) * M_DH, :] * (M_DH ** -0.5)
            n_old = nin_ref[0, hh]
            d_h = dec[hh:hh + 1]
            kw = k * w_s[hh:hh + 1]
            g_ref[G_SM + hh:G_SM + hh + 1, :] = jnp.sum(q * k, axis=0, keepdims=True) * w_s[hh:hh + 1]
            g_ref[G_QN + hh:G_QN + hh + 1, :] = jnp.sum(q * n_old, axis=0, keepdims=True)
            nout_ref[0, hh] = d_h * n_old + kw
            p_ref[O_MK + hh * M_DH:O_MK + (hh + 1) * M_DH, :] = kw

        cos = cos_ref[...]
        sin = sin_ref[...]
        for hh in range(N_HEADS):
            g_ref[G_GAM + hh:G_GAM + hh + 1, :] = jnp.full((1, B), math.exp(LG[hh]), F32)
            rot = []
            for off, scale in ((O_RQ, 1.0), (O_RK, R_DH ** -0.5)):
                x1 = p_ref[off + hh * R_DH:off + hh * R_DH + ROPE_HALF, :]
                x2 = p_ref[off + hh * R_DH + ROPE_HALF:off + (hh + 1) * R_DH, :]
                r1 = (x1 * cos - x2 * sin) * scale
                r2 = (x1 * sin + x2 * cos) * scale
                p_ref[off + hh * R_DH:off + hh * R_DH + ROPE_HALF, :] = r1
                p_ref[off + hh * R_DH + ROPE_HALF:off + (hh + 1) * R_DH, :] = r2
                rot.append((r1, r2))
            (q1, q2), (k1, k2) = rot
            g_ref[G_SR + hh:G_SR + hh + 1, :] = (jnp.sum(q1 * k1, axis=0, keepdims=True)
                                                 + jnp.sum(q2 * k2, axis=0, keepdims=True))

        lbs = _hgrn_lower_bounds(hlbt_ref[...], 1)
        lb = lbs[0]
        for i in range(1, DEPTH):
            lb = jnp.where(l == i, lbs[i], lb)
        lfh, kh = _hgrn_gates(p_ref[O_HF:O_HF + H_WIDTH, :], lb)
        eh = jnp.exp(lfh)
        qh = p_ref[O_HQ:O_HQ + H_WIDTH, :]
        p_ref[O_HF:O_HF + H_WIDTH, :] = eh
        p_ref[O_HK:O_HK + H_WIDTH, :] = kh
        p_ref[O_HQ:O_HQ + H_WIDTH, :] = qh * eh
        qk = qh * kh
        for hh in range(N_HEADS):
            g_ref[G_SH + hh:G_SH + hh + 1, :] = jnp.sum(qk[hh * H_DK:(hh + 1) * H_DK], axis=0, keepdims=True)

    def rows(base, width, n):
        return pl.ds(pl.multiple_of(base + h * width + kb * n, SUBLANES), n)

    dec = g_ref[pl.ds(G_DEC + h, 1), :]
    acc = _stream_state(p_ref[rows(O_MQ, M_DH, KR_M), :], dec, p_ref[rows(O_MK, M_DH, KR_M), :],
                        p_ref[pl.ds(pl.multiple_of(O_MV + h * M_DH, SUBLANES), M_DH), :], cin_ref, cout_ref, KR_M)
    ht_ref[pl.ds(pl.multiple_of(h * M_DH, SUBLANES), M_DH), :] += acc

    gam = g_ref[pl.ds(G_GAM + h, 1), :]
    acc = _stream_state(p_ref[rows(O_RQ, R_DH, KR_M), :], gam, p_ref[rows(O_RK, R_DH, KR_M), :],
                        p_ref[pl.ds(pl.multiple_of(O_RV + h * R_DH, SUBLANES), R_DH), :], rin_ref, rout_ref, KR_M)
    ht_ref[pl.ds(pl.multiple_of(M_WIDTH + h * R_DH, SUBLANES), R_DH), :] += acc

    acc = _stream_state(p_ref[rows(O_HQ, H_DK, KR_H), :], p_ref[rows(O_HF, H_DK, KR_H), :],
                        p_ref[rows(O_HK, H_DK, KR_H), :],
                        p_ref[pl.ds(pl.multiple_of(O_HI + h * H_DK, SUBLANES), H_DK), :], sin_st_ref, sout_ref, KR_H)
    ht_ref[pl.ds(pl.multiple_of(M_WIDTH + R_WIDTH + h * H_DK, SUBLANES), H_DK), :] += acc

    @pl.when(last)
    def _finish():
        def head_ln(a):
            mu = jnp.mean(a, axis=0, keepdims=True)
            d = a - mu
            return d * lax.rsqrt(jnp.mean(d * d, axis=0, keepdims=True) + HEAD_EPS)

        for hh in range(N_HEADS):
            sl = slice(hh * M_DH, (hh + 1) * M_DH)
            v = p_ref[O_MV + hh * M_DH:O_MV + (hh + 1) * M_DH, :]
            d_h = g_ref[G_DEC + hh:G_DEC + hh + 1, :]
            s = g_ref[G_SM + hh:G_SM + hh + 1, :]
            num = s * v + ht_ref[sl, :] * d_h
            den = s + g_ref[G_QN + hh:G_QN + hh + 1, :] * d_h
            hm = num / jnp.maximum(jnp.abs(den), g_ref[G_EN + hh:G_EN + hh + 1, :])
            o = p_ref[O_MO + hh * M_DH:O_MO + (hh + 1) * M_DH, :]
            z = p_ref[O_MZ + hh * M_DH:O_MZ + (hh + 1) * M_DH, :]
            mix_ref[sl, :] = head_ln(hm) * dcol_ref[0, sl, :] * _sigmoid(o) * _silu(z)
        for hh in range(N_HEADS):
            sl = slice(M_WIDTH + hh * R_DH, M_WIDTH + (hh + 1) * R_DH)
            v = p_ref[O_RV + hh * R_DH:O_RV + (hh + 1) * R_DH, :]
            hr = g_ref[G_SR + hh:G_SR + hh + 1, :] * v + ht_ref[sl, :] * math.exp(LG[hh])
            gg = p_ref[O_RG + hh * R_DH:O_RG + (hh + 1) * R_DH, :]
            mix_ref[sl, :] = head_ln(hr) * dcol_ref[0, sl, :] * _silu(gg)
        for hh in range(N_HEADS):
            sl = slice(M_WIDTH + R_WIDTH + hh * H_DK, M_WIDTH + R_WIDTH + (hh + 1) * H_DK)
            v = p_ref[O_HI + hh * H_DK:O_HI + (hh + 1) * H_DK, :]
            ho = g_ref[G_SH + hh:G_SH + hh + 1, :] * v + ht_ref[sl, :]
            hn = ho * lax.rsqrt(jnp.mean(ho * ho, axis=0, keepdims=True) + HEAD_EPS)
            gg = p_ref[O_HG + hh * H_DK:O_HG + (hh + 1) * H_DK, :]
            mix_ref[sl, :] = hn * dcol_ref[0, sl, :] * _silu(gg)
        x = xs_ref[...]
        out = _dot(mix_ref[...].T.astype(BF16), wout_ref[0])
        y = _layer_norm_rows(ALPHA * x + out, vec_ref[0, V_LNG:V_LNG + 1, :], vec_ref[0, V_LNB:V_LNB + 1, :])
        xs_ref[...] = y
        y_ref[0] = y


def _decode(x, wt, wout, cos, sin, dvec, dcol, hlbt, vec, st_c, st_n, st_m, st_conv, st_r, st_s):
    B = x.shape[0]

    def lspec(a):
        nd = a.ndim
        return pl.BlockSpec((1,) + a.shape[1:], lambda l, h, kb: (l,) + (0,) * (nd - 1))

    def kv_spec(a, n):
        return pl.BlockSpec((1, 1, n) + a.shape[3:], lambda l, h, kb: (l, h, kb, 0, 0))

    in_specs = [
        _full_spec(x.shape), lspec(wt), lspec(wout), _full_spec(cos.shape), _full_spec(sin.shape),
        lspec(dvec), lspec(dcol), _full_spec(hlbt.shape), lspec(vec),
        kv_spec(st_c, KR_M), lspec(st_n), lspec(st_m), lspec(st_conv), kv_spec(st_r, KR_M), kv_spec(st_s, KR_H),
    ]
    out_shape = (
        jax.ShapeDtypeStruct((DEPTH, B, D_MODEL), F32),
        jax.ShapeDtypeStruct(st_c.shape, F32), jax.ShapeDtypeStruct(st_n.shape, F32),
        jax.ShapeDtypeStruct(st_m.shape, F32), jax.ShapeDtypeStruct(st_conv.shape, F32),
        jax.ShapeDtypeStruct(st_r.shape, F32), jax.ShapeDtypeStruct(st_s.shape, F32),
    )
    out_specs = (
        pl.BlockSpec((1, B, D_MODEL), lambda l, h, kb: (l, 0, 0)),
        kv_spec(st_c, KR_M), lspec(st_n), lspec(st_m), lspec(st_conv), kv_spec(st_r, KR_M), kv_spec(st_s, KR_H),
    )
    scratch = [
        pltpu.VMEM((N_IN + H_WIDTH, B), F32),
        pltpu.VMEM((B, D_MODEL), F32),
        pltpu.VMEM((D_MIX, B), F32),
        pltpu.VMEM((D_MIX, B), F32),
        pltpu.VMEM((8 * N_HEADS, B), F32),
    ]
    return pl.pallas_call(
        _decode_kernel, grid=(DEPTH, N_HEADS, DEC_KB), in_specs=in_specs, out_specs=out_specs,
        out_shape=out_shape, scratch_shapes=scratch,
        compiler_params=pltpu.CompilerParams(dimension_semantics=("arbitrary", "arbitrary", "arbitrary"),
                                             vmem_limit_bytes=VMEM_LIMIT),
        name="decode_step",
    )(x, wt, wout, cos, sin, dvec, dcol, hlbt, vec, st_c, st_n, st_m, st_conv, st_r, st_s)


def _rope_angles(pos):
    inv = ROPE_BASE ** (-jnp.arange(ROPE_HALF, dtype=F32) / ROPE_HALF)
    ang = pos.astype(F32)[:, None] * inv[None, :]
    return jnp.cos(ang), jnp.sin(ang)


PREFILL_BLOCK = 512


def kernel(x_prompt, x_sample, state_mlstm_C, state_mlstm_n, state_mlstm_m, state_mlstm_conv, state_ret, state_hgrn, w_in, conv_w, conv_b, b_mgate, m_norm_w, r_norm_w, h_norm_w, hgrn_lb, w_out, ln_g, ln_b):
    B, T, _ = x_prompt.shape

    wt = jnp.swapaxes(w_in, 1, 2).astype(BF16)
    win_p = jnp.concatenate(
        [_pad_heads(wt[:, O_MQ:O_IF], 1),
         _pad_rope_heads(wt[:, O_RQ:O_RV], 1), _pad_heads(wt[:, O_RV:O_HF], 1),
         wt[:, O_HF:N_IN],
         jnp.pad(wt[:, O_IF:O_RQ], ((0, 0), (0, LANES - 2 * N_HEADS), (0, 0)))], axis=1)
    wout_b = w_out.astype(BF16)
    wout_p = jnp.concatenate([_pad_heads(wout_b[:, 0:M_WIDTH + R_WIDTH], 1), wout_b[:, M_WIDTH + R_WIDTH:]],
                             axis=1)

    def row(a):
        return jnp.pad(a, ((0, 0), (0, D_MODEL - a.shape[-1])))[:, None, :]

    vec = jnp.concatenate([_pad_heads(conv_w, 2), row(_pad_heads(conv_b, 1)), row(b_mgate),
                           row(_pad_heads(m_norm_w, 1)), row(_pad_heads(r_norm_w, 1)), row(h_norm_w),
                           row(ln_g), row(ln_b)], axis=1)
    vec = jnp.pad(vec, ((0, 0), (0, V_ROWS - vec.shape[1]), (0, 0)))
    dvec = jnp.pad(jnp.concatenate([conv_w, conv_b[:, None, :]], axis=1),
                   ((0, 0), (0, SUBLANES - CONV_W - 1), (0, 0)))
    dcol = jnp.concatenate([m_norm_w, r_norm_w, h_norm_w, b_mgate], axis=1)[:, :, None]
    hlb = hgrn_lb.astype(F32)

    lv = jnp.asarray(_level_table(CHUNK))

    c, s = _rope_angles(jnp.arange(T, dtype=jnp.int32))
    z = jnp.zeros((T, LANES // 2 - ROPE_HALF), F32)
    cos_p = jnp.concatenate([c, z, c, z], axis=1)
    sin_p = jnp.concatenate([-s, z, s, z], axis=1)

    hp = x_prompt
    st = [[] for _ in range(6)]
    for l in range(DEPTH):
        outs = _prefill_layer(l, hp, win_p, wout_p, cos_p, sin_p, vec, hlb, lv, PREFILL_BLOCK)
        hp = outs[0]
        for k in range(6):
            st[k].append(outs[1 + k])
    mC_p = jnp.stack(st[0])
    mn_p = jnp.stack(st[1])
    mm_p = jnp.stack(st[2])[:, :, 0, :]
    conv_p = _unpad_heads(jnp.stack(st[3]))
    ret_p = jnp.stack(st[4])
    hgrn_p = jnp.stack(st[5])

    n_s = x_sample.shape[0]
    cs, ss = _rope_angles(PAST_LEN + jnp.arange(x_sample.shape[1], dtype=jnp.int32))
    cos_s = jnp.broadcast_to(cs[0][:, None], (ROPE_HALF, n_s))
    sin_s = jnp.broadcast_to(ss[0][:, None], (ROPE_HALF, n_s))
    ys, c_t, n_t, m_t, conv_t, r_t, s_t = _decode(
        x_sample[:, 0, :], wt, wout_b, cos_s, sin_s, dvec, dcol, hlb.T, vec,
        jnp.transpose(state_mlstm_C, (0, 2, 3, 4, 1)), jnp.transpose(state_mlstm_n, (0, 2, 3, 1)),
        jnp.transpose(state_mlstm_m, (0, 2, 1)), jnp.transpose(state_mlstm_conv, (0, 2, 1, 3)),
        jnp.transpose(state_ret, (0, 2, 3, 4, 1)), jnp.transpose(state_hgrn, (0, 2, 3, 4, 1)))
    hs = ys[DEPTH - 1][:, None, :]
    mC_s = jnp.transpose(c_t, (0, 4, 1, 2, 3))
    mn_s = jnp.transpose(n_t, (0, 3, 1, 2))
    mm_s = jnp.transpose(m_t, (0, 2, 1))
    conv_s = jnp.transpose(conv_t, (0, 2, 1, 3))
    ret_s = jnp.transpose(r_t, (0, 4, 1, 2, 3))
    hgrn_s = jnp.transpose(s_t, (0, 4, 1, 2, 3))

    return (hp, hs, mC_p, mn_p, mm_p, conv_p, ret_p, hgrn_p, mC_s, mn_s, mm_s, conv_s, ret_s, hgrn_s)
```

```python
import functools
import math

import numpy as np
import jax
import jax.numpy as jnp
from jax import lax
from jax.experimental import pallas as pl
from jax.experimental.pallas import tpu as pltpu

F32 = jnp.float32
BF16 = jnp.bfloat16

D_MODEL = 1024
DEPTH = 2
PAST_LEN = 16384
N_HEADS = 4
M_DH = 96
R_DH = 96
H_DK = 64
M_WIDTH = N_HEADS * M_DH
R_WIDTH = N_HEADS * R_DH
H_WIDTH = N_HEADS * H_DK
D_MIX = M_WIDTH + R_WIDTH + H_WIDTH
CONV_W = 4
ROPE_BASE = 10000.0
LN_EPS = 1e-5
HEAD_EPS = 1e-6
ALPHA = (2 * DEPTH) ** 0.25

LANES = 128
SUBLANES = 8
HP = N_HEADS * LANES
ROPE_HALF = R_DH // 2
VMEM_LIMIT = 56 * 1024 * 1024

O_MQ, O_MK, O_MV, O_MO, O_MZ = 0, M_WIDTH, 2 * M_WIDTH, 3 * M_WIDTH, 4 * M_WIDTH
O_IF = 5 * M_WIDTH
O_RQ = O_IF + 2 * N_HEADS
O_RK, O_RV, O_RG = O_RQ + R_WIDTH, O_RQ + 2 * R_WIDTH, O_RQ + 3 * R_WIDTH
O_HF = O_RQ + 4 * R_WIDTH
O_HI, O_HQ, O_HG = O_HF + H_WIDTH, O_HF + 2 * H_WIDTH, O_HF + 3 * H_WIDTH
N_IN = O_HF + 4 * H_WIDTH
O_HK = N_IN

C_MQ, C_MK, C_MV, C_MO, C_MZ = 0, HP, 2 * HP, 3 * HP, 4 * HP
C_RQ, C_RK, C_RV, C_RG = 5 * HP, 6 * HP, 7 * HP, 8 * HP
C_HF = 9 * HP
C_HI = C_HF + H_WIDTH
C_HQ = C_HI + H_WIDTH
C_HG = C_HQ + H_WIDTH
C_IF = C_HG + H_WIDTH
NP_IN = C_IF + LANES
C_HK = NP_IN
NP_SCR = NP_IN + H_WIDTH
MIXP = 2 * HP + H_WIDTH

V_CONVW, V_CONVB, V_BIF, V_MNW, V_RNW, V_HNW, V_LNG, V_LNB = 0, CONV_W, CONV_W + 1, CONV_W + 2, CONV_W + 3, CONV_W + 4, CONV_W + 5, CONV_W + 6
V_ROWS = 2 * SUBLANES
V_CONV8 = V_ROWS
V_ROWS8 = V_ROWS + (CONV_W + 1) * SUBLANES

CHUNK = 128
N_LEVELS = int(math.log2(CHUNK))
LG = [math.log1p(-2.0 ** (-5.0 - h)) for h in range(N_HEADS)]


def _pad_groups(a, axis, group, padded):
    shp = a.shape
    n = shp[axis] // group
    a = a.reshape(shp[:axis] + (n, group) + shp[axis + 1:])
    pad = [(0, 0)] * a.ndim
    pad[axis + 1] = (0, padded - group)
    return jnp.pad(a, pad).reshape(shp[:axis] + (n * padded,) + shp[axis + 1:])


def _pad_heads(a, axis):
    return _pad_groups(a, axis, M_DH, LANES)


def _pad_rope_heads(a, axis):
    return _pad_groups(a, axis, ROPE_HALF, LANES // 2)


def _unpad_heads(a, dh=M_DH):
    shp = a.shape
    return a.reshape(shp[:-1] + (shp[-1] // LANES, LANES))[..., :dh].reshape(shp[:-1] + (shp[-1] // LANES * dh,))


def _level_table(L):
    nl = int(math.log2(L))
    tt = np.arange(L)[:, None]
    ss = np.arange(L)[None, :]
    lv = np.full((L, L), -1, np.int32)
    for n in range(nl):
        w = L >> (n + 1)
        same = (tt // (2 * w)) == (ss // (2 * w))
        cond = same & ((tt % (2 * w)) >= w) & ((ss % (2 * w)) < w)
        lv = np.where(cond, n, lv)
    lv = np.where(tt == ss, nl, lv)
    return lv.astype(np.int32)


def _dot(a, b):
    return jnp.dot(a, b, preferred_element_type=F32)


def _dot_nt(a, b):
    return lax.dot_general(a, b, (((1,), (1,)), ((), ())), preferred_element_type=F32)


def _sigmoid(x):
    return 1.0 / (1.0 + jnp.exp(-x))


def _silu(x):
    return x * _sigmoid(x)


def _log_sigmoid(x):
    return jnp.minimum(x, 0.0) - jnp.log1p(jnp.exp(-jnp.abs(x)))


def _lane(shape):
    return lax.broadcasted_iota(jnp.int32, shape, len(shape) - 1)


def _hgrn_lower_bounds(hlb, layer_axis):
    def take(a, l):
        return a[l:l + 1] if layer_axis == 0 else a[:, l:l + 1]
    mx = jnp.max(hlb, axis=layer_axis, keepdims=True)
    e = jnp.exp(hlb - mx)
    p = e / jnp.sum(e, axis=layer_axis, keepdims=True)
    out = []
    acc = None
    for l in range(DEPTH):
        acc = take(p, l) if acc is None else acc + take(p, l)
        out.append(acc - take(p, 0))
    return out


def _hgrn_gates(fpre, lb):
    a = jnp.log(lb)
    c = jnp.log1p(-lb) + _log_sigmoid(fpre)
    mx = jnp.maximum(a, c)
    lf = mx + jnp.log(jnp.exp(a - mx) + jnp.exp(c - mx))
    kh = (1.0 - lb) * _sigmoid(-fpre)
    return lf, kh


def _cumsum_rows(x):
    L, W = x.shape
    sub = lax.broadcasted_iota(jnp.int32, (SUBLANES, W), 0)
    out = []
    carry = None
    for r in range(L // SUBLANES):
        t = x[r * SUBLANES:(r + 1) * SUBLANES, :]
        s = 1
        while s < SUBLANES:
            t = t + jnp.where(sub >= s, pltpu.roll(t, s, 0), 0.0)
            s *= 2
        if carry is not None:
            t = t + carry
        carry = t[SUBLANES - 1:SUBLANES, :]
        out.append(t)
    return jnp.concatenate(out, axis=0)


def _mid_row_broadcast(bc, w):
    L, W = bc.shape
    if 2 * w >= SUBLANES:
        blocks = [jnp.broadcast_to(bc[b * 2 * w + w:b * 2 * w + w + 1, :], (2 * w, W)) for b in range(L // (2 * w))]
        return blocks[0] if len(blocks) == 1 else jnp.concatenate(blocks, axis=0)
    x3 = bc.reshape(L // SUBLANES, SUBLANES, W)
    sub = lax.broadcasted_iota(jnp.int32, x3.shape, 1)
    out = None
    for b in range(SUBLANES // (2 * w)):
        piece = jnp.broadcast_to(x3[:, b * 2 * w + w:b * 2 * w + w + 1, :], x3.shape)
        out = piece if out is None else jnp.where(sub >= b * 2 * w, piece, out)
    return out.reshape(L, W)


def _layer_norm_rows(r, g, b):
    mu = jnp.mean(r, axis=-1, keepdims=True)
    d = r - mu
    var = jnp.mean(d * d, axis=-1, keepdims=True)
    return d * lax.rsqrt(var + LN_EPS) * g + b


def _head_layer_norm_padded(h, valid):
    dh = M_DH
    hz = jnp.where(valid, h, 0.0)
    mu = jnp.sum(hz, axis=-1, keepdims=True) * (1.0 / dh)
    d = jnp.where(valid, h - mu, 0.0)
    var = jnp.sum(d * d, axis=-1, keepdims=True) * (1.0 / dh)
    return d * lax.rsqrt(var + HEAD_EPS)


def _pair_rms_norm(h, low):
    sq = h * h
    m0 = jnp.sum(jnp.where(low, sq, 0.0), axis=-1, keepdims=True) * (1.0 / H_DK)
    m1 = jnp.sum(jnp.where(low, 0.0, sq), axis=-1, keepdims=True) * (1.0 / H_DK)
    return h * jnp.where(low, lax.rsqrt(m0 + HEAD_EPS), lax.rsqrt(m1 + HEAD_EPS))


def _prefill_kernel(layer, TB,
                    x_ref, win_ref, wout_ref, cos_ref, sin_ref, vec_ref, hlb_ref, lv_ref,
                    y_ref, cout_ref, nout_ref, mout_ref, convout_ref, rout_ref, sout_ref,
                    p_ref, pre_ref, ho_ref, cum_ref, ml_ref, pm_ref, kw_ref, qt_ref, kt_ref, qi_ref, dh_ref,
                    u_ref, ur_ref, uh_ref, c_ref, r_ref, s_ref, m_ref, dm_ref, dec_ref,
                    xb_ref, it_ref, gr_ref, gw_ref, ge_ref):
    L = CHUNK
    NL = N_LEVELS
    n_chunks = TB // L
    j = pl.program_id(1)
    n_tb = pl.num_programs(1)

    @pl.when(j == 0)
    def _init():
        c_ref[...] = jnp.zeros_like(c_ref)
        r_ref[...] = jnp.zeros_like(r_ref)
        s_ref[...] = jnp.zeros_like(s_ref)
        m_ref[...] = jnp.zeros_like(m_ref)
        pre_ref[0:SUBLANES, :] = jnp.zeros((SUBLANES, 2 * HP), F32)
        ti = lax.broadcasted_iota(jnp.int32, (L, L), 0)
        si = lax.broadcasted_iota(jnp.int32, (L, L), 1)
        dist = (ti - si).astype(F32)
        row = lax.broadcasted_iota(jnp.int32, (L, LANES), 0).astype(F32)
        for h in range(N_HEADS):
            dm_ref[h] = jnp.where(ti >= si, jnp.exp(dist * LG[h]), 0.0)
            dec_ref[0:L, h * LANES:(h + 1) * LANES] = jnp.exp((row + 1.0) * LG[h])
            dec_ref[L:2 * L, h * LANES:(h + 1) * LANES] = jnp.exp((L - 1.0 - row) * LG[h])

    xb_ref[...] = x_ref[0].astype(BF16)
    lb = _hgrn_lower_bounds(hlb_ref[...], 0)[layer]
    PIECE = 2 * LANES

    def project(c0):
        c1 = min(c0 + PIECE, NP_IN)
        res = _dot_nt(xb_ref[...], win_ref[c0:c1, :])
        if c0 < C_MV:
            pre_ref[SUBLANES:SUBLANES + TB, c0:c1] = res
        else:
            p_ref[:, c0:c1] = res

    def gate_math(r0, n):
        rows = slice(r0, r0 + n)
        lf, kh = _hgrn_gates(p_ref[rows, C_HF:C_HF + H_WIDTH], lb)
        p_ref[rows, C_HF:C_HF + H_WIDTH] = lf
        p_ref[rows, C_HK:C_HK + H_WIDTH] = kh
        g = p_ref[rows, C_IF:C_IF + LANES] + vec_ref[V_BIF:V_BIF + 1, 0:LANES]
        p_ref[rows, C_IF:C_IF + LANES] = jnp.where(_lane((n, LANES)) < N_HEADS, g, _log_sigmoid(g))

    def conv_math(half, r0, n):
        cols = slice(half * HP, (half + 1) * HP)
        lo = SUBLANES + r0
        def tap(i):
            return jnp.tile(vec_ref[V_CONV8 + i * SUBLANES:V_CONV8 + (i + 1) * SUBLANES, cols], (n // SUBLANES, 1))

        ext = pre_ref[lo - SUBLANES:lo + n, cols]
        acc = tap(CONV_W) + tap(CONV_W - 1) * ext[SUBLANES:, :]
        for t in range(1, CONV_W):
            shifted = pltpu.roll(ext, t, 0)[SUBLANES:, :]
            acc = acc + tap(CONV_W - 1 - t) * shifted
        scale = 1.0 if half == 0 else M_DH ** -0.5
        p_ref[r0:r0 + n, C_MQ + half * HP:C_MQ + (half + 1) * HP] = _silu(acc) * scale

    def rope_math(col, scale, r0, n):
        rows = slice(r0, r0 + n)
        cos = cos_ref[rows, :]
        sin = sin_ref[rows, :]
        for h in range(N_HEADS):
            sl = slice(col + h * LANES, col + (h + 1) * LANES)
            v = p_ref[rows, sl]
            p_ref[rows, sl] = (v * cos + pltpu.roll(v, LANES // 2, 1) * sin) * scale

    def value_math(r0, n):
        one96 = jnp.where(_lane((n, LANES)) == M_DH, 1.0, 0.0)
        for h in range(N_HEADS):
            sl = slice(C_MV + h * LANES, C_MV + (h + 1) * LANES)
            p_ref[r0:r0 + n, sl] = p_ref[r0:r0 + n, sl] + one96

    order = ([C_HF, C_HF + 4 * H_WIDTH] + list(range(C_MQ, C_MV, PIECE)) + list(range(C_RQ, C_RV, PIECE))
             + list(range(C_MV, C_MO, PIECE)) + list(range(C_HF + PIECE, C_HF + 4 * H_WIDTH, PIECE))
             + list(range(C_MO, C_RQ, PIECE)) + list(range(C_RV, C_HF, PIECE)))
    assert sorted(order) == list(range(0, NP_IN, PIECE))
    pos = {c0: i for i, c0 in enumerate(order)}
    vec = []
    for r0 in range(0, TB, 64):
        vec.append((pos[C_HF + 4 * H_WIDTH], functools.partial(gate_math, r0, 64)))
    for half in range(2):
        for r0 in range(0, TB, 32):
            vec.append((pos[C_MQ + half * HP + HP - PIECE], functools.partial(conv_math, half, r0, 32)))
    for col, scale in ((C_RQ, 1.0), (C_RK, R_DH ** -0.5)):
        for r0 in range(0, TB, 64):
            vec.append((pos[col + HP - PIECE], functools.partial(rope_math, col, scale, r0, 64)))
    for r0 in range(0, TB, 128):
        vec.append((pos[C_MV + HP - PIECE], functools.partial(value_math, r0, 128)))
    per_piece = -(-len(vec) // (len(order) - 6))
    vi = 0
    for i, c0 in enumerate(order):
        project(c0)
        done = 0
        while vi < len(vec) and vec[vi][0] < i and done < per_piece:
            vec[vi][1]()
            vi += 1
            done += 1
    while vi < len(vec):
        vec[vi][1]()
        vi += 1
    pre_ref[SUBLANES - (CONV_W - 1):SUBLANES, :] = pre_ref[TB + SUBLANES - (CONV_W - 1):TB + SUBLANES, :]

    tril = lax.broadcasted_iota(jnp.int32, (L, L), 0) >= lax.broadcasted_iota(jnp.int32, (L, L), 1)
    bd = (lax.broadcasted_iota(jnp.int32, (LANES, LANES), 0) < H_DK) == (_lane((LANES, LANES)) < H_DK)
    low = _lane((L, LANES)) < H_DK
    lane = _lane((L, LANES))

    def intra(c, carry):
        rows = pl.ds(pl.multiple_of(c * L, L), L)

        ga = p_ref[rows, C_IF:C_IF + LANES]
        cum = _cumsum_rows(ga)
        bc = _cumsum_rows(p_ref[rows, C_HF:C_HF + H_WIDTH])

        def head(col, h, dt=None):
            a = p_ref[rows, col + h * LANES:col + (h + 1) * LANES]
            return a if dt is None else a.astype(dt)

        qk_r = [_dot_nt(head(C_RQ, h, BF16), head(C_RK, h, BF16)) for h in range(N_HEADS)]
        qk_m = [_dot_nt(head(C_MQ, h, BF16), head(C_MK, h, BF16)) for h in range(N_HEADS)]

        qh = p_ref[rows, C_HQ:C_HQ + H_WIDTH]
        khh = p_ref[rows, C_HK:C_HK + H_WIDTH]
        head_sel = [jnp.where((_lane((1, H_WIDTH)) & H_DK) == hh * H_DK, 1.0, 0.0).astype(BF16) for hh in range(2)]
        for n in range(NL + 1):
            if n < NL:
                fac = jnp.exp(-jnp.abs(bc - _mid_row_broadcast(bc, L >> (n + 1))))
                qn, kn = (qh * fac).astype(BF16), (khh * fac).astype(BF16)
            else:
                qn, kn = qh.astype(BF16), khh.astype(BF16)
            qt_ref[n, 0] = qn * head_sel[0]
            qt_ref[n, 1] = qn * head_sel[1]
            kt_ref[n] = kn
        e_pre = jnp.exp(bc)
        qi_ref[rows, :] = (qh * e_pre).astype(BF16)
        dh_ref[pl.ds(c, 1), :] = e_pre[L - 1:L, :]
        k_suf = khh * jnp.exp(bc[L - 1:L, :] - bc)

        for h in range(N_HEADS):
            v = head(C_RV, h, BF16)
            s = (qk_r[h] * dm_ref[h]).astype(BF16)
            ho_ref[rows, HP + h * LANES:HP + (h + 1) * LANES] = _dot(s, v)
            kd_t = (head(C_RK, h) * dec_ref[L:2 * L, h * LANES:(h + 1) * LANES]).T.astype(BF16)
            ur_ref[c, h] = _dot(kd_t, v)

        lv = lv_ref[...]
        heads = [(p, hh) for p in range(N_HEADS // 2) for hh in range(2)]

        def level_scores(p, hh):
            psl = slice(p * LANES, (p + 1) * LANES)
            return [_dot_nt(qt_ref[n, hh, :, psl], kt_ref[n, :, psl]) for n in range(NL + 1)]

        def fold(scores):
            s_mat = jnp.zeros((L, L), F32)
            for n, sc in enumerate(scores):
                s_mat = jnp.where(lv == n, sc, s_mat)
            return s_mat.astype(BF16)

        pending = level_scores(*heads[0])

        cum_ref[rows, :] = cum
        a_all = ga - pltpu.roll(cum, LANES - N_HEADS, 1)
        a_t = a_all.T
        ml = jnp.zeros((L, LANES), F32)
        for h in range(N_HEADS):
            b_col = cum[:, N_HEADS + h:N_HEADS + h + 1]
            log_d = jnp.where(tril, b_col + a_t[h:h + 1, :], -jnp.inf)
            m_loc = jnp.max(log_d, axis=1, keepdims=True)
            pm_ref[h] = (qk_m[h] * jnp.exp(log_d - m_loc)).astype(BF16)
            ml = jnp.where(lane == h, m_loc, ml)
            w_loc = jnp.exp(b_col[L - 1:L, :] + a_all[:, h:h + 1] - m_loc[L - 1:L, :])
            kw_ref[h] = (head(C_MK, h) * w_loc).T.astype(BF16)
        ml_ref[rows, :] = ml

        s_bf = []
        for i in range(len(heads)):
            nxt = level_scores(*heads[i + 1]) if i + 1 < len(heads) else None
            s_bf.append(fold(pending))
            pending = nxt
            if i == 0:
                for h in range(N_HEADS):
                    v = head(C_MV, h, BF16)
                    ho_ref[rows, h * LANES:(h + 1) * LANES] = _dot(pm_ref[h], v)
                    u_ref[c, h] = _dot(kw_ref[h], v)
        for p in range(N_HEADS // 2):
            psl = slice(p * LANES, (p + 1) * LANES)
            vp = p_ref[rows, C_HI + p * LANES:C_HI + (p + 1) * LANES]
            ho_ref[rows, 2 * HP + p * LANES:2 * HP + (p + 1) * LANES] = (
                _dot(s_bf[2 * p], jnp.where(low, vp, 0.0).astype(BF16))
                + _dot(s_bf[2 * p + 1], jnp.where(low, 0.0, vp).astype(BF16)))
            uh_ref[c, p] = jnp.where(bd, _dot(k_suf[:, psl].T.astype(BF16), vp.astype(BF16)), 0.0)
        return carry

    lax.fori_loop(0, n_chunks, intra, 0, unroll=2)

    for c in range(n_chunks):
        rows = slice(c * L, (c + 1) * L)

        ml = ml_ref[rows, :]
        b_all = pltpu.roll(cum_ref[rows, :], LANES - N_HEADS, 1)
        m_prev = m_ref[0:1, :]
        m_t = jnp.maximum(ml, b_all + m_prev)
        r_all = jnp.exp(ml - m_t)
        w_all = jnp.exp(b_all + m_prev - m_t)
        e_all = jnp.exp(-m_t)
        for h in range(N_HEADS):
            sl = slice(h * LANES, (h + 1) * LANES)
            gr_ref[rows, sl] = jnp.broadcast_to(r_all[:, h:h + 1], (L, LANES))
            gw_ref[rows, sl] = jnp.broadcast_to(w_all[:, h:h + 1], (L, LANES))
            ge_ref[rows, sl] = jnp.broadcast_to(e_all[:, h:h + 1], (L, LANES))
        m_new = m_t[L - 1:L, :]
        dec_row = jnp.exp(b_all[L - 1:L, :] + m_prev - m_new)
        g_row = jnp.exp(ml[L - 1:L, :] - m_new)
        m_ref[0:1, :] = m_new

        for h in range(N_HEADS):
            q = p_ref[rows, C_MQ + h * LANES:C_MQ + (h + 1) * LANES].astype(BF16)
            it_ref[rows, h * LANES:(h + 1) * LANES] = _dot(q, c_ref[h].astype(BF16))
        for h in range(N_HEADS):
            q = p_ref[rows, C_RQ + h * LANES:C_RQ + (h + 1) * LANES]
            qd = (q * dec_ref[0:L, h * LANES:(h + 1) * LANES]).astype(BF16)
            it_ref[rows, HP + h * LANES:HP + (h + 1) * LANES] = _dot(qd, r_ref[h].astype(BF16))
        for p in range(N_HEADS // 2):
            psl = slice(p * LANES, (p + 1) * LANES)
            it_ref[rows, 2 * HP + p * LANES:2 * HP + (p + 1) * LANES] = _dot(qi_ref[rows, psl], s_ref[p].astype(BF16))

        d_col = dh_ref[c:c + 1, :].T
        for h in range(N_HEADS):
            c_ref[h] = dec_row[:, h:h + 1] * c_ref[h] + g_row[:, h:h + 1] * u_ref[c, h]
            r_ref[h] = math.exp(L * LG[h]) * r_ref[h] + ur_ref[c, h]
        for p in range(N_HEADS // 2):
            s_ref[p] = d_col[p * LANES:(p + 1) * LANES, :] * s_ref[p] + uh_ref[c, p]

    def mix_rows(rows, T):
        valid = _lane((T, LANES)) < M_DH
        low_t = _lane((T, LANES)) < H_DK
        parts = []
        for h in range(N_HEADS):
            sl = slice(h * LANES, (h + 1) * LANES)
            num = gr_ref[rows, sl] * ho_ref[rows, sl] + gw_ref[rows, sl] * it_ref[rows, sl]
            den = num[:, M_DH:M_DH + 1]
            hm = num / jnp.maximum(jnp.abs(den), ge_ref[rows, sl])
            o = p_ref[rows, C_MO + h * LANES:C_MO + (h + 1) * LANES]
            z = p_ref[rows, C_MZ + h * LANES:C_MZ + (h + 1) * LANES]
            gain = vec_ref[V_MNW:V_MNW + 1, sl]
            parts.append((_head_layer_norm_padded(hm, valid) * gain * _sigmoid(o) * _silu(z)).astype(BF16))
        for h in range(N_HEADS):
            sl = slice(HP + h * LANES, HP + (h + 1) * LANES)
            hn = _head_layer_norm_padded(ho_ref[rows, sl] + it_ref[rows, sl], valid)
            g = p_ref[rows, C_RG + h * LANES:C_RG + (h + 1) * LANES]
            parts.append((hn * vec_ref[V_RNW:V_RNW + 1, h * LANES:(h + 1) * LANES] * _silu(g)).astype(BF16))
        for p in range(N_HEADS // 2):
            sl = slice(2 * HP + p * LANES, 2 * HP + (p + 1) * LANES)
            hn = _pair_rms_norm(ho_ref[rows, sl] + it_ref[rows, sl], low_t)
            g = p_ref[rows, C_HG + p * LANES:C_HG + (p + 1) * LANES]
            parts.append((hn * vec_ref[V_HNW:V_HNW + 1, p * LANES:(p + 1) * LANES] * _silu(g)).astype(BF16))
        return jnp.concatenate(parts, axis=-1)

    def out_rows(rows, proj):
        y_ref[0, rows, :] = _layer_norm_rows(ALPHA * x_ref[0, rows, :] + proj,
                                             vec_ref[V_LNG:V_LNG + 1, :], vec_ref[V_LNB:V_LNB + 1, :])

    half = TB // 2
    rows0, rows1 = slice(0, half), slice(half, TB)
    proj0 = _dot(mix_rows(rows0, half), wout_ref[...])
    mix1 = mix_rows(rows1, half)
    out_rows(rows0, proj0)
    out_rows(rows1, _dot(mix1, wout_ref[...]))

    @pl.when(j == n_tb - 1)
    def _final_states():
        convout_ref[0] = pre_ref[SUBLANES - (CONV_W - 1):SUBLANES, :]
        for h in range(N_HEADS):
            c_aug = c_ref[h]
            cout_ref[0, h] = c_aug[0:M_DH, 0:M_DH]
            nout_ref[0, h:h + 1, :] = c_aug.T[M_DH:M_DH + 1, 0:M_DH]
            r_full = r_ref[h]
            rout_ref[0, h] = jnp.concatenate(
                [r_full[0:ROPE_HALF, 0:R_DH], r_full[LANES // 2:LANES // 2 + ROPE_HALF, 0:R_DH]], axis=0)
            p, hh = divmod(h, 2)
            sout_ref[0, h] = s_ref[p][hh * H_DK:(hh + 1) * H_DK, hh * H_DK:(hh + 1) * H_DK]
        mout_ref[0] = m_ref[0:1, 0:N_HEADS]


def _full_spec(shape):
    nd = len(shape)
    return pl.BlockSpec(shape, lambda *_: (0,) * nd)


def _prefill_layer(layer, x, win, wout, cos, sin, vec, hlb, lv, TB):
    B, T, _ = x.shape
    n_tb = T // TB
    n_chunks = TB // CHUNK
    kern = functools.partial(_prefill_kernel, layer, TB)

    def layer_spec(a, single_buffer=False):
        kw = dict(pipeline_mode=pl.Buffered(1)) if single_buffer else {}
        return pl.BlockSpec((None,) + a.shape[1:], lambda b, j: (layer,) + (0,) * (a.ndim - 1), **kw)

    in_specs = [
        pl.BlockSpec((1, TB, D_MODEL), lambda b, j: (b, j, 0)),
        layer_spec(win, True), layer_spec(wout, True),
        pl.BlockSpec((TB, LANES), lambda b, j: (j, 0)),
        pl.BlockSpec((TB, LANES), lambda b, j: (j, 0)),
        layer_spec(vec), _full_spec(hlb.shape), _full_spec(lv.shape),
    ]
    out_shape = (
        jax.ShapeDtypeStruct((B, T, D_MODEL), F32),
        jax.ShapeDtypeStruct((B, N_HEADS, M_DH, M_DH), F32),
        jax.ShapeDtypeStruct((B, N_HEADS, M_DH), F32),
        jax.ShapeDtypeStruct((B, 1, N_HEADS), F32),
        jax.ShapeDtypeStruct((B, CONV_W - 1, 2 * HP), F32),
        jax.ShapeDtypeStruct((B, N_HEADS, R_DH, R_DH), F32),
        jax.ShapeDtypeStruct((B, N_HEADS, H_DK, H_DK), F32),
    )
    out_specs = (
        pl.BlockSpec((1, TB, D_MODEL), lambda b, j: (b, j, 0)),
        pl.BlockSpec((1, N_HEADS, M_DH, M_DH), lambda b, j: (b, 0, 0, 0)),
        pl.BlockSpec((1, N_HEADS, M_DH), lambda b, j: (b, 0, 0)),
        pl.BlockSpec((1, 1, N_HEADS), lambda b, j: (b, 0, 0)),
        pl.BlockSpec((1, CONV_W - 1, 2 * HP), lambda b, j: (b, 0, 0)),
        pl.BlockSpec((1, N_HEADS, R_DH, R_DH), lambda b, j: (b, 0, 0, 0)),
        pl.BlockSpec((1, N_HEADS, H_DK, H_DK), lambda b, j: (b, 0, 0, 0)),
    )
    L = CHUNK
    scratch = [
        pltpu.VMEM((TB, NP_SCR), F32),
        pltpu.VMEM((TB + SUBLANES, 2 * HP), F32),
        pltpu.VMEM((TB, MIXP), F32),
        pltpu.VMEM((TB, LANES), F32),
        pltpu.VMEM((TB, LANES), F32),
        pltpu.VMEM((N_HEADS, L, L), BF16),
        pltpu.VMEM((N_HEADS, LANES, L), BF16),
        pltpu.VMEM((N_LEVELS + 1, 2, L, H_WIDTH), BF16),
        pltpu.VMEM((N_LEVELS + 1, L, H_WIDTH), BF16),
        pltpu.VMEM((TB, H_WIDTH), BF16),
        pltpu.VMEM((max(n_chunks, SUBLANES), H_WIDTH), F32),
        pltpu.VMEM((n_chunks, N_HEADS, LANES, LANES), F32),
        pltpu.VMEM((n_chunks, N_HEADS, LANES, LANES), F32),
        pltpu.VMEM((n_chunks, N_HEADS // 2, LANES, LANES), F32),
        pltpu.VMEM((N_HEADS, LANES, LANES), F32),
        pltpu.VMEM((N_HEADS, LANES, LANES), F32),
        pltpu.VMEM((N_HEADS // 2, LANES, LANES), F32),
        pltpu.VMEM((SUBLANES, LANES), F32),
        pltpu.VMEM((N_HEADS, L, L), F32),
        pltpu.VMEM((2 * L, HP), F32),
        pltpu.VMEM((TB, D_MODEL), BF16),
        pltpu.VMEM((TB, MIXP), F32),
        pltpu.VMEM((TB, HP), F32),
        pltpu.VMEM((TB, HP), F32),
        pltpu.VMEM((TB, HP), F32),
    ]
    return pl.pallas_call(
        kern, grid=(B, n_tb), in_specs=in_specs, out_specs=out_specs, out_shape=out_shape,
        scratch_shapes=scratch,
        compiler_params=pltpu.CompilerParams(dimension_semantics=("arbitrary", "arbitrary"),
                                             vmem_limit_bytes=VMEM_LIMIT),
        name=f"prefill_layer{layer}",
    )(x, win, wout, cos, sin, vec, hlb, lv)


DEC_KB = 4
KR_M = M_DH // DEC_KB
KR_H = H_DK // DEC_KB
G_DEC, G_WS, G_EN, G_GAM, G_SM, G_QN, G_SR, G_SH = (i * N_HEADS for i in range(8))


def _stream_state(q_blk, dec, kw_blk, v, in_ref, out_ref, nrows):
    acc = jnp.zeros_like(v)
    for kk in range(nrows):
        st = in_ref[0, 0, kk]
        d = dec if dec.shape[0] == 1 else dec[kk:kk + 1, :]
        acc = acc + q_blk[kk:kk + 1, :] * st
        out_ref[0, 0, kk] = d * st + kw_blk[kk:kk + 1, :] * v
    return acc


def _decode_kernel(x_ref, wt_ref, wout_ref, cos_ref, sin_ref, dvec_ref, dcol_ref, hlbt_ref, vec_ref,
                   cin_ref, nin_ref, min_ref, convin_ref, rin_ref, sin_st_ref,
                   y_ref, cout_ref, nout_ref, mout_ref, convout_ref, rout_ref, sout_ref,
                   p_ref, xs_ref, ht_ref, mix_ref, g_ref):
    l = pl.program_id(0)
    h = pl.program_id(1)
    kb = pl.program_id(2)
    first = jnp.logical_and(h == 0, kb == 0)
    last = jnp.logical_and(h == N_HEADS - 1, kb == DEC_KB - 1)
    B = x_ref.shape[0]

    @pl.when(jnp.logical_and(first, l == 0))
    def _load_x():
        xs_ref[...] = x_ref[...]

    @pl.when(first)
    def _project():
        p_ref[0:N_IN, :] = _dot_nt(wt_ref[0], xs_ref[...].astype(BF16))
        ht_ref[...] = jnp.zeros_like(ht_ref)

        pre = p_ref[O_MQ:O_MQ + 2 * M_WIDTH, :].T
        cw = dvec_ref[0, 0:CONV_W, :]
        acc = dvec_ref[0, CONV_W:CONV_W + 1, :] + cw[CONV_W - 1:CONV_W, :] * pre
        for jj in range(CONV_W - 1):
            acc = acc + cw[jj:jj + 1, :] * convin_ref[0, jj]
        for jj in range(CONV_W - 2):
            convout_ref[0, jj] = convin_ref[0, jj + 1]
        convout_ref[0, CONV_W - 2] = pre
        p_ref[O_MQ:O_MQ + 2 * M_WIDTH, :] = _silu(acc).T

        gt = p_ref[O_IF:O_IF + 2 * N_HEADS, :] + dcol_ref[0, D_MIX:D_MIX + 2 * N_HEADS, :]
        ig = gt[0:N_HEADS]
        lf = _log_sigmoid(gt[N_HEADS:2 * N_HEADS])
        m_prev = min_ref[0]
        m_t = jnp.maximum(ig, lf + m_prev)
        dec = jnp.exp(lf + m_prev - m_t)
        w_s = jnp.exp(ig - m_t)
        mout_ref[0] = m_t
        g_ref[G_DEC:G_DEC + N_HEADS, :] = dec
        g_ref[G_WS:G_WS + N_HEADS, :] = w_s
        g_ref[G_EN:G_EN + N_HEADS, :] = jnp.exp(-m_t)
        for hh in range(N_HEADS):
            q = p_ref[O_MQ + hh * M_DH:O_MQ + (hh + 1) * M_DH, :]
            k = p_ref[O_MK + hh * M_DH:O_MK + (hh + 1) * M_DH, :] * (M_DH ** -0.5)
            n_old = nin_ref[0, hh]
            d_h = dec[hh:hh + 1]
            kw = k * w_s[hh:hh + 1]
            g_ref[G_SM + hh:G_SM + hh + 1, :] = jnp.sum(q * k, axis=0, keepdims=True) * w_s[hh:hh + 1]
            g_ref[G_QN + hh:G_QN + hh + 1, :] = jnp.sum(q * n_old, axis=0, keepdims=True)
            nout_ref[0, hh] = d_h * n_old + kw
            p_ref[O_MK + hh * M_DH:O_MK + (hh + 1) * M_DH, :] = kw

        cos = cos_ref[...]
        sin = sin_ref[...]
        for hh in range(N_HEADS):
            g_ref[G_GAM + hh:G_GAM + hh + 1, :] = jnp.full((1, B), math.exp(LG[hh]), F32)
            rot = []
            for off, scale in ((O_RQ, 1.0), (O_RK, R_DH ** -0.5)):
                x1 = p_ref[off + hh * R_DH:off + hh * R_DH + ROPE_HALF, :]
                x2 = p_ref[off + hh * R_DH + ROPE_HALF:off + (hh + 1) * R_DH, :]
                r1 = (x1 * cos - x2 * sin) * scale
                r2 = (x1 * sin + x2 * cos) * scale
                p_ref[off + hh * R_DH:off + hh * R_DH + ROPE_HALF, :] = r1
                p_ref[off + hh * R_DH + ROPE_HALF:off + (hh + 1) * R_DH, :] = r2
                rot.append((r1, r2))
            (q1, q2), (k1, k2) = rot
            g_ref[G_SR + hh:G_SR + hh + 1, :] = (jnp.sum(q1 * k1, axis=0, keepdims=True)
                                                 + jnp.sum(q2 * k2, axis=0, keepdims=True))

        lbs = _hgrn_lower_bounds(hlbt_ref[...], 1)
        lb = lbs[0]
        for i in range(1, DEPTH):
            lb = jnp.where(l == i, lbs[i], lb)
        lfh, kh = _hgrn_gates(p_ref[O_HF:O_HF + H_WIDTH, :], lb)
        eh = jnp.exp(lfh)
        qh = p_ref[O_HQ:O_HQ + H_WIDTH, :]
        p_ref[O_HF:O_HF + H_WIDTH, :] = eh
        p_ref[O_HK:O_HK + H_WIDTH, :] = kh
        p_ref[O_HQ:O_HQ + H_WIDTH, :] = qh * eh
        qk = qh * kh
        for hh in range(N_HEADS):
            g_ref[G_SH + hh:G_SH + hh + 1, :] = jnp.sum(qk[hh * H_DK:(hh + 1) * H_DK], axis=0, keepdims=True)

    def rows(base, width, n):
        return pl.ds(pl.multiple_of(base + h * width + kb * n, SUBLANES), n)

    dec = g_ref[pl.ds(G_DEC + h, 1), :]
    acc = _stream_state(p_ref[rows(O_MQ, M_DH, KR_M), :], dec, p_ref[rows(O_MK, M_DH, KR_M), :],
                        p_ref[pl.ds(pl.multiple_of(O_MV + h * M_DH, SUBLANES), M_DH), :], cin_ref, cout_ref, KR_M)
    ht_ref[pl.ds(pl.multiple_of(h * M_DH, SUBLANES), M_DH), :] += acc

    gam = g_ref[pl.ds(G_GAM + h, 1), :]
    acc = _stream_state(p_ref[rows(O_RQ, R_DH, KR_M), :], gam, p_ref[rows(O_RK, R_DH, KR_M), :],
                        p_ref[pl.ds(pl.multiple_of(O_RV + h * R_DH, SUBLANES), R_DH), :], rin_ref, rout_ref, KR_M)
    ht_ref[pl.ds(pl.multiple_of(M_WIDTH + h * R_DH, SUBLANES), R_DH), :] += acc

    acc = _stream_state(p_ref[rows(O_HQ, H_DK, KR_H), :], p_ref[rows(O_HF, H_DK, KR_H), :],
                        p_ref[rows(O_HK, H_DK, KR_H), :],
                        p_ref[pl.ds(pl.multiple_of(O_HI + h * H_DK, SUBLANES), H_DK), :], sin_st_ref, sout_ref, KR_H)
    ht_ref[pl.ds(pl.multiple_of(M_WIDTH + R_WIDTH + h * H_DK, SUBLANES), H_DK), :] += acc

    @pl.when(last)
    def _finish():
        def head_ln(a):
            mu = jnp.mean(a, axis=0, keepdims=True)
            d = a - mu
            return d * lax.rsqrt(jnp.mean(d * d, axis=0, keepdims=True) + HEAD_EPS)

        for hh in range(N_HEADS):
            sl = slice(hh * M_DH, (hh + 1) * M_DH)
            v = p_ref[O_MV + hh * M_DH:O_MV + (hh + 1) * M_DH, :]
            d_h = g_ref[G_DEC + hh:G_DEC + hh + 1, :]
            s = g_ref[G_SM + hh:G_SM + hh + 1, :]
            num = s * v + ht_ref[sl, :] * d_h
            den = s + g_ref[G_QN + hh:G_QN + hh + 1, :] * d_h
            hm = num / jnp.maximum(jnp.abs(den), g_ref[G_EN + hh:G_EN + hh + 1, :])
            o = p_ref[O_MO + hh * M_DH:O_MO + (hh + 1) * M_DH, :]
            z = p_ref[O_MZ + hh * M_DH:O_MZ + (hh + 1) * M_DH, :]
            mix_ref[sl, :] = head_ln(hm) * dcol_ref[0, sl, :] * _sigmoid(o) * _silu(z)
        for hh in range(N_HEADS):
            sl = slice(M_WIDTH + hh * R_DH, M_WIDTH + (hh + 1) * R_DH)
            v = p_ref[O_RV + hh * R_DH:O_RV + (hh + 1) * R_DH, :]
            hr = g_ref[G_SR + hh:G_SR + hh + 1, :] * v + ht_ref[sl, :] * math.exp(LG[hh])
            gg = p_ref[O_RG + hh * R_DH:O_RG + (hh + 1) * R_DH, :]
            mix_ref[sl, :] = head_ln(hr) * dcol_ref[0, sl, :] * _silu(gg)
        for hh in range(N_HEADS):
            sl = slice(M_WIDTH + R_WIDTH + hh * H_DK, M_WIDTH + R_WIDTH + (hh + 1) * H_DK)
            v = p_ref[O_HI + hh * H_DK:O_HI + (hh + 1) * H_DK, :]
            ho = g_ref[G_SH + hh:G_SH + hh + 1, :] * v + ht_ref[sl, :]
            hn = ho * lax.rsqrt(jnp.mean(ho * ho, axis=0, keepdims=True) + HEAD_EPS)
            gg = p_ref[O_HG + hh * H_DK:O_HG + (hh + 1) * H_DK, :]
            mix_ref[sl, :] = hn * dcol_ref[0, sl, :] * _silu(gg)
        x = xs_ref[...]
        out = _dot(mix_ref[...].T.astype(BF16), wout_ref[0])
        y = _layer_norm_rows(ALPHA * x + out, vec_ref[0, V_LNG:V_LNG + 1, :], vec_ref[0, V_LNB:V_LNB + 1, :])
        xs_ref[...] = y
        y_ref[0] = y


def _decode(x, wt, wout, cos, sin, dvec, dcol, hlbt, vec, st_c, st_n, st_m, st_conv, st_r, st_s):
    B = x.shape[0]

    def lspec(a):
        nd = a.ndim
        return pl.BlockSpec((1,) + a.shape[1:], lambda l, h, kb: (l,) + (0,) * (nd - 1))

    def kv_spec(a, n):
        return pl.BlockSpec((1, 1, n) + a.shape[3:], lambda l, h, kb: (l, h, kb, 0, 0))

    in_specs = [
        _full_spec(x.shape), lspec(wt), lspec(wout), _full_spec(cos.shape), _full_spec(sin.shape),
        lspec(dvec), lspec(dcol), _full_spec(hlbt.shape), lspec(vec),
        kv_spec(st_c, KR_M), lspec(st_n), lspec(st_m), lspec(st_conv), kv_spec(st_r, KR_M), kv_spec(st_s, KR_H),
    ]
    out_shape = (
        jax.ShapeDtypeStruct((DEPTH, B, D_MODEL), F32),
        jax.ShapeDtypeStruct(st_c.shape, F32), jax.ShapeDtypeStruct(st_n.shape, F32),
        jax.ShapeDtypeStruct(st_m.shape, F32), jax.ShapeDtypeStruct(st_conv.shape, F32),
        jax.ShapeDtypeStruct(st_r.shape, F32), jax.ShapeDtypeStruct(st_s.shape, F32),
    )
    out_specs = (
        pl.BlockSpec((1, B, D_MODEL), lambda l, h, kb: (l, 0, 0)),
        kv_spec(st_c, KR_M), lspec(st_n), lspec(st_m), lspec(st_conv), kv_spec(st_r, KR_M), kv_spec(st_s, KR_H),
    )
    scratch = [
        pltpu.VMEM((N_IN + H_WIDTH, B), F32),
        pltpu.VMEM((B, D_MODEL), F32),
        pltpu.VMEM((D_MIX, B), F32),
        pltpu.VMEM((D_MIX, B), F32),
        pltpu.VMEM((8 * N_HEADS, B), F32),
    ]
    return pl.pallas_call(
        _decode_kernel, grid=(DEPTH, N_HEADS, DEC_KB), in_specs=in_specs, out_specs=out_specs,
        out_shape=out_shape, scratch_shapes=scratch,
        compiler_params=pltpu.CompilerParams(dimension_semantics=("arbitrary", "arbitrary", "arbitrary"),
                                             vmem_limit_bytes=VMEM_LIMIT),
        name="decode_step",
    )(x, wt, wout, cos, sin, dvec, dcol, hlbt, vec, st_c, st_n, st_m, st_conv, st_r, st_s)


def _rope_angles(pos):
    inv = ROPE_BASE ** (-jnp.arange(ROPE_HALF, dtype=F32) / ROPE_HALF)
    ang = pos.astype(F32)[:, None] * inv[None, :]
    return jnp.cos(ang), jnp.sin(ang)


PREFILL_BLOCK = 512


def kernel(x_prompt, x_sample, state_mlstm_C, state_mlstm_n, state_mlstm_m, state_mlstm_conv, state_ret, state_hgrn, w_in, conv_w, conv_b, b_mgate, m_norm_w, r_norm_w, h_norm_w, hgrn_lb, w_out, ln_g, ln_b):
    B, T, _ = x_prompt.shape

    wt = jnp.swapaxes(w_in, 1, 2).astype(BF16)
    win_p = jnp.concatenate(
        [_pad_heads(wt[:, O_MQ:O_IF], 1),
         _pad_rope_heads(wt[:, O_RQ:O_RV], 1), _pad_heads(wt[:, O_RV:O_HF], 1),
         wt[:, O_HF:N_IN],
         jnp.pad(wt[:, O_IF:O_RQ], ((0, 0), (0, LANES - 2 * N_HEADS), (0, 0)))], axis=1)
    wout_b = w_out.astype(BF16)
    wout_p = jnp.concatenate([_pad_heads(wout_b[:, 0:M_WIDTH + R_WIDTH], 1), wout_b[:, M_WIDTH + R_WIDTH:]],
                             axis=1)

    def row(a):
        return jnp.pad(a, ((0, 0), (0, D_MODEL - a.shape[-1])))[:, None, :]

    vec = jnp.concatenate([_pad_heads(conv_w, 2), row(_pad_heads(conv_b, 1)), row(b_mgate),
                           row(_pad_heads(m_norm_w, 1)), row(_pad_heads(r_norm_w, 1)), row(h_norm_w),
                           row(ln_g), row(ln_b)], axis=1)
    vec = jnp.pad(vec, ((0, 0), (0, V_ROWS - vec.shape[1]), (0, 0)))
    vec = jnp.concatenate([vec, jnp.repeat(vec[:, V_CONVW:V_CONVB + 1], SUBLANES, axis=1)], axis=1)
    dvec = jnp.pad(jnp.concatenate([conv_w, conv_b[:, None, :]], axis=1),
                   ((0, 0), (0, SUBLANES - CONV_W - 1), (0, 0)))
    dcol = jnp.concatenate([m_norm_w, r_norm_w, h_norm_w, b_mgate], axis=1)[:, :, None]
    hlb = hgrn_lb.astype(F32)

    lv = jnp.asarray(_level_table(CHUNK))

    c, s = _rope_angles(jnp.arange(T, dtype=jnp.int32))
    z = jnp.zeros((T, LANES // 2 - ROPE_HALF), F32)
    cos_p = jnp.concatenate([c, z, c, z], axis=1)
    sin_p = jnp.concatenate([-s, z, s, z], axis=1)

    hp = x_prompt
    st = [[] for _ in range(6)]
    for l in range(DEPTH):
        outs = _prefill_layer(l, hp, win_p, wout_p, cos_p, sin_p, vec, hlb, lv, PREFILL_BLOCK)
        hp = outs[0]
        for k in range(6):
            st[k].append(outs[1 + k])
    mC_p = jnp.stack(st[0])
    mn_p = jnp.stack(st[1])
    mm_p = jnp.stack(st[2])[:, :, 0, :]
    conv_p = _unpad_heads(jnp.stack(st[3]))
    ret_p = jnp.stack(st[4])
    hgrn_p = jnp.stack(st[5])

    n_s = x_sample.shape[0]
    cs, ss = _rope_angles(PAST_LEN + jnp.arange(x_sample.shape[1], dtype=jnp.int32))
    cos_s = jnp.broadcast_to(cs[0][:, None], (ROPE_HALF, n_s))
    sin_s = jnp.broadcast_to(ss[0][:, None], (ROPE_HALF, n_s))
    ys, c_t, n_t, m_t, conv_t, r_t, s_t = _decode(
        x_sample[:, 0, :], wt, wout_b, cos_s, sin_s, dvec, dcol, hlb.T, vec,
        jnp.transpose(state_mlstm_C, (0, 2, 3, 4, 1)), jnp.transpose(state_mlstm_n, (0, 2, 3, 1)),
        jnp.transpose(state_mlstm_m, (0, 2, 1)), jnp.transpose(state_mlstm_conv, (0, 2, 1, 3)),
        jnp.transpose(state_ret, (0, 2, 3, 4, 1)), jnp.transpose(state_hgrn, (0, 2, 3, 4, 1)))
    hs = ys[DEPTH - 1][:, None, :]
    mC_s = jnp.transpose(c_t, (0, 4, 1, 2, 3))
    mn_s = jnp.transpose(n_t, (0, 3, 1, 2))
    mm_s = jnp.transpose(m_t, (0, 2, 1))
    conv_s = jnp.transpose(conv_t, (0, 2, 1, 3))
    ret_s = jnp.transpose(r_t, (0, 4, 1, 2, 3))
    hgrn_s = jnp.transpose(s_t, (0, 4, 1, 2, 3))

    return (hp, hs, mC_p, mn_p, mm_p, conv_p, ret_p, hgrn_p, mC_s, mn_s, mm_s, conv_s, ret_s, hgrn_s)
```

```python
import functools
import math

import numpy as np
import jax
import jax.numpy as jnp
from jax import lax
from jax.experimental import pallas as pl
from jax.experimental.pallas import tpu as pltpu

F32 = jnp.float32
BF16 = jnp.bfloat16

D_MODEL = 1024
DEPTH = 2
PAST_LEN = 16384
N_HEADS = 4
M_DH = 96
R_DH = 96
H_DK = 64
M_WIDTH = N_HEADS * M_DH
R_WIDTH = N_HEADS * R_DH
H_WIDTH = N_HEADS * H_DK
D_MIX = M_WIDTH + R_WIDTH + H_WIDTH
CONV_W = 4
ROPE_BASE = 10000.0
LN_EPS = 1e-5
HEAD_EPS = 1e-6
ALPHA = (2 * DEPTH) ** 0.25

LANES = 128
SUBLANES = 8
HP = N_HEADS * LANES
ROPE_HALF = R_DH // 2
VMEM_LIMIT = 56 * 1024 * 1024

O_MQ, O_MK, O_MV, O_MO, O_MZ = 0, M_WIDTH, 2 * M_WIDTH, 3 * M_WIDTH, 4 * M_WIDTH
O_IF = 5 * M_WIDTH
O_RQ = O_IF + 2 * N_HEADS
O_RK, O_RV, O_RG = O_RQ + R_WIDTH, O_RQ + 2 * R_WIDTH, O_RQ + 3 * R_WIDTH
O_HF = O_RQ + 4 * R_WIDTH
O_HI, O_HQ, O_HG = O_HF + H_WIDTH, O_HF + 2 * H_WIDTH, O_HF + 3 * H_WIDTH
N_IN = O_HF + 4 * H_WIDTH
O_HK = N_IN

C_MQ, C_MK, C_MV, C_MO, C_MZ = 0, HP, 2 * HP, 3 * HP, 4 * HP
C_RQ, C_RK, C_RV, C_RG = 5 * HP, 6 * HP, 7 * HP, 8 * HP
C_HF = 9 * HP
C_HI = C_HF + H_WIDTH
C_HQ = C_HI + H_WIDTH
C_HG = C_HQ + H_WIDTH
C_IF = C_HG + H_WIDTH
NP_IN = C_IF + LANES
C_HK = NP_IN
NP_SCR = NP_IN + H_WIDTH
MIXP = 2 * HP + H_WIDTH

V_CONVW, V_CONVB, V_BIF, V_MNW, V_RNW, V_HNW, V_LNG, V_LNB = 0, CONV_W, CONV_W + 1, CONV_W + 2, CONV_W + 3, CONV_W + 4, CONV_W + 5, CONV_W + 6
V_ROWS = 2 * SUBLANES
V_CONV8 = V_ROWS
V_ROWS8 = V_ROWS + (CONV_W + 1) * SUBLANES

CHUNK = 128
N_LEVELS = int(math.log2(CHUNK))
LG = [math.log1p(-2.0 ** (-5.0 - h)) for h in range(N_HEADS)]


def _pad_groups(a, axis, group, padded):
    shp = a.shape
    n = shp[axis] // group
    a = a.reshape(shp[:axis] + (n, group) + shp[axis + 1:])
    pad = [(0, 0)] * a.ndim
    pad[axis + 1] = (0, padded - group)
    return jnp.pad(a, pad).reshape(shp[:axis] + (n * padded,) + shp[axis + 1:])


def _pad_heads(a, axis):
    return _pad_groups(a, axis, M_DH, LANES)


def _pad_rope_heads(a, axis):
    return _pad_groups(a, axis, ROPE_HALF, LANES // 2)


def _unpad_heads(a, dh=M_DH):
    shp = a.shape
    return a.reshape(shp[:-1] + (shp[-1] // LANES, LANES))[..., :dh].reshape(shp[:-1] + (shp[-1] // LANES * dh,))


def _level_table(L):
    nl = int(math.log2(L))
    tt = np.arange(L)[:, None]
    ss = np.arange(L)[None, :]
    lv = np.full((L, L), -1, np.int32)
    for n in range(nl):
        w = L >> (n + 1)
        same = (tt // (2 * w)) == (ss // (2 * w))
        cond = same & ((tt % (2 * w)) >= w) & ((ss % (2 * w)) < w)
        lv = np.where(cond, n, lv)
    lv = np.where(tt == ss, nl, lv)
    return lv.astype(np.int32)


def _dot(a, b):
    return jnp.dot(a, b, preferred_element_type=F32)


def _dot_nt(a, b):
    return lax.dot_general(a, b, (((1,), (1,)), ((), ())), preferred_element_type=F32)


def _sigmoid(x):
    return 1.0 / (1.0 + jnp.exp(-x))


def _silu(x):
    return x * _sigmoid(x)


def _log_sigmoid(x):
    return jnp.minimum(x, 0.0) - jnp.log1p(jnp.exp(-jnp.abs(x)))


def _lane(shape):
    return lax.broadcasted_iota(jnp.int32, shape, len(shape) - 1)


def _hgrn_lower_bounds(hlb, layer_axis):
    def take(a, l):
        return a[l:l + 1] if layer_axis == 0 else a[:, l:l + 1]
    mx = jnp.max(hlb, axis=layer_axis, keepdims=True)
    e = jnp.exp(hlb - mx)
    p = e / jnp.sum(e, axis=layer_axis, keepdims=True)
    out = []
    acc = None
    for l in range(DEPTH):
        acc = take(p, l) if acc is None else acc + take(p, l)
        out.append(acc - take(p, 0))
    return out


def _hgrn_gates(fpre, lb):
    a = jnp.log(lb)
    c = jnp.log1p(-lb) + _log_sigmoid(fpre)
    mx = jnp.maximum(a, c)
    lf = mx + jnp.log(jnp.exp(a - mx) + jnp.exp(c - mx))
    kh = (1.0 - lb) * _sigmoid(-fpre)
    return lf, kh


def _cumsum_rows(x):
    L, W = x.shape
    sub = lax.broadcasted_iota(jnp.int32, (SUBLANES, W), 0)
    out = []
    carry = None
    for r in range(L // SUBLANES):
        t = x[r * SUBLANES:(r + 1) * SUBLANES, :]
        s = 1
        while s < SUBLANES:
            t = t + jnp.where(sub >= s, pltpu.roll(t, s, 0), 0.0)
            s *= 2
        if carry is not None:
            t = t + carry
        carry = t[SUBLANES - 1:SUBLANES, :]
        out.append(t)
    return jnp.concatenate(out, axis=0)


def _mid_row_broadcast(bc, w):
    L, W = bc.shape
    if 2 * w >= SUBLANES:
        blocks = [jnp.broadcast_to(bc[b * 2 * w + w:b * 2 * w + w + 1, :], (2 * w, W)) for b in range(L // (2 * w))]
        return blocks[0] if len(blocks) == 1 else jnp.concatenate(blocks, axis=0)
    x3 = bc.reshape(L // SUBLANES, SUBLANES, W)
    sub = lax.broadcasted_iota(jnp.int32, x3.shape, 1)
    out = None
    for b in range(SUBLANES // (2 * w)):
        piece = jnp.broadcast_to(x3[:, b * 2 * w + w:b * 2 * w + w + 1, :], x3.shape)
        out = piece if out is None else jnp.where(sub >= b * 2 * w, piece, out)
    return out.reshape(L, W)


def _layer_norm_rows(r, g, b):
    mu = jnp.mean(r, axis=-1, keepdims=True)
    d = r - mu
    var = jnp.mean(d * d, axis=-1, keepdims=True)
    return d * lax.rsqrt(var + LN_EPS) * g + b


def _head_layer_norm_padded(h, valid):
    dh = M_DH
    hz = jnp.where(valid, h, 0.0)
    mu = jnp.sum(hz, axis=-1, keepdims=True) * (1.0 / dh)
    d = jnp.where(valid, h - mu, 0.0)
    var = jnp.sum(d * d, axis=-1, keepdims=True) * (1.0 / dh)
    return d * lax.rsqrt(var + HEAD_EPS)


def _pair_rms_norm(h, low):
    sq = h * h
    m0 = jnp.sum(jnp.where(low, sq, 0.0), axis=-1, keepdims=True) * (1.0 / H_DK)
    m1 = jnp.sum(jnp.where(low, 0.0, sq), axis=-1, keepdims=True) * (1.0 / H_DK)
    return h * jnp.where(low, lax.rsqrt(m0 + HEAD_EPS), lax.rsqrt(m1 + HEAD_EPS))


def _prefill_kernel(layer, TB,
                    x_ref, win_ref, wout_ref, cos_ref, sin_ref, vec_ref, hlb_ref, lv_ref,
                    y_ref, cout_ref, nout_ref, mout_ref, convout_ref, rout_ref, sout_ref,
                    p_ref, pre_ref, ho_ref, cum_ref, ml_ref, pm_ref, kw_ref, qt_ref, kt_ref, qi_ref, dh_ref,
                    u_ref, ur_ref, uh_ref, c_ref, r_ref, s_ref, m_ref, dm_ref, dec_ref,
                    xb_ref, it_ref, gr_ref, gw_ref, ge_ref):
    L = CHUNK
    NL = N_LEVELS
    n_chunks = TB // L
    j = pl.program_id(1)
    n_tb = pl.num_programs(1)

    @pl.when(j == 0)
    def _init():
        c_ref[...] = jnp.zeros_like(c_ref)
        r_ref[...] = jnp.zeros_like(r_ref)
        s_ref[...] = jnp.zeros_like(s_ref)
        m_ref[...] = jnp.zeros_like(m_ref)
        pre_ref[0:SUBLANES, :] = jnp.zeros((SUBLANES, 2 * HP), F32)
        ti = lax.broadcasted_iota(jnp.int32, (L, L), 0)
        si = lax.broadcasted_iota(jnp.int32, (L, L), 1)
        dist = (ti - si).astype(F32)
        row = lax.broadcasted_iota(jnp.int32, (L, LANES), 0).astype(F32)
        for h in range(N_HEADS):
            dm_ref[h] = jnp.where(ti >= si, jnp.exp(dist * LG[h]), 0.0)
            dec_ref[0:L, h * LANES:(h + 1) * LANES] = jnp.exp((row + 1.0) * LG[h])
            dec_ref[L:2 * L, h * LANES:(h + 1) * LANES] = jnp.exp((L - 1.0 - row) * LG[h])

    xb_ref[...] = x_ref[0].astype(BF16)
    lb = _hgrn_lower_bounds(hlb_ref[...], 0)[layer]
    PIECE = 2 * LANES

    def project(c0):
        c1 = min(c0 + PIECE, NP_IN)
        res = _dot_nt(xb_ref[...], win_ref[c0:c1, :])
        if c0 < C_MV:
            pre_ref[SUBLANES:SUBLANES + TB, c0:c1] = res
        else:
            p_ref[:, c0:c1] = res

    def gate_math(r0, n):
        rows = slice(r0, r0 + n)
        lf, kh = _hgrn_gates(p_ref[rows, C_HF:C_HF + H_WIDTH], lb)
        p_ref[rows, C_HF:C_HF + H_WIDTH] = lf
        p_ref[rows, C_HK:C_HK + H_WIDTH] = kh
        g = p_ref[rows, C_IF:C_IF + LANES] + vec_ref[V_BIF:V_BIF + 1, 0:LANES]
        p_ref[rows, C_IF:C_IF + LANES] = jnp.where(_lane((n, LANES)) < N_HEADS, g, _log_sigmoid(g))

    def conv_math(half, r0, n):
        cols = slice(half * HP, (half + 1) * HP)
        lo = SUBLANES + r0
        def tap(i):
            return jnp.tile(vec_ref[V_CONV8 + i * SUBLANES:V_CONV8 + (i + 1) * SUBLANES, cols], (n // SUBLANES, 1))

        ext = pre_ref[lo - SUBLANES:lo + n, cols]
        acc = tap(CONV_W) + tap(CONV_W - 1) * ext[SUBLANES:, :]
        for t in range(1, CONV_W):
            shifted = pltpu.roll(ext, t, 0)[SUBLANES:, :]
            acc = acc + tap(CONV_W - 1 - t) * shifted
        scale = 1.0 if half == 0 else M_DH ** -0.5
        p_ref[r0:r0 + n, C_MQ + half * HP:C_MQ + (half + 1) * HP] = _silu(acc) * scale

    def rope_math(col, scale, r0, n):
        rows = slice(r0, r0 + n)
        cos = cos_ref[rows, :]
        sin = sin_ref[rows, :]
        for h in range(N_HEADS):
            sl = slice(col + h * LANES, col + (h + 1) * LANES)
            v = p_ref[rows, sl]
            p_ref[rows, sl] = (v * cos + pltpu.roll(v, LANES // 2, 1) * sin) * scale

    def value_math(r0, n):
        one96 = jnp.where(_lane((n, LANES)) == M_DH, 1.0, 0.0)
        for h in range(N_HEADS):
            sl = slice(C_MV + h * LANES, C_MV + (h + 1) * LANES)
            p_ref[r0:r0 + n, sl] = p_ref[r0:r0 + n, sl] + one96

    order = ([C_HF, C_HF + 4 * H_WIDTH] + list(range(C_MQ, C_MV, PIECE)) + list(range(C_RQ, C_RV, PIECE))
             + list(range(C_MV, C_MO, PIECE)) + list(range(C_HF + PIECE, C_HF + 4 * H_WIDTH, PIECE))
             + list(range(C_MO, C_RQ, PIECE)) + list(range(C_RV, C_HF, PIECE)))
    assert sorted(order) == list(range(0, NP_IN, PIECE))
    pos = {c0: i for i, c0 in enumerate(order)}
    vec = []
    for r0 in range(0, TB, 64):
        vec.append((pos[C_HF + 4 * H_WIDTH], functools.partial(gate_math, r0, 64)))
    for half in range(2):
        for r0 in range(0, TB, 32):
            vec.append((pos[C_MQ + half * HP + HP - PIECE], functools.partial(conv_math, half, r0, 32)))
    for col, scale in ((C_RQ, 1.0), (C_RK, R_DH ** -0.5)):
        for r0 in range(0, TB, 64):
            vec.append((pos[col + HP - PIECE], functools.partial(rope_math, col, scale, r0, 64)))
    for r0 in range(0, TB, 128):
        vec.append((pos[C_MV + HP - PIECE], functools.partial(value_math, r0, 128)))
    per_piece = -(-len(vec) // (len(order) - 6))
    vi = 0
    for i, c0 in enumerate(order):
        project(c0)
        done = 0
        while vi < len(vec) and vec[vi][0] < i and done < per_piece:
            vec[vi][1]()
            vi += 1
            done += 1
    while vi < len(vec):
        vec[vi][1]()
        vi += 1
    pre_ref[SUBLANES - (CONV_W - 1):SUBLANES, :] = pre_ref[TB + SUBLANES - (CONV_W - 1):TB + SUBLANES, :]

    tril = lax.broadcasted_iota(jnp.int32, (L, L), 0) >= lax.broadcasted_iota(jnp.int32, (L, L), 1)
    bd = (lax.broadcasted_iota(jnp.int32, (LANES, LANES), 0) < H_DK) == (_lane((LANES, LANES)) < H_DK)
    low = _lane((L, LANES)) < H_DK
    lane = _lane((L, LANES))

    def intra(c, carry):
        rows = pl.ds(pl.multiple_of(c * L, L), L)

        ga = p_ref[rows, C_IF:C_IF + LANES]
        cum = _cumsum_rows(ga)
        bc = _cumsum_rows(p_ref[rows, C_HF:C_HF + H_WIDTH])

        def head(col, h, dt=None):
            a = p_ref[rows, col + h * LANES:col + (h + 1) * LANES]
            return a if dt is None else a.astype(dt)

        qk_r = [_dot_nt(head(C_RQ, h, BF16), head(C_RK, h, BF16)) for h in range(N_HEADS)]
        qk_m = [_dot_nt(head(C_MQ, h, BF16), head(C_MK, h, BF16)) for h in range(N_HEADS)]

        qh = p_ref[rows, C_HQ:C_HQ + H_WIDTH]
        khh = p_ref[rows, C_HK:C_HK + H_WIDTH]
        head_sel = [jnp.where((_lane((1, H_WIDTH)) & H_DK) == hh * H_DK, 1.0, 0.0).astype(BF16) for hh in range(2)]
        for n in range(NL + 1):
            if n < NL:
                fac = jnp.exp(-jnp.abs(bc - _mid_row_broadcast(bc, L >> (n + 1))))
                qn, kn = (qh * fac).astype(BF16), (khh * fac).astype(BF16)
            else:
                qn, kn = qh.astype(BF16), khh.astype(BF16)
            qt_ref[n, 0] = qn * head_sel[0]
            qt_ref[n, 1] = qn * head_sel[1]
            kt_ref[n] = kn
        e_pre = jnp.exp(bc)
        qi_ref[rows, :] = (qh * e_pre).astype(BF16)
        dh_ref[pl.ds(c, 1), :] = e_pre[L - 1:L, :]
        k_suf = khh * jnp.exp(bc[L - 1:L, :] - bc)

        for h in range(N_HEADS):
            v = head(C_RV, h, BF16)
            s = (qk_r[h] * dm_ref[h]).astype(BF16)
            ho_ref[rows, HP + h * LANES:HP + (h + 1) * LANES] = _dot(s, v)
            kd_t = (head(C_RK, h) * dec_ref[L:2 * L, h * LANES:(h + 1) * LANES]).T.astype(BF16)
            ur_ref[c, h] = _dot(kd_t, v)

        lv = lv_ref[...]
        heads = [(p, hh) for p in range(N_HEADS // 2) for hh in range(2)]

        def level_scores(p, hh):
            psl = slice(p * LANES, (p + 1) * LANES)
            return [_dot_nt(qt_ref[n, hh, :, psl], kt_ref[n, :, psl]) for n in range(NL + 1)]

        def fold(scores):
            s_mat = jnp.zeros((L, L), F32)
            for n, sc in enumerate(scores):
                s_mat = jnp.where(lv == n, sc, s_mat)
            return s_mat.astype(BF16)

        pending = level_scores(*heads[0])

        cum_ref[rows, :] = cum
        a_all = ga - pltpu.roll(cum, LANES - N_HEADS, 1)
        a_t = a_all.T
        ml = jnp.zeros((L, LANES), F32)
        for h in range(N_HEADS):
            b_col = cum[:, N_HEADS + h:N_HEADS + h + 1]
            log_d = jnp.where(tril, b_col + a_t[h:h + 1, :], -jnp.inf)
            m_loc = jnp.max(log_d, axis=1, keepdims=True)
            pm_ref[h] = (qk_m[h] * jnp.exp(log_d - m_loc)).astype(BF16)
            ml = jnp.where(lane == h, m_loc, ml)
            w_loc = jnp.exp(b_col[L - 1:L, :] + a_all[:, h:h + 1] - m_loc[L - 1:L, :])
            kw_ref[h] = (head(C_MK, h) * w_loc).T.astype(BF16)
        ml_ref[rows, :] = ml

        s_bf = []
        for i in range(len(heads)):
            nxt = level_scores(*heads[i + 1]) if i + 1 < len(heads) else None
            s_bf.append(fold(pending))
            pending = nxt
            if i == 0:
                for h in range(N_HEADS):
                    v = head(C_MV, h, BF16)
                    ho_ref[rows, h * LANES:(h + 1) * LANES] = _dot(pm_ref[h], v)
                    u_ref[c, h] = _dot(kw_ref[h], v)
        for p in range(N_HEADS // 2):
            psl = slice(p * LANES, (p + 1) * LANES)
            vp = p_ref[rows, C_HI + p * LANES:C_HI + (p + 1) * LANES]
            ho_ref[rows, 2 * HP + p * LANES:2 * HP + (p + 1) * LANES] = (
                _dot(s_bf[2 * p], jnp.where(low, vp, 0.0).astype(BF16))
                + _dot(s_bf[2 * p + 1], jnp.where(low, 0.0, vp).astype(BF16)))
            uh_ref[c, p] = jnp.where(bd, _dot(k_suf[:, psl].T.astype(BF16), vp.astype(BF16)), 0.0)
        return carry

    lax.fori_loop(0, n_chunks, intra, 0, unroll=True)

    for c in range(n_chunks):
        rows = slice(c * L, (c + 1) * L)

        ml = ml_ref[rows, :]
        b_all = pltpu.roll(cum_ref[rows, :], LANES - N_HEADS, 1)
        m_prev = m_ref[0:1, :]
        m_t = jnp.maximum(ml, b_all + m_prev)
        r_all = jnp.exp(ml - m_t)
        w_all = jnp.exp(b_all + m_prev - m_t)
        e_all = jnp.exp(-m_t)
        for h in range(N_HEADS):
            sl = slice(h * LANES, (h + 1) * LANES)
            gr_ref[rows, sl] = jnp.broadcast_to(r_all[:, h:h + 1], (L, LANES))
            gw_ref[rows, sl] = jnp.broadcast_to(w_all[:, h:h + 1], (L, LANES))
            ge_ref[rows, sl] = jnp.broadcast_to(e_all[:, h:h + 1], (L, LANES))
        m_new = m_t[L - 1:L, :]
        dec_row = jnp.exp(b_all[L - 1:L, :] + m_prev - m_new)
        g_row = jnp.exp(ml[L - 1:L, :] - m_new)
        m_ref[0:1, :] = m_new

        for h in range(N_HEADS):
            q = p_ref[rows, C_MQ + h * LANES:C_MQ + (h + 1) * LANES].astype(BF16)
            it_ref[rows, h * LANES:(h + 1) * LANES] = _dot(q, c_ref[h].astype(BF16))
        for h in range(N_HEADS):
            q = p_ref[rows, C_RQ + h * LANES:C_RQ + (h + 1) * LANES]
            qd = (q * dec_ref[0:L, h * LANES:(h + 1) * LANES]).astype(BF16)
            it_ref[rows, HP + h * LANES:HP + (h + 1) * LANES] = _dot(qd, r_ref[h].astype(BF16))
        for p in range(N_HEADS // 2):
            psl = slice(p * LANES, (p + 1) * LANES)
            it_ref[rows, 2 * HP + p * LANES:2 * HP + (p + 1) * LANES] = _dot(qi_ref[rows, psl], s_ref[p].astype(BF16))

        d_col = dh_ref[c:c + 1, :].T
        for h in range(N_HEADS):
            c_ref[h] = dec_row[:, h:h + 1] * c_ref[h] + g_row[:, h:h + 1] * u_ref[c, h]
            r_ref[h] = math.exp(L * LG[h]) * r_ref[h] + ur_ref[c, h]
        for p in range(N_HEADS // 2):
            s_ref[p] = d_col[p * LANES:(p + 1) * LANES, :] * s_ref[p] + uh_ref[c, p]

    def mix_rows(rows, T):
        valid = _lane((T, LANES)) < M_DH
        low_t = _lane((T, LANES)) < H_DK
        parts = []
        for h in range(N_HEADS):
            sl = slice(h * LANES, (h + 1) * LANES)
            num = gr_ref[rows, sl] * ho_ref[rows, sl] + gw_ref[rows, sl] * it_ref[rows, sl]
            den = num[:, M_DH:M_DH + 1]
            hm = num / jnp.maximum(jnp.abs(den), ge_ref[rows, sl])
            o = p_ref[rows, C_MO + h * LANES:C_MO + (h + 1) * LANES]
            z = p_ref[rows, C_MZ + h * LANES:C_MZ + (h + 1) * LANES]
            gain = vec_ref[V_MNW:V_MNW + 1, sl]
            parts.append((_head_layer_norm_padded(hm, valid) * gain * _sigmoid(o) * _silu(z)).astype(BF16))
        for h in range(N_HEADS):
            sl = slice(HP + h * LANES, HP + (h + 1) * LANES)
            hn = _head_layer_norm_padded(ho_ref[rows, sl] + it_ref[rows, sl], valid)
            g = p_ref[rows, C_RG + h * LANES:C_RG + (h + 1) * LANES]
            parts.append((hn * vec_ref[V_RNW:V_RNW + 1, h * LANES:(h + 1) * LANES] * _silu(g)).astype(BF16))
        for p in range(N_HEADS // 2):
            sl = slice(2 * HP + p * LANES, 2 * HP + (p + 1) * LANES)
            hn = _pair_rms_norm(ho_ref[rows, sl] + it_ref[rows, sl], low_t)
            g = p_ref[rows, C_HG + p * LANES:C_HG + (p + 1) * LANES]
            parts.append((hn * vec_ref[V_HNW:V_HNW + 1, p * LANES:(p + 1) * LANES] * _silu(g)).astype(BF16))
        return jnp.concatenate(parts, axis=-1)

    def out_rows(rows, proj):
        y_ref[0, rows, :] = _layer_norm_rows(ALPHA * x_ref[0, rows, :] + proj,
                                             vec_ref[V_LNG:V_LNG + 1, :], vec_ref[V_LNB:V_LNB + 1, :])

    half = TB // 2
    rows0, rows1 = slice(0, half), slice(half, TB)
    proj0 = _dot(mix_rows(rows0, half), wout_ref[...])
    mix1 = mix_rows(rows1, half)
    out_rows(rows0, proj0)
    out_rows(rows1, _dot(mix1, wout_ref[...]))

    @pl.when(j == n_tb - 1)
    def _final_states():
        convout_ref[0] = pre_ref[SUBLANES - (CONV_W - 1):SUBLANES, :]
        for h in range(N_HEADS):
            c_aug = c_ref[h]
            cout_ref[0, h] = c_aug[0:M_DH, 0:M_DH]
            nout_ref[0, h:h + 1, :] = c_aug.T[M_DH:M_DH + 1, 0:M_DH]
            r_full = r_ref[h]
            rout_ref[0, h] = jnp.concatenate(
                [r_full[0:ROPE_HALF, 0:R_DH], r_full[LANES // 2:LANES // 2 + ROPE_HALF, 0:R_DH]], axis=0)
            p, hh = divmod(h, 2)
            sout_ref[0, h] = s_ref[p][hh * H_DK:(hh + 1) * H_DK, hh * H_DK:(hh + 1) * H_DK]
        mout_ref[0] = m_ref[0:1, 0:N_HEADS]


def _full_spec(shape):
    nd = len(shape)
    return pl.BlockSpec(shape, lambda *_: (0,) * nd)


def _prefill_layer(layer, x, win, wout, cos, sin, vec, hlb, lv, TB):
    B, T, _ = x.shape
    n_tb = T // TB
    n_chunks = TB // CHUNK
    kern = functools.partial(_prefill_kernel, layer, TB)

    def layer_spec(a, single_buffer=False):
        kw = dict(pipeline_mode=pl.Buffered(1)) if single_buffer else {}
        return pl.BlockSpec((None,) + a.shape[1:], lambda b, j: (layer,) + (0,) * (a.ndim - 1), **kw)

    in_specs = [
        pl.BlockSpec((1, TB, D_MODEL), lambda b, j: (b, j, 0)),
        layer_spec(win, True), layer_spec(wout, True),
        pl.BlockSpec((TB, LANES), lambda b, j: (j, 0)),
        pl.BlockSpec((TB, LANES), lambda b, j: (j, 0)),
        layer_spec(vec), _full_spec(hlb.shape), _full_spec(lv.shape),
    ]
    out_shape = (
        jax.ShapeDtypeStruct((B, T, D_MODEL), F32),
        jax.ShapeDtypeStruct((B, N_HEADS, M_DH, M_DH), F32),
        jax.ShapeDtypeStruct((B, N_HEADS, M_DH), F32),
        jax.ShapeDtypeStruct((B, 1, N_HEADS), F32),
        jax.ShapeDtypeStruct((B, CONV_W - 1, 2 * HP), F32),
        jax.ShapeDtypeStruct((B, N_HEADS, R_DH, R_DH), F32),
        jax.ShapeDtypeStruct((B, N_HEADS, H_DK, H_DK), F32),
    )
    out_specs = (
        pl.BlockSpec((1, TB, D_MODEL), lambda b, j: (b, j, 0)),
        pl.BlockSpec((1, N_HEADS, M_DH, M_DH), lambda b, j: (b, 0, 0, 0)),
        pl.BlockSpec((1, N_HEADS, M_DH), lambda b, j: (b, 0, 0)),
        pl.BlockSpec((1, 1, N_HEADS), lambda b, j: (b, 0, 0)),
        pl.BlockSpec((1, CONV_W - 1, 2 * HP), lambda b, j: (b, 0, 0)),
        pl.BlockSpec((1, N_HEADS, R_DH, R_DH), lambda b, j: (b, 0, 0, 0)),
        pl.BlockSpec((1, N_HEADS, H_DK, H_DK), lambda b, j: (b, 0, 0, 0)),
    )
    L = CHUNK
    scratch = [
        pltpu.VMEM((TB, NP_SCR), F32),
        pltpu.VMEM((TB + SUBLANES, 2 * HP), F32),
        pltpu.VMEM((TB, MIXP), F32),
        pltpu.VMEM((TB, LANES), F32),
        pltpu.VMEM((TB, LANES), F32),
        pltpu.VMEM((N_HEADS, L, L), BF16),
        pltpu.VMEM((N_HEADS, LANES, L), BF16),
        pltpu.VMEM((N_LEVELS + 1, 2, L, H_WIDTH), BF16),
        pltpu.VMEM((N_LEVELS + 1, L, H_WIDTH), BF16),
        pltpu.VMEM((TB, H_WIDTH), BF16),
        pltpu.VMEM((max(n_chunks, SUBLANES), H_WIDTH), F32),
        pltpu.VMEM((n_chunks, N_HEADS, LANES, LANES), F32),
        pltpu.VMEM((n_chunks, N_HEADS, LANES, LANES), F32),
        pltpu.VMEM((n_chunks, N_HEADS // 2, LANES, LANES), F32),
        pltpu.VMEM((N_HEADS, LANES, LANES), F32),
        pltpu.VMEM((N_HEADS, LANES, LANES), F32),
        pltpu.VMEM((N_HEADS // 2, LANES, LANES), F32),
        pltpu.VMEM((SUBLANES, LANES), F32),
        pltpu.VMEM((N_HEADS, L, L), F32),
        pltpu.VMEM((2 * L, HP), F32),
        pltpu.VMEM((TB, D_MODEL), BF16),
        pltpu.VMEM((TB, MIXP), F32),
        pltpu.VMEM((TB, HP), F32),
        pltpu.VMEM((TB, HP), F32),
        pltpu.VMEM((TB, HP), F32),
    ]
    return pl.pallas_call(
        kern, grid=(B, n_tb), in_specs=in_specs, out_specs=out_specs, out_shape=out_shape,
        scratch_shapes=scratch,
        compiler_params=pltpu.CompilerParams(dimension_semantics=("arbitrary", "arbitrary"),
                                             vmem_limit_bytes=VMEM_LIMIT),
        name=f"prefill_layer{layer}",
    )(x, win, wout, cos, sin, vec, hlb, lv)


DEC_KB = 4
KR_M = M_DH // DEC_KB
KR_H = H_DK // DEC_KB
G_DEC, G_WS, G_EN, G_GAM, G_SM, G_QN, G_SR, G_SH = (i * N_HEADS for i in range(8))


def _stream_state(q_blk, dec, kw_blk, v, in_ref, out_ref, nrows):
    acc = jnp.zeros_like(v)
    for kk in range(nrows):
        st = in_ref[0, 0, kk]
        d = dec if dec.shape[0] == 1 else dec[kk:kk + 1, :]
        acc = acc + q_blk[kk:kk + 1, :] * st
        out_ref[0, 0, kk] = d * st + kw_blk[kk:kk + 1, :] * v
    return acc


def _decode_kernel(x_ref, wt_ref, wout_ref, cos_ref, sin_ref, dvec_ref, dcol_ref, hlbt_ref, vec_ref,
                   cin_ref, nin_ref, min_ref, convin_ref, rin_ref, sin_st_ref,
                   y_ref, cout_ref, nout_ref, mout_ref, convout_ref, rout_ref, sout_ref,
                   p_ref, xs_ref, ht_ref, mix_ref, g_ref):
    l = pl.program_id(0)
    h = pl.program_id(1)
    kb = pl.program_id(2)
    first = jnp.logical_and(h == 0, kb == 0)
    last = jnp.logical_and(h == N_HEADS - 1, kb == DEC_KB - 1)
    B = x_ref.shape[0]

    @pl.when(jnp.logical_and(first, l == 0))
    def _load_x():
        xs_ref[...] = x_ref[...]

    @pl.when(first)
    def _project():
        p_ref[0:N_IN, :] = _dot_nt(wt_ref[0], xs_ref[...].astype(BF16))
        ht_ref[...] = jnp.zeros_like(ht_ref)

        pre = p_ref[O_MQ:O_MQ + 2 * M_WIDTH, :].T
        cw = dvec_ref[0, 0:CONV_W, :]
        acc = dvec_ref[0, CONV_W:CONV_W + 1, :] + cw[CONV_W - 1:CONV_W, :] * pre
        for jj in range(CONV_W - 1):
            acc = acc + cw[jj:jj + 1, :] * convin_ref[0, jj]
        for jj in range(CONV_W - 2):
            convout_ref[0, jj] = convin_ref[0, jj + 1]
        convout_ref[0, CONV_W - 2] = pre
        p_ref[O_MQ:O_MQ + 2 * M_WIDTH, :] = _silu(acc).T

        gt = p_ref[O_IF:O_IF + 2 * N_HEADS, :] + dcol_ref[0, D_MIX:D_MIX + 2 * N_HEADS, :]
        ig = gt[0:N_HEADS]
        lf = _log_sigmoid(gt[N_HEADS:2 * N_HEADS])
        m_prev = min_ref[0]
        m_t = jnp.maximum(ig, lf + m_prev)
        dec = jnp.exp(lf + m_prev - m_t)
        w_s = jnp.exp(ig - m_t)
        mout_ref[0] = m_t
        g_ref[G_DEC:G_DEC + N_HEADS, :] = dec
        g_ref[G_WS:G_WS + N_HEADS, :] = w_s
        g_ref[G_EN:G_EN + N_HEADS, :] = jnp.exp(-m_t)
        for hh in range(N_HEADS):
            q = p_ref[O_MQ + hh * M_DH:O_MQ + (hh + 1) * M_DH, :]
            k = p_ref[O_MK + hh * M_DH:O_MK + (hh + 1) * M_DH, :] * (M_DH ** -0.5)
            n_old = nin_ref[0, hh]
            d_h = dec[hh:hh + 1]
            kw = k * w_s[hh:hh + 1]
            g_ref[G_SM + hh:G_SM + hh + 1, :] = jnp.sum(q * k, axis=0, keepdims=True) * w_s[hh:hh + 1]
            g_ref[G_QN + hh:G_QN + hh + 1, :] = jnp.sum(q * n_old, axis=0, keepdims=True)
            nout_ref[0, hh] = d_h * n_old + kw
            p_ref[O_MK + hh * M_DH:O_MK + (hh + 1) * M_DH, :] = kw

        cos = cos_ref[...]
        sin = sin_ref[...]
        for hh in range(N_HEADS):
            g_ref[G_GAM + hh:G_GAM + hh + 1, :] = jnp.full((1, B), math.exp(LG[hh]), F32)
            rot = []
            for off, scale in ((O_RQ, 1.0), (O_RK, R_DH ** -0.5)):
                x1 = p_ref[off + hh * R_DH:off + hh * R_DH + ROPE_HALF, :]
                x2 = p_ref[off + hh * R_DH + ROPE_HALF:off + (hh + 1) * R_DH, :]
                r1 = (x1 * cos - x2 * sin) * scale
                r2 = (x1 * sin + x2 * cos) * scale
                p_ref[off + hh * R_DH:off + hh * R_DH + ROPE_HALF, :] = r1
                p_ref[off + hh * R_DH + ROPE_HALF:off + (hh + 1) * R_DH, :] = r2
                rot.append((r1, r2))
            (q1, q2), (k1, k2) = rot
            g_ref[G_SR + hh:G_SR + hh + 1, :] = (jnp.sum(q1 * k1, axis=0, keepdims=True)
                                                 + jnp.sum(q2 * k2, axis=0, keepdims=True))

        lbs = _hgrn_lower_bounds(hlbt_ref[...], 1)
        lb = lbs[0]
        for i in range(1, DEPTH):
            lb = jnp.where(l == i, lbs[i], lb)
        lfh, kh = _hgrn_gates(p_ref[O_HF:O_HF + H_WIDTH, :], lb)
        eh = jnp.exp(lfh)
        qh = p_ref[O_HQ:O_HQ + H_WIDTH, :]
        p_ref[O_HF:O_HF + H_WIDTH, :] = eh
        p_ref[O_HK:O_HK + H_WIDTH, :] = kh
        p_ref[O_HQ:O_HQ + H_WIDTH, :] = qh * eh
        qk = qh * kh
        for hh in range(N_HEADS):
            g_ref[G_SH + hh:G_SH + hh + 1, :] = jnp.sum(qk[hh * H_DK:(hh + 1) * H_DK], axis=0, keepdims=True)

    def rows(base, width, n):
        return pl.ds(pl.multiple_of(base + h * width + kb * n, SUBLANES), n)

    dec = g_ref[pl.ds(G_DEC + h, 1), :]
    acc = _stream_state(p_ref[rows(O_MQ, M_DH, KR_M), :], dec, p_ref[rows(O_MK, M_DH, KR_M), :],
                        p_ref[pl.ds(pl.multiple_of(O_MV + h * M_DH, SUBLANES), M_DH), :], cin_ref, cout_ref, KR_M)
    ht_ref[pl.ds(pl.multiple_of(h * M_DH, SUBLANES), M_DH), :] += acc

    gam = g_ref[pl.ds(G_GAM + h, 1), :]
    acc = _stream_state(p_ref[rows(O_RQ, R_DH, KR_M), :], gam, p_ref[rows(O_RK, R_DH, KR_M), :],
                        p_ref[pl.ds(pl.multiple_of(O_RV + h * R_DH, SUBLANES), R_DH), :], rin_ref, rout_ref, KR_M)
    ht_ref[pl.ds(pl.multiple_of(M_WIDTH + h * R_DH, SUBLANES), R_DH), :] += acc

    acc = _stream_state(p_ref[rows(O_HQ, H_DK, KR_H), :], p_ref[rows(O_HF, H_DK, KR_H), :],
                        p_ref[rows(O_HK, H_DK, KR_H), :],
                        p_ref[pl.ds(pl.multiple_of(O_HI + h * H_DK, SUBLANES), H_DK), :], sin_st_ref, sout_ref, KR_H)
    ht_ref[pl.ds(pl.multiple_of(M_WIDTH + R_WIDTH + h * H_DK, SUBLANES), H_DK), :] += acc

    @pl.when(last)
    def _finish():
        def head_ln(a):
            mu = jnp.mean(a, axis=0, keepdims=True)
            d = a - mu
            return d * lax.rsqrt(jnp.mean(d * d, axis=0, keepdims=True) + HEAD_EPS)

        for hh in range(N_HEADS):
            sl = slice(hh * M_DH, (hh + 1) * M_DH)
            v = p_ref[O_MV + hh * M_DH:O_MV + (hh + 1) * M_DH, :]
            d_h = g_ref[G_DEC + hh:G_DEC + hh + 1, :]
            s = g_ref[G_SM + hh:G_SM + hh + 1, :]
            num = s * v + ht_ref[sl, :] * d_h
            den = s + g_ref[G_QN + hh:G_QN + hh + 1, :] * d_h
            hm = num / jnp.maximum(jnp.abs(den), g_ref[G_EN + hh:G_EN + hh + 1, :])
            o = p_ref[O_MO + hh * M_DH:O_MO + (hh + 1) * M_DH, :]
            z = p_ref[O_MZ + hh * M_DH:O_MZ + (hh + 1) * M_DH, :]
            mix_ref[sl, :] = head_ln(hm) * dcol_ref[0, sl, :] * _sigmoid(o) * _silu(z)
        for hh in range(N_HEADS):
            sl = slice(M_WIDTH + hh * R_DH, M_WIDTH + (hh + 1) * R_DH)
            v = p_ref[O_RV + hh * R_DH:O_RV + (hh + 1) * R_DH, :]
            hr = g_ref[G_SR + hh:G_SR + hh + 1, :] * v + ht_ref[sl, :] * math.exp(LG[hh])
            gg = p_ref[O_RG + hh * R_DH:O_RG + (hh + 1) * R_DH, :]
            mix_ref[sl, :] = head_ln(hr) * dcol_ref[0, sl, :] * _silu(gg)
        for hh in range(N_HEADS):
            sl = slice(M_WIDTH + R_WIDTH + hh * H_DK, M_WIDTH + R_WIDTH + (hh + 1) * H_DK)
            v = p_ref[O_HI + hh * H_DK:O_HI + (hh + 1) * H_DK, :]
            ho = g_ref[G_SH + hh:G_SH + hh + 1, :] * v + ht_ref[sl, :]
            hn = ho * lax.rsqrt(jnp.mean(ho * ho, axis=0, keepdims=True) + HEAD_EPS)
            gg = p_ref[O_HG + hh * H_DK:O_HG + (hh + 1) * H_DK, :]
            mix_ref[sl, :] = hn * dcol_ref[0, sl, :] * _silu(gg)
        x = xs_ref[...]
        out = _dot(mix_ref[...].T.astype(BF16), wout_ref[0])
        y = _layer_norm_rows(ALPHA * x + out, vec_ref[0, V_LNG:V_LNG + 1, :], vec_ref[0, V_LNB:V_LNB + 1, :])
        xs_ref[...] = y
        y_ref[0] = y


def _decode(x, wt, wout, cos, sin, dvec, dcol, hlbt, vec, st_c, st_n, st_m, st_conv, st_r, st_s):
    B = x.shape[0]

    def lspec(a):
        nd = a.ndim
        return pl.BlockSpec((1,) + a.shape[1:], lambda l, h, kb: (l,) + (0,) * (nd - 1))

    def kv_spec(a, n):
        return pl.BlockSpec((1, 1, n) + a.shape[3:], lambda l, h, kb: (l, h, kb, 0, 0))

    in_specs = [
        _full_spec(x.shape), lspec(wt), lspec(wout), _full_spec(cos.shape), _full_spec(sin.shape),
        lspec(dvec), lspec(dcol), _full_spec(hlbt.shape), lspec(vec),
        kv_spec(st_c, KR_M), lspec(st_n), lspec(st_m), lspec(st_conv), kv_spec(st_r, KR_M), kv_spec(st_s, KR_H),
    ]
    out_shape = (
        jax.ShapeDtypeStruct((DEPTH, B, D_MODEL), F32),
        jax.ShapeDtypeStruct(st_c.shape, F32), jax.ShapeDtypeStruct(st_n.shape, F32),
        jax.ShapeDtypeStruct(st_m.shape, F32), jax.ShapeDtypeStruct(st_conv.shape, F32),
        jax.ShapeDtypeStruct(st_r.shape, F32), jax.ShapeDtypeStruct(st_s.shape, F32),
    )
    out_specs = (
        pl.BlockSpec((1, B, D_MODEL), lambda l, h, kb: (l, 0, 0)),
        kv_spec(st_c, KR_M), lspec(st_n), lspec(st_m), lspec(st_conv), kv_spec(st_r, KR_M), kv_spec(st_s, KR_H),
    )
    scratch = [
        pltpu.VMEM((N_IN + H_WIDTH, B), F32),
        pltpu.VMEM((B, D_MODEL), F32),
        pltpu.VMEM((D_MIX, B), F32),
        pltpu.VMEM((D_MIX, B), F32),
        pltpu.VMEM((8 * N_HEADS, B), F32),
    ]
    return pl.pallas_call(
        _decode_kernel, grid=(DEPTH, N_HEADS, DEC_KB), in_specs=in_specs, out_specs=out_specs,
        out_shape=out_shape, scratch_shapes=scratch,
        compiler_params=pltpu.CompilerParams(dimension_semantics=("arbitrary", "arbitrary", "arbitrary"),
                                             vmem_limit_bytes=VMEM_LIMIT),
        name="decode_step",
    )(x, wt, wout, cos, sin, dvec, dcol, hlbt, vec, st_c, st_n, st_m, st_conv, st_r, st_s)


def _rope_angles(pos):
    inv = ROPE_BASE ** (-jnp.arange(ROPE_HALF, dtype=F32) / ROPE_HALF)
    ang = pos.astype(F32)[:, None] * inv[None, :]
    return jnp.cos(ang), jnp.sin(ang)


PREFILL_BLOCK = 512


def kernel(x_prompt, x_sample, state_mlstm_C, state_mlstm_n, state_mlstm_m, state_mlstm_conv, state_ret, state_hgrn, w_in, conv_w, conv_b, b_mgate, m_norm_w, r_norm_w, h_norm_w, hgrn_lb, w_out, ln_g, ln_b):
    B, T, _ = x_prompt.shape

    wt = jnp.swapaxes(w_in, 1, 2).astype(BF16)
    win_p = jnp.concatenate(
        [_pad_heads(wt[:, O_MQ:O_IF], 1),
         _pad_rope_heads(wt[:, O_RQ:O_RV], 1), _pad_heads(wt[:, O_RV:O_HF], 1),
         wt[:, O_HF:N_IN],
         jnp.pad(wt[:, O_IF:O_RQ], ((0, 0), (0, LANES - 2 * N_HEADS), (0, 0)))], axis=1)
    wout_b = w_out.astype(BF16)
    wout_p = jnp.concatenate([_pad_heads(wout_b[:, 0:M_WIDTH + R_WIDTH], 1), wout_b[:, M_WIDTH + R_WIDTH:]],
                             axis=1)

    def row(a):
        return jnp.pad(a, ((0, 0), (0, D_MODEL - a.shape[-1])))[:, None, :]

    vec = jnp.concatenate([_pad_heads(conv_w, 2), row(_pad_heads(conv_b, 1)), row(b_mgate),
                           row(_pad_heads(m_norm_w, 1)), row(_pad_heads(r_norm_w, 1)), row(h_norm_w),
                           row(ln_g), row(ln_b)], axis=1)
    vec = jnp.pad(vec, ((0, 0), (0, V_ROWS - vec.shape[1]), (0, 0)))
    vec = jnp.concatenate([vec, jnp.repeat(vec[:, V_CONVW:V_CONVB + 1], SUBLANES, axis=1)], axis=1)
    dvec = jnp.pad(jnp.concatenate([conv_w, conv_b[:, None, :]], axis=1),
                   ((0, 0), (0, SUBLANES - CONV_W - 1), (0, 0)))
    dcol = jnp.concatenate([m_norm_w, r_norm_w, h_norm_w, b_mgate], axis=1)[:, :, None]
    hlb = hgrn_lb.astype(F32)

    lv = jnp.asarray(_level_table(CHUNK))

    c, s = _rope_angles(jnp.arange(T, dtype=jnp.int32))
    z = jnp.zeros((T, LANES // 2 - ROPE_HALF), F32)
    cos_p = jnp.concatenate([c, z, c, z], axis=1)
    sin_p = jnp.concatenate([-s, z, s, z], axis=1)

    hp = x_prompt
    st = [[] for _ in range(6)]
    for l in range(DEPTH):
        outs = _prefill_layer(l, hp, win_p, wout_p, cos_p, sin_p, vec, hlb, lv, PREFILL_BLOCK)
        hp = outs[0]
        for k in range(6):
            st[k].append(outs[1 + k])
    mC_p = jnp.stack(st[0])
    mn_p = jnp.stack(st[1])
    mm_p = jnp.stack(st[2])[:, :, 0, :]
    conv_p = _unpad_heads(jnp.stack(st[3]))
    ret_p = jnp.stack(st[4])
    hgrn_p = jnp.stack(st[5])

    n_s = x_sample.shape[0]
    cs, ss = _rope_angles(PAST_LEN + jnp.arange(x_sample.shape[1], dtype=jnp.int32))
    cos_s = jnp.broadcast_to(cs[0][:, None], (ROPE_HALF, n_s))
    sin_s = jnp.broadcast_to(ss[0][:, None], (ROPE_HALF, n_s))
    ys, c_t, n_t, m_t, conv_t, r_t, s_t = _decode(
        x_sample[:, 0, :], wt, wout_b, cos_s, sin_s, dvec, dcol, hlb.T, vec,
        jnp.transpose(state_mlstm_C, (0, 2, 3, 4, 1)), jnp.transpose(state_mlstm_n, (0, 2, 3, 1)),
        jnp.transpose(state_mlstm_m, (0, 2, 1)), jnp.transpose(state_mlstm_conv, (0, 2, 1, 3)),
        jnp.transpose(state_ret, (0, 2, 3, 4, 1)), jnp.transpose(state_hgrn, (0, 2, 3, 4, 1)))
    hs = ys[DEPTH - 1][:, None, :]
    mC_s = jnp.transpose(c_t, (0, 4, 1, 2, 3))
    mn_s = jnp.transpose(n_t, (0, 3, 1, 2))
    mm_s = jnp.transpose(m_t, (0, 2, 1))
    conv_s = jnp.transpose(conv_t, (0, 2, 1, 3))
    ret_s = jnp.transpose(r_t, (0, 4, 1, 2, 3))
    hgrn_s = jnp.transpose(s_t, (0, 4, 1, 2, 3))

    return (hp, hs, mC_p, mn_p, mm_p, conv_p, ret_p, hgrn_p, mC_s, mn_s, mm_s, conv_s, ret_s, hgrn_s)
```

```python
import functools
import math

import numpy as np
import jax
import jax.numpy as jnp
from jax import lax
from jax.experimental import pallas as pl
from jax.experimental.pallas import tpu as pltpu

F32 = jnp.float32
BF16 = jnp.bfloat16

D_MODEL = 1024
DEPTH = 2
PAST_LEN = 16384
N_HEADS = 4
M_DH = 96
R_DH = 96
H_DK = 64
M_WIDTH = N_HEADS * M_DH
R_WIDTH = N_HEADS * R_DH
H_WIDTH = N_HEADS * H_DK
D_MIX = M_WIDTH + R_WIDTH + H_WIDTH
CONV_W = 4
ROPE_BASE = 10000.0
LN_EPS = 1e-5
HEAD_EPS = 1e-6
ALPHA = (2 * DEPTH) ** 0.25

LANES = 128
SUBLANES = 8
HP = N_HEADS * LANES
ROPE_HALF = R_DH // 2
VMEM_LIMIT = 56 * 1024 * 1024

O_MQ, O_MK, O_MV, O_MO, O_MZ = 0, M_WIDTH, 2 * M_WIDTH, 3 * M_WIDTH, 4 * M_WIDTH
O_IF = 5 * M_WIDTH
O_RQ = O_IF + 2 * N_HEADS
O_RK, O_RV, O_RG = O_RQ + R_WIDTH, O_RQ + 2 * R_WIDTH, O_RQ + 3 * R_WIDTH
O_HF = O_RQ + 4 * R_WIDTH
O_HI, O_HQ, O_HG = O_HF + H_WIDTH, O_HF + 2 * H_WIDTH, O_HF + 3 * H_WIDTH
N_IN = O_HF + 4 * H_WIDTH
O_HK = N_IN

C_MQ, C_MK, C_MV, C_MO, C_MZ = 0, HP, 2 * HP, 3 * HP, 4 * HP
C_RQ, C_RK, C_RV, C_RG = 5 * HP, 6 * HP, 7 * HP, 8 * HP
C_HF = 9 * HP
C_HI = C_HF + H_WIDTH
C_HQ = C_HI + H_WIDTH
C_HG = C_HQ + H_WIDTH
C_IF = C_HG + H_WIDTH
NP_IN = C_IF + LANES
C_HK = NP_IN
NP_SCR = NP_IN + H_WIDTH
MIXP = 2 * HP + H_WIDTH

V_CONVW, V_CONVB, V_BIF, V_MNW, V_RNW, V_HNW, V_LNG, V_LNB = 0, CONV_W, CONV_W + 1, CONV_W + 2, CONV_W + 3, CONV_W + 4, CONV_W + 5, CONV_W + 6
V_ROWS = 2 * SUBLANES
V_CONV8 = V_ROWS
V_ROWS8 = V_ROWS + (CONV_W + 1) * SUBLANES

CHUNK = 128
N_LEVELS = int(math.log2(CHUNK))
LG = [math.log1p(-2.0 ** (-5.0 - h)) for h in range(N_HEADS)]


def _pad_groups(a, axis, group, padded):
    shp = a.shape
    n = shp[axis] // group
    a = a.reshape(shp[:axis] + (n, group) + shp[axis + 1:])
    pad = [(0, 0)] * a.ndim
    pad[axis + 1] = (0, padded - group)
    return jnp.pad(a, pad).reshape(shp[:axis] + (n * padded,) + shp[axis + 1:])


def _pad_heads(a, axis):
    return _pad_groups(a, axis, M_DH, LANES)


def _pad_rope_heads(a, axis):
    return _pad_groups(a, axis, ROPE_HALF, LANES // 2)


def _unpad_heads(a, dh=M_DH):
    shp = a.shape
    return a.reshape(shp[:-1] + (shp[-1] // LANES, LANES))[..., :dh].reshape(shp[:-1] + (shp[-1] // LANES * dh,))


def _level_table(L):
    nl = int(math.log2(L))
    tt = np.arange(L)[:, None]
    ss = np.arange(L)[None, :]
    lv = np.full((L, L), -1, np.int32)
    for n in range(nl):
        w = L >> (n + 1)
        same = (tt // (2 * w)) == (ss // (2 * w))
        cond = same & ((tt % (2 * w)) >= w) & ((ss % (2 * w)) < w)
        lv = np.where(cond, n, lv)
    lv = np.where(tt == ss, nl, lv)
    return lv.astype(np.int32)


def _dot(a, b):
    return jnp.dot(a, b, preferred_element_type=F32)


def _dot_nt(a, b):
    return lax.dot_general(a, b, (((1,), (1,)), ((), ())), preferred_element_type=F32)


def _sigmoid(x):
    return 1.0 / (1.0 + jnp.exp(-x))


def _silu(x):
    return x * _sigmoid(x)


def _log_sigmoid(x):
    return jnp.minimum(x, 0.0) - jnp.log1p(jnp.exp(-jnp.abs(x)))


def _lane(shape):
    return lax.broadcasted_iota(jnp.int32, shape, len(shape) - 1)


def _hgrn_lower_bounds(hlb, layer_axis):
    def take(a, l):
        return a[l:l + 1] if layer_axis == 0 else a[:, l:l + 1]
    mx = jnp.max(hlb, axis=layer_axis, keepdims=True)
    e = jnp.exp(hlb - mx)
    p = e / jnp.sum(e, axis=layer_axis, keepdims=True)
    out = []
    acc = None
    for l in range(DEPTH):
        acc = take(p, l) if acc is None else acc + take(p, l)
        out.append(acc - take(p, 0))
    return out


def _hgrn_gates(fpre, lb):
    a = jnp.log(lb)
    c = jnp.log1p(-lb) + _log_sigmoid(fpre)
    mx = jnp.maximum(a, c)
    lf = mx + jnp.log(jnp.exp(a - mx) + jnp.exp(c - mx))
    kh = (1.0 - lb) * _sigmoid(-fpre)
    return lf, kh


def _cumsum_rows(x):
    L, W = x.shape
    sub = lax.broadcasted_iota(jnp.int32, (SUBLANES, W), 0)
    out = []
    carry = None
    for r in range(L // SUBLANES):
        t = x[r * SUBLANES:(r + 1) * SUBLANES, :]
        s = 1
        while s < SUBLANES:
            t = t + jnp.where(sub >= s, pltpu.roll(t, s, 0), 0.0)
            s *= 2
        if carry is not None:
            t = t + carry
        carry = t[SUBLANES - 1:SUBLANES, :]
        out.append(t)
    return jnp.concatenate(out, axis=0)


def _mid_row_broadcast(bc, w):
    L, W = bc.shape
    if 2 * w >= SUBLANES:
        blocks = [jnp.broadcast_to(bc[b * 2 * w + w:b * 2 * w + w + 1, :], (2 * w, W)) for b in range(L // (2 * w))]
        return blocks[0] if len(blocks) == 1 else jnp.concatenate(blocks, axis=0)
    x3 = bc.reshape(L // SUBLANES, SUBLANES, W)
    sub = lax.broadcasted_iota(jnp.int32, x3.shape, 1)
    out = None
    for b in range(SUBLANES // (2 * w)):
        piece = jnp.broadcast_to(x3[:, b * 2 * w + w:b * 2 * w + w + 1, :], x3.shape)
        out = piece if out is None else jnp.where(sub >= b * 2 * w, piece, out)
    return out.reshape(L, W)


def _layer_norm_rows(r, g, b):
    mu = jnp.mean(r, axis=-1, keepdims=True)
    d = r - mu
    var = jnp.mean(d * d, axis=-1, keepdims=True)
    return d * lax.rsqrt(var + LN_EPS) * g + b


def _head_layer_norm_padded(h, valid):
    dh = M_DH
    hz = jnp.where(valid, h, 0.0)
    mu = jnp.sum(hz, axis=-1, keepdims=True) * (1.0 / dh)
    d = jnp.where(valid, h - mu, 0.0)
    var = jnp.sum(d * d, axis=-1, keepdims=True) * (1.0 / dh)
    return d * lax.rsqrt(var + HEAD_EPS)


def _pair_rms_norm(h, low):
    sq = h * h
    m0 = jnp.sum(jnp.where(low, sq, 0.0), axis=-1, keepdims=True) * (1.0 / H_DK)
    m1 = jnp.sum(jnp.where(low, 0.0, sq), axis=-1, keepdims=True) * (1.0 / H_DK)
    return h * jnp.where(low, lax.rsqrt(m0 + HEAD_EPS), lax.rsqrt(m1 + HEAD_EPS))


def _prefill_kernel(layer, TB,
                    x_ref, win_ref, wout_ref, cos_ref, sin_ref, vec_ref, hlb_ref, lv_ref,
                    y_ref, cout_ref, nout_ref, mout_ref, convout_ref, rout_ref, sout_ref,
                    p_ref, pre_ref, ho_ref, cum_ref, ml_ref, pm_ref, kw_ref, qt_ref, kt_ref, qi_ref, dh_ref,
                    u_ref, ur_ref, uh_ref, c_ref, r_ref, s_ref, m_ref, dm_ref, dec_ref,
                    xb_ref, it_ref, gr_ref, gw_ref, ge_ref):
    L = CHUNK
    NL = N_LEVELS
    n_chunks = TB // L
    j = pl.program_id(1)
    n_tb = pl.num_programs(1)

    @pl.when(j == 0)
    def _init():
        c_ref[...] = jnp.zeros_like(c_ref)
        r_ref[...] = jnp.zeros_like(r_ref)
        s_ref[...] = jnp.zeros_like(s_ref)
        m_ref[...] = jnp.zeros_like(m_ref)
        pre_ref[0:SUBLANES, :] = jnp.zeros((SUBLANES, 2 * HP), F32)
        ti = lax.broadcasted_iota(jnp.int32, (L, L), 0)
        si = lax.broadcasted_iota(jnp.int32, (L, L), 1)
        dist = (ti - si).astype(F32)
        row = lax.broadcasted_iota(jnp.int32, (L, LANES), 0).astype(F32)
        for h in range(N_HEADS):
            dm_ref[h] = jnp.where(ti >= si, jnp.exp(dist * LG[h]), 0.0)
            dec_ref[0:L, h * LANES:(h + 1) * LANES] = jnp.exp((row + 1.0) * LG[h])
            dec_ref[L:2 * L, h * LANES:(h + 1) * LANES] = jnp.exp((L - 1.0 - row) * LG[h])

    xb_ref[...] = x_ref[0].astype(BF16)
    lb = _hgrn_lower_bounds(hlb_ref[...], 0)[layer]
    PIECE = 2 * LANES

    def project(c0):
        c1 = min(c0 + PIECE, NP_IN)
        res = _dot_nt(xb_ref[...], win_ref[c0:c1, :])
        if c0 < C_MV:
            pre_ref[SUBLANES:SUBLANES + TB, c0:c1] = res
        else:
            p_ref[:, c0:c1] = res

    def gate_math(r0, n):
        rows = slice(r0, r0 + n)
        lf, kh = _hgrn_gates(p_ref[rows, C_HF:C_HF + H_WIDTH], lb)
        p_ref[rows, C_HF:C_HF + H_WIDTH] = lf
        p_ref[rows, C_HK:C_HK + H_WIDTH] = kh
        g = p_ref[rows, C_IF:C_IF + LANES] + vec_ref[V_BIF:V_BIF + 1, 0:LANES]
        p_ref[rows, C_IF:C_IF + LANES] = jnp.where(_lane((n, LANES)) < N_HEADS, g, _log_sigmoid(g))

    def conv_math(half, r0, n):
        cols = slice(half * HP, (half + 1) * HP)
        lo = SUBLANES + r0
        def tap(i):
            return jnp.tile(vec_ref[V_CONV8 + i * SUBLANES:V_CONV8 + (i + 1) * SUBLANES, cols], (n // SUBLANES, 1))

        ext = pre_ref[lo - SUBLANES:lo + n, cols]
        acc = tap(CONV_W) + tap(CONV_W - 1) * ext[SUBLANES:, :]
        for t in range(1, CONV_W):
            shifted = pltpu.roll(ext, t, 0)[SUBLANES:, :]
            acc = acc + tap(CONV_W - 1 - t) * shifted
        scale = 1.0 if half == 0 else M_DH ** -0.5
        p_ref[r0:r0 + n, C_MQ + half * HP:C_MQ + (half + 1) * HP] = _silu(acc) * scale

    def rope_math(col, scale, r0, n):
        rows = slice(r0, r0 + n)
        cos = cos_ref[rows, :]
        sin = sin_ref[rows, :]
        for h in range(N_HEADS):
            sl = slice(col + h * LANES, col + (h + 1) * LANES)
            v = p_ref[rows, sl]
            p_ref[rows, sl] = (v * cos + pltpu.roll(v, LANES // 2, 1) * sin) * scale

    def value_math(r0, n):
        one96 = jnp.where(_lane((n, LANES)) == M_DH, 1.0, 0.0)
        for h in range(N_HEADS):
            sl = slice(C_MV + h * LANES, C_MV + (h + 1) * LANES)
            p_ref[r0:r0 + n, sl] = p_ref[r0:r0 + n, sl] + one96

    order = ([C_HF, C_HF + 4 * H_WIDTH] + list(range(C_MQ, C_MV, PIECE)) + list(range(C_RQ, C_RV, PIECE))
             + list(range(C_MV, C_MO, PIECE)) + list(range(C_HF + PIECE, C_HF + 4 * H_WIDTH, PIECE))
             + list(range(C_MO, C_RQ, PIECE)) + list(range(C_RV, C_HF, PIECE)))
    assert sorted(order) == list(range(0, NP_IN, PIECE))
    pos = {c0: i for i, c0 in enumerate(order)}
    vec = []
    for r0 in range(0, TB, 64):
        vec.append((pos[C_HF + 4 * H_WIDTH], functools.partial(gate_math, r0, 64)))
    for half in range(2):
        for r0 in range(0, TB, 32):
            vec.append((pos[C_MQ + half * HP + HP - PIECE], functools.partial(conv_math, half, r0, 32)))
    for col, scale in ((C_RQ, 1.0), (C_RK, R_DH ** -0.5)):
        for r0 in range(0, TB, 64):
            vec.append((pos[col + HP - PIECE], functools.partial(rope_math, col, scale, r0, 64)))
    for r0 in range(0, TB, 128):
        vec.append((pos[C_MV + HP - PIECE], functools.partial(value_math, r0, 128)))
    per_piece = -(-len(vec) // (len(order) - 6))
    vi = 0
    for i, c0 in enumerate(order):
        project(c0)
        done = 0
        while vi < len(vec) and vec[vi][0] < i and done < per_piece:
            vec[vi][1]()
            vi += 1
            done += 1
    while vi < len(vec):
        vec[vi][1]()
        vi += 1
    pre_ref[SUBLANES - (CONV_W - 1):SUBLANES, :] = pre_ref[TB + SUBLANES - (CONV_W - 1):TB + SUBLANES, :]

    tril = lax.broadcasted_iota(jnp.int32, (L, L), 0) >= lax.broadcasted_iota(jnp.int32, (L, L), 1)
    bd = (lax.broadcasted_iota(jnp.int32, (LANES, LANES), 0) < H_DK) == (_lane((LANES, LANES)) < H_DK)
    low = _lane((L, LANES)) < H_DK
    lane = _lane((L, LANES))

    def intra(c):
        rows = slice(c * L, (c + 1) * L)

        ga = p_ref[rows, C_IF:C_IF + LANES]
        cum = _cumsum_rows(ga)
        bc = _cumsum_rows(p_ref[rows, C_HF:C_HF + H_WIDTH])

        def head(col, h, dt=None):
            a = p_ref[rows, col + h * LANES:col + (h + 1) * LANES]
            return a if dt is None else a.astype(dt)

        qh = p_ref[rows, C_HQ:C_HQ + H_WIDTH]
        khh = p_ref[rows, C_HK:C_HK + H_WIDTH]
        head_sel = [jnp.where((_lane((1, H_WIDTH)) & H_DK) == hh * H_DK, 1.0, 0.0).astype(BF16) for hh in range(2)]
        for n in range(NL + 1):
            if n < NL:
                fac = jnp.exp(-jnp.abs(bc - _mid_row_broadcast(bc, L >> (n + 1))))
                qn, kn = (qh * fac).astype(BF16), (khh * fac).astype(BF16)
            else:
                qn, kn = qh.astype(BF16), khh.astype(BF16)
            qt_ref[n, 0] = qn * head_sel[0]
            qt_ref[n, 1] = qn * head_sel[1]
            kt_ref[n] = kn
        e_pre = jnp.exp(bc)
        qi_ref[rows, :] = (qh * e_pre).astype(BF16)
        dh_ref[c:c + 1, :] = e_pre[L - 1:L, :]
        k_suf = khh * jnp.exp(bc[L - 1:L, :] - bc)

        for h in range(N_HEADS):
            v = head(C_RV, h, BF16)
            s = (_dot_nt(head(C_RQ, h, BF16), head(C_RK, h, BF16)) * dm_ref[h]).astype(BF16)
            ho_ref[rows, HP + h * LANES:HP + (h + 1) * LANES] = _dot(s, v)
            kd_t = (head(C_RK, h) * dec_ref[L:2 * L, h * LANES:(h + 1) * LANES]).T.astype(BF16)
            ur_ref[c, h] = _dot(kd_t, v)

        lv = lv_ref[...]
        heads = [(p, hh) for p in range(N_HEADS // 2) for hh in range(2)]

        def level_scores(p, hh):
            psl = slice(p * LANES, (p + 1) * LANES)
            return [_dot_nt(qt_ref[n, hh, :, psl], kt_ref[n, :, psl]) for n in range(NL + 1)]

        def fold(scores):
            s_mat = jnp.zeros((L, L), F32)
            for n, sc in enumerate(scores):
                s_mat = jnp.where(lv == n, sc, s_mat)
            return s_mat.astype(BF16)

        pending = level_scores(*heads[0])

        cum_ref[rows, :] = cum
        a_all = ga - pltpu.roll(cum, LANES - N_HEADS, 1)
        a_t = a_all.T
        ml = jnp.zeros((L, LANES), F32)
        for h in range(N_HEADS):
            b_col = cum[:, N_HEADS + h:N_HEADS + h + 1]
            log_d = jnp.where(tril, b_col + a_t[h:h + 1, :], -jnp.inf)
            m_loc = jnp.max(log_d, axis=1, keepdims=True)
            pm_ref[h] = (_dot_nt(head(C_MQ, h, BF16), head(C_MK, h, BF16)) * jnp.exp(log_d - m_loc)).astype(BF16)
            ml = jnp.where(lane == h, m_loc, ml)
            w_loc = jnp.exp(b_col[L - 1:L, :] + a_all[:, h:h + 1] - m_loc[L - 1:L, :])
            kw_ref[h] = (head(C_MK, h) * w_loc).T.astype(BF16)
        ml_ref[rows, :] = ml

        s_bf = []
        for i in range(len(heads)):
            nxt = level_scores(*heads[i + 1]) if i + 1 < len(heads) else None
            s_bf.append(fold(pending))
            pending = nxt
            if i == 0:
                for h in range(N_HEADS):
                    v = head(C_MV, h, BF16)
                    ho_ref[rows, h * LANES:(h + 1) * LANES] = _dot(pm_ref[h], v)
                    u_ref[c, h] = _dot(kw_ref[h], v)
        for p in range(N_HEADS // 2):
            psl = slice(p * LANES, (p + 1) * LANES)
            vp = p_ref[rows, C_HI + p * LANES:C_HI + (p + 1) * LANES]
            ho_ref[rows, 2 * HP + p * LANES:2 * HP + (p + 1) * LANES] = (
                _dot(s_bf[2 * p], jnp.where(low, vp, 0.0).astype(BF16))
                + _dot(s_bf[2 * p + 1], jnp.where(low, 0.0, vp).astype(BF16)))
            uh_ref[c, p] = jnp.where(bd, _dot(k_suf[:, psl].T.astype(BF16), vp.astype(BF16)), 0.0)

    def carry_states(c):
        rows = slice(c * L, (c + 1) * L)

        ml = ml_ref[rows, :]
        b_all = pltpu.roll(cum_ref[rows, :], LANES - N_HEADS, 1)
        m_prev = m_ref[0:1, :]
        m_t = jnp.maximum(ml, b_all + m_prev)
        r_all = jnp.exp(ml - m_t)
        w_all = jnp.exp(b_all + m_prev - m_t)
        e_all = jnp.exp(-m_t)
        for h in range(N_HEADS):
            sl = slice(h * LANES, (h + 1) * LANES)
            gr_ref[rows, sl] = jnp.broadcast_to(r_all[:, h:h + 1], (L, LANES))
            gw_ref[rows, sl] = jnp.broadcast_to(w_all[:, h:h + 1], (L, LANES))
            ge_ref[rows, sl] = jnp.broadcast_to(e_all[:, h:h + 1], (L, LANES))
        m_new = m_t[L - 1:L, :]
        dec_row = jnp.exp(b_all[L - 1:L, :] + m_prev - m_new)
        g_row = jnp.exp(ml[L - 1:L, :] - m_new)
        m_ref[0:1, :] = m_new

        for h in range(N_HEADS):
            q = p_ref[rows, C_MQ + h * LANES:C_MQ + (h + 1) * LANES].astype(BF16)
            it_ref[rows, h * LANES:(h + 1) * LANES] = _dot(q, c_ref[h].astype(BF16))
        for h in range(N_HEADS):
            q = p_ref[rows, C_RQ + h * LANES:C_RQ + (h + 1) * LANES]
            qd = (q * dec_ref[0:L, h * LANES:(h + 1) * LANES]).astype(BF16)
            it_ref[rows, HP + h * LANES:HP + (h + 1) * LANES] = _dot(qd, r_ref[h].astype(BF16))
        for p in range(N_HEADS // 2):
            psl = slice(p * LANES, (p + 1) * LANES)
            it_ref[rows, 2 * HP + p * LANES:2 * HP + (p + 1) * LANES] = _dot(qi_ref[rows, psl], s_ref[p].astype(BF16))

        d_col = dh_ref[c:c + 1, :].T
        for h in range(N_HEADS):
            c_ref[h] = dec_row[:, h:h + 1] * c_ref[h] + g_row[:, h:h + 1] * u_ref[c, h]
            r_ref[h] = math.exp(L * LG[h]) * r_ref[h] + ur_ref[c, h]
        for p in range(N_HEADS // 2):
            s_ref[p] = d_col[p * LANES:(p + 1) * LANES, :] * s_ref[p] + uh_ref[c, p]

    def mix_rows(rows, T):
        valid = _lane((T, LANES)) < M_DH
        low_t = _lane((T, LANES)) < H_DK
        parts = []
        for h in range(N_HEADS):
            sl = slice(h * LANES, (h + 1) * LANES)
            num = gr_ref[rows, sl] * ho_ref[rows, sl] + gw_ref[rows, sl] * it_ref[rows, sl]
            den = num[:, M_DH:M_DH + 1]
            hm = num / jnp.maximum(jnp.abs(den), ge_ref[rows, sl])
            o = p_ref[rows, C_MO + h * LANES:C_MO + (h + 1) * LANES]
            z = p_ref[rows, C_MZ + h * LANES:C_MZ + (h + 1) * LANES]
            gain = vec_ref[V_MNW:V_MNW + 1, sl]
            parts.append((_head_layer_norm_padded(hm, valid) * gain * _sigmoid(o) * _silu(z)).astype(BF16))
        for h in range(N_HEADS):
            sl = slice(HP + h * LANES, HP + (h + 1) * LANES)
            hn = _head_layer_norm_padded(ho_ref[rows, sl] + it_ref[rows, sl], valid)
            g = p_ref[rows, C_RG + h * LANES:C_RG + (h + 1) * LANES]
            parts.append((hn * vec_ref[V_RNW:V_RNW + 1, h * LANES:(h + 1) * LANES] * _silu(g)).astype(BF16))
        for p in range(N_HEADS // 2):
            sl = slice(2 * HP + p * LANES, 2 * HP + (p + 1) * LANES)
            hn = _pair_rms_norm(ho_ref[rows, sl] + it_ref[rows, sl], low_t)
            g = p_ref[rows, C_HG + p * LANES:C_HG + (p + 1) * LANES]
            parts.append((hn * vec_ref[V_HNW:V_HNW + 1, p * LANES:(p + 1) * LANES] * _silu(g)).astype(BF16))
        return jnp.concatenate(parts, axis=-1)

    def out_rows(rows, proj):
        y_ref[0, rows, :] = _layer_norm_rows(ALPHA * x_ref[0, rows, :] + proj,
                                             vec_ref[V_LNG:V_LNG + 1, :], vec_ref[V_LNB:V_LNB + 1, :])

    for c in range(n_chunks):
        intra(c)
        carry_states(c)
    half = TB // 2
    rows0, rows1 = slice(0, half), slice(half, TB)
    mix0 = mix_rows(rows0, half)
    mix1 = mix_rows(rows1, half)
    proj0 = _dot(mix0, wout_ref[...])
    proj1 = _dot(mix1, wout_ref[...])
    out_rows(rows0, proj0)
    out_rows(rows1, proj1)

    @pl.when(j == n_tb - 1)
    def _final_states():
        convout_ref[0] = pre_ref[SUBLANES - (CONV_W - 1):SUBLANES, :]
        for h in range(N_HEADS):
            c_aug = c_ref[h]
            cout_ref[0, h] = c_aug[0:M_DH, 0:M_DH]
            nout_ref[0, h:h + 1, :] = c_aug.T[M_DH:M_DH + 1, 0:M_DH]
            r_full = r_ref[h]
            rout_ref[0, h] = jnp.concatenate(
                [r_full[0:ROPE_HALF, 0:R_DH], r_full[LANES // 2:LANES // 2 + ROPE_HALF, 0:R_DH]], axis=0)
            p, hh = divmod(h, 2)
            sout_ref[0, h] = s_ref[p][hh * H_DK:(hh + 1) * H_DK, hh * H_DK:(hh + 1) * H_DK]
        mout_ref[0] = m_ref[0:1, 0:N_HEADS]


def _full_spec(shape):
    nd = len(shape)
    return pl.BlockSpec(shape, lambda *_: (0,) * nd)


def _prefill_layer(layer, x, win, wout, cos, sin, vec, hlb, lv, TB):
    B, T, _ = x.shape
    n_tb = T // TB
    n_chunks = TB // CHUNK
    kern = functools.partial(_prefill_kernel, layer, TB)

    def layer_spec(a, single_buffer=False):
        kw = dict(pipeline_mode=pl.Buffered(1)) if single_buffer else {}
        return pl.BlockSpec((None,) + a.shape[1:], lambda b, j: (layer,) + (0,) * (a.ndim - 1), **kw)

    in_specs = [
        pl.BlockSpec((1, TB, D_MODEL), lambda b, j: (b, j, 0)),
        layer_spec(win, True), layer_spec(wout, True),
        pl.BlockSpec((TB, LANES), lambda b, j: (j, 0)),
        pl.BlockSpec((TB, LANES), lambda b, j: (j, 0)),
        layer_spec(vec), _full_spec(hlb.shape), _full_spec(lv.shape),
    ]
    out_shape = (
        jax.ShapeDtypeStruct((B, T, D_MODEL), F32),
        jax.ShapeDtypeStruct((B, N_HEADS, M_DH, M_DH), F32),
        jax.ShapeDtypeStruct((B, N_HEADS, M_DH), F32),
        jax.ShapeDtypeStruct((B, 1, N_HEADS), F32),
        jax.ShapeDtypeStruct((B, CONV_W - 1, 2 * HP), F32),
        jax.ShapeDtypeStruct((B, N_HEADS, R_DH, R_DH), F32),
        jax.ShapeDtypeStruct((B, N_HEADS, H_DK, H_DK), F32),
    )
    out_specs = (
        pl.BlockSpec((1, TB, D_MODEL), lambda b, j: (b, j, 0)),
        pl.BlockSpec((1, N_HEADS, M_DH, M_DH), lambda b, j: (b, 0, 0, 0)),
        pl.BlockSpec((1, N_HEADS, M_DH), lambda b, j: (b, 0, 0)),
        pl.BlockSpec((1, 1, N_HEADS), lambda b, j: (b, 0, 0)),
        pl.BlockSpec((1, CONV_W - 1, 2 * HP), lambda b, j: (b, 0, 0)),
        pl.BlockSpec((1, N_HEADS, R_DH, R_DH), lambda b, j: (b, 0, 0, 0)),
        pl.BlockSpec((1, N_HEADS, H_DK, H_DK), lambda b, j: (b, 0, 0, 0)),
    )
    L = CHUNK
    scratch = [
        pltpu.VMEM((TB, NP_SCR), F32),
        pltpu.VMEM((TB + SUBLANES, 2 * HP), F32),
        pltpu.VMEM((TB, MIXP), F32),
        pltpu.VMEM((TB, LANES), F32),
        pltpu.VMEM((TB, LANES), F32),
        pltpu.VMEM((N_HEADS, L, L), BF16),
        pltpu.VMEM((N_HEADS, LANES, L), BF16),
        pltpu.VMEM((N_LEVELS + 1, 2, L, H_WIDTH), BF16),
        pltpu.VMEM((N_LEVELS + 1, L, H_WIDTH), BF16),
        pltpu.VMEM((TB, H_WIDTH), BF16),
        pltpu.VMEM((max(n_chunks, SUBLANES), H_WIDTH), F32),
        pltpu.VMEM((n_chunks, N_HEADS, LANES, LANES), F32),
        pltpu.VMEM((n_chunks, N_HEADS, LANES, LANES), F32),
        pltpu.VMEM((n_chunks, N_HEADS // 2, LANES, LANES), F32),
        pltpu.VMEM((N_HEADS, LANES, LANES), F32),
        pltpu.VMEM((N_HEADS, LANES, LANES), F32),
        pltpu.VMEM((N_HEADS // 2, LANES, LANES), F32),
        pltpu.VMEM((SUBLANES, LANES), F32),
        pltpu.VMEM((N_HEADS, L, L), F32),
        pltpu.VMEM((2 * L, HP), F32),
        pltpu.VMEM((TB, D_MODEL), BF16),
        pltpu.VMEM((TB, MIXP), F32),
        pltpu.VMEM((TB, HP), F32),
        pltpu.VMEM((TB, HP), F32),
        pltpu.VMEM((TB, HP), F32),
    ]
    return pl.pallas_call(
        kern, grid=(B, n_tb), in_specs=in_specs, out_specs=out_specs, out_shape=out_shape,
        scratch_shapes=scratch,
        compiler_params=pltpu.CompilerParams(dimension_semantics=("arbitrary", "arbitrary"),
                                             vmem_limit_bytes=VMEM_LIMIT),
        name=f"prefill_layer{layer}",
    )(x, win, wout, cos, sin, vec, hlb, lv)


DEC_KB = 4
KR_M = M_DH // DEC_KB
KR_H = H_DK // DEC_KB
G_DEC, G_WS, G_EN, G_GAM, G_SM, G_QN, G_SR, G_SH = (i * N_HEADS for i in range(8))


def _stream_state(q_blk, dec, kw_blk, v, in_ref, out_ref, nrows):
    acc = jnp.zeros_like(v)
    for kk in range(nrows):
        st = in_ref[0, 0, kk]
        d = dec if dec.shape[0] == 1 else dec[kk:kk + 1, :]
        acc = acc + q_blk[kk:kk + 1, :] * st
        out_ref[0, 0, kk] = d * st + kw_blk[kk:kk + 1, :] * v
    return acc


def _decode_kernel(x_ref, wt_ref, wout_ref, cos_ref, sin_ref, dvec_ref, dcol_ref, hlbt_ref, vec_ref,
                   cin_ref, nin_ref, min_ref, convin_ref, rin_ref, sin_st_ref,
                   y_ref, cout_ref, nout_ref, mout_ref, convout_ref, rout_ref, sout_ref,
                   p_ref, xs_ref, ht_ref, mix_ref, g_ref):
    l = pl.program_id(0)
    h = pl.program_id(1)
    kb = pl.program_id(2)
    first = jnp.logical_and(h == 0, kb == 0)
    last = jnp.logical_and(h == N_HEADS - 1, kb == DEC_KB - 1)
    B = x_ref.shape[0]

    @pl.when(jnp.logical_and(first, l == 0))
    def _load_x():
        xs_ref[...] = x_ref[...]

    @pl.when(first)
    def _project():
        p_ref[0:N_IN, :] = _dot_nt(wt_ref[0], xs_ref[...].astype(BF16))
        ht_ref[...] = jnp.zeros_like(ht_ref)

        pre = p_ref[O_MQ:O_MQ + 2 * M_WIDTH, :].T
        cw = dvec_ref[0, 0:CONV_W, :]
        acc = dvec_ref[0, CONV_W:CONV_W + 1, :] + cw[CONV_W - 1:CONV_W, :] * pre
        for jj in range(CONV_W - 1):
            acc = acc + cw[jj:jj + 1, :] * convin_ref[0, jj]
        for jj in range(CONV_W - 2):
            convout_ref[0, jj] = convin_ref[0, jj + 1]
        convout_ref[0, CONV_W - 2] = pre
        p_ref[O_MQ:O_MQ + 2 * M_WIDTH, :] = _silu(acc).T

        gt = p_ref[O_IF:O_IF + 2 * N_HEADS, :] + dcol_ref[0, D_MIX:D_MIX + 2 * N_HEADS, :]
        ig = gt[0:N_HEADS]
        lf = _log_sigmoid(gt[N_HEADS:2 * N_HEADS])
        m_prev = min_ref[0]
        m_t = jnp.maximum(ig, lf + m_prev)
        dec = jnp.exp(lf + m_prev - m_t)
        w_s = jnp.exp(ig - m_t)
        mout_ref[0] = m_t
        g_ref[G_DEC:G_DEC + N_HEADS, :] = dec
        g_ref[G_WS:G_WS + N_HEADS, :] = w_s
        g_ref[G_EN:G_EN + N_HEADS, :] = jnp.exp(-m_t)
        for hh in range(N_HEADS):
            q = p_ref[O_MQ + hh * M_DH:O_MQ + (hh + 1) * M_DH, :]
            k = p_ref[O_MK + hh * M_DH:O_MK + (hh + 1) * M_DH, :] * (M_DH ** -0.5)
            n_old = nin_ref[0, hh]
            d_h = dec[hh:hh + 1]
            kw = k * w_s[hh:hh + 1]
            g_ref[G_SM + hh:G_SM + hh + 1, :] = jnp.sum(q * k, axis=0, keepdims=True) * w_s[hh:hh + 1]
            g_ref[G_QN + hh:G_QN + hh + 1, :] = jnp.sum(q * n_old, axis=0, keepdims=True)
            nout_ref[0, hh] = d_h * n_old + kw
            p_ref[O_MK + hh * M_DH:O_MK + (hh + 1) * M_DH, :] = kw

        cos = cos_ref[...]
        sin = sin_ref[...]
        for hh in range(N_HEADS):
            g_ref[G_GAM + hh:G_GAM + hh + 1, :] = jnp.full((1, B), math.exp(LG[hh]), F32)
            rot = []
            for off, scale in ((O_RQ, 1.0), (O_RK, R_DH ** -0.5)):
                x1 = p_ref[off + hh * R_DH:off + hh * R_DH + ROPE_HALF, :]
                x2 = p_ref[off + hh * R_DH + ROPE_HALF:off + (hh + 1) * R_DH, :]
                r1 = (x1 * cos - x2 * sin) * scale
                r2 = (x1 * sin + x2 * cos) * scale
                p_ref[off + hh * R_DH:off + hh * R_DH + ROPE_HALF, :] = r1
                p_ref[off + hh * R_DH + ROPE_HALF:off + (hh + 1) * R_DH, :] = r2
                rot.append((r1, r2))
            (q1, q2), (k1, k2) = rot
            g_ref[G_SR + hh:G_SR + hh + 1, :] = (jnp.sum(q1 * k1, axis=0, keepdims=True)
                                                 + jnp.sum(q2 * k2, axis=0, keepdims=True))

        lbs = _hgrn_lower_bounds(hlbt_ref[...], 1)
        lb = lbs[0]
        for i in range(1, DEPTH):
            lb = jnp.where(l == i, lbs[i], lb)
        lfh, kh = _hgrn_gates(p_ref[O_HF:O_HF + H_WIDTH, :], lb)
        eh = jnp.exp(lfh)
        qh = p_ref[O_HQ:O_HQ + H_WIDTH, :]
        p_ref[O_HF:O_HF + H_WIDTH, :] = eh
        p_ref[O_HK:O_HK + H_WIDTH, :] = kh
        p_ref[O_HQ:O_HQ + H_WIDTH, :] = qh * eh
        qk = qh * kh
        for hh in range(N_HEADS):
            g_ref[G_SH + hh:G_SH + hh + 1, :] = jnp.sum(qk[hh * H_DK:(hh + 1) * H_DK], axis=0, keepdims=True)

    def rows(base, width, n):
        return pl.ds(pl.multiple_of(base + h * width + kb * n, SUBLANES), n)

    dec = g_ref[pl.ds(G_DEC + h, 1), :]
    acc = _stream_state(p_ref[rows(O_MQ, M_DH, KR_M), :], dec, p_ref[rows(O_MK, M_DH, KR_M), :],
                        p_ref[pl.ds(pl.multiple_of(O_MV + h * M_DH, SUBLANES), M_DH), :], cin_ref, cout_ref, KR_M)
    ht_ref[pl.ds(pl.multiple_of(h * M_DH, SUBLANES), M_DH), :] += acc

    gam = g_ref[pl.ds(G_GAM + h, 1), :]
    acc = _stream_state(p_ref[rows(O_RQ, R_DH, KR_M), :], gam, p_ref[rows(O_RK, R_DH, KR_M), :],
                        p_ref[pl.ds(pl.multiple_of(O_RV + h * R_DH, SUBLANES), R_DH), :], rin_ref, rout_ref, KR_M)
    ht_ref[pl.ds(pl.multiple_of(M_WIDTH + h * R_DH, SUBLANES), R_DH), :] += acc

    acc = _stream_state(p_ref[rows(O_HQ, H_DK, KR_H), :], p_ref[rows(O_HF, H_DK, KR_H), :],
                        p_ref[rows(O_HK, H_DK, KR_H), :],
                        p_ref[pl.ds(pl.multiple_of(O_HI + h * H_DK, SUBLANES), H_DK), :], sin_st_ref, sout_ref, KR_H)
    ht_ref[pl.ds(pl.multiple_of(M_WIDTH + R_WIDTH + h * H_DK, SUBLANES), H_DK), :] += acc

    @pl.when(last)
    def _finish():
        def head_ln(a):
            mu = jnp.mean(a, axis=0, keepdims=True)
            d = a - mu
            return d * lax.rsqrt(jnp.mean(d * d, axis=0, keepdims=True) + HEAD_EPS)

        for hh in range(N_HEADS):
            sl = slice(hh * M_DH, (hh + 1) * M_DH)
            v = p_ref[O_MV + hh * M_DH:O_MV + (hh + 1) * M_DH, :]
            d_h = g_ref[G_DEC + hh:G_DEC + hh + 1, :]
            s = g_ref[G_SM + hh:G_SM + hh + 1, :]
            num = s * v + ht_ref[sl, :] * d_h
            den = s + g_ref[G_QN + hh:G_QN + hh + 1, :] * d_h
            hm = num / jnp.maximum(jnp.abs(den), g_ref[G_EN + hh:G_EN + hh + 1, :])
            o = p_ref[O_MO + hh * M_DH:O_MO + (hh + 1) * M_DH, :]
            z = p_ref[O_MZ + hh * M_DH:O_MZ + (hh + 1) * M_DH, :]
            mix_ref[sl, :] = head_ln(hm) * dcol_ref[0, sl, :] * _sigmoid(o) * _silu(z)
        for hh in range(N_HEADS):
            sl = slice(M_WIDTH + hh * R_DH, M_WIDTH + (hh + 1) * R_DH)
            v = p_ref[O_RV + hh * R_DH:O_RV + (hh + 1) * R_DH, :]
            hr = g_ref[G_SR + hh:G_SR + hh + 1, :] * v + ht_ref[sl, :] * math.exp(LG[hh])
            gg = p_ref[O_RG + hh * R_DH:O_RG + (hh + 1) * R_DH, :]
            mix_ref[sl, :] = head_ln(hr) * dcol_ref[0, sl, :] * _silu(gg)
        for hh in range(N_HEADS):
            sl = slice(M_WIDTH + R_WIDTH + hh * H_DK, M_WIDTH + R_WIDTH + (hh + 1) * H_DK)
            v = p_ref[O_HI + hh * H_DK:O_HI + (hh + 1) * H_DK, :]
            ho = g_ref[G_SH + hh:G_SH + hh + 1, :] * v + ht_ref[sl, :]
            hn = ho * lax.rsqrt(jnp.mean(ho * ho, axis=0, keepdims=True) + HEAD_EPS)
            gg = p_ref[O_HG + hh * H_DK:O_HG + (hh + 1) * H_DK, :]
            mix_ref[sl, :] = hn * dcol_ref[0, sl, :] * _silu(gg)
        x = xs_ref[...]
        out = _dot(mix_ref[...].T.astype(BF16), wout_ref[0])
        y = _layer_norm_rows(ALPHA * x + out, vec_ref[0, V_LNG:V_LNG + 1, :], vec_ref[0, V_LNB:V_LNB + 1, :])
        xs_ref[...] = y
        y_ref[0] = y


def _decode(x, wt, wout, cos, sin, dvec, dcol, hlbt, vec, st_c, st_n, st_m, st_conv, st_r, st_s):
    B = x.shape[0]

    def lspec(a):
        nd = a.ndim
        return pl.BlockSpec((1,) + a.shape[1:], lambda l, h, kb: (l,) + (0,) * (nd - 1))

    def kv_spec(a, n):
        return pl.BlockSpec((1, 1, n) + a.shape[3:], lambda l, h, kb: (l, h, kb, 0, 0))

    in_specs = [
        _full_spec(x.shape), lspec(wt), lspec(wout), _full_spec(cos.shape), _full_spec(sin.shape),
        lspec(dvec), lspec(dcol), _full_spec(hlbt.shape), lspec(vec),
        kv_spec(st_c, KR_M), lspec(st_n), lspec(st_m), lspec(st_conv), kv_spec(st_r, KR_M), kv_spec(st_s, KR_H),
    ]
    out_shape = (
        jax.ShapeDtypeStruct((DEPTH, B, D_MODEL), F32),
        jax.ShapeDtypeStruct(st_c.shape, F32), jax.ShapeDtypeStruct(st_n.shape, F32),
        jax.ShapeDtypeStruct(st_m.shape, F32), jax.ShapeDtypeStruct(st_conv.shape, F32),
        jax.ShapeDtypeStruct(st_r.shape, F32), jax.ShapeDtypeStruct(st_s.shape, F32),
    )
    out_specs = (
        pl.BlockSpec((1, B, D_MODEL), lambda l, h, kb: (l, 0, 0)),
        kv_spec(st_c, KR_M), lspec(st_n), lspec(st_m), lspec(st_conv), kv_spec(st_r, KR_M), kv_spec(st_s, KR_H),
    )
    scratch = [
        pltpu.VMEM((N_IN + H_WIDTH, B), F32),
        pltpu.VMEM((B, D_MODEL), F32),
        pltpu.VMEM((D_MIX, B), F32),
        pltpu.VMEM((D_MIX, B), F32),
        pltpu.VMEM((8 * N_HEADS, B), F32),
    ]
    return pl.pallas_call(
        _decode_kernel, grid=(DEPTH, N_HEADS, DEC_KB), in_specs=in_specs, out_specs=out_specs,
        out_shape=out_shape, scratch_shapes=scratch,
        compiler_params=pltpu.CompilerParams(dimension_semantics=("arbitrary", "arbitrary", "arbitrary"),
                                             vmem_limit_bytes=VMEM_LIMIT),
        name="decode_step",
    )(x, wt, wout, cos, sin, dvec, dcol, hlbt, vec, st_c, st_n, st_m, st_conv, st_r, st_s)


def _rope_angles(pos):
    inv = ROPE_BASE ** (-jnp.arange(ROPE_HALF, dtype=F32) / ROPE_HALF)
    ang = pos.astype(F32)[:, None] * inv[None, :]
    return jnp.cos(ang), jnp.sin(ang)


PREFILL_BLOCK = 512


def kernel(x_prompt, x_sample, state_mlstm_C, state_mlstm_n, state_mlstm_m, state_mlstm_conv, state_ret, state_hgrn, w_in, conv_w, conv_b, b_mgate, m_norm_w, r_norm_w, h_norm_w, hgrn_lb, w_out, ln_g, ln_b):
    B, T, _ = x_prompt.shape

    wt = jnp.swapaxes(w_in, 1, 2).astype(BF16)
    win_p = jnp.concatenate(
        [_pad_heads(wt[:, O_MQ:O_IF], 1),
         _pad_rope_heads(wt[:, O_RQ:O_RV], 1), _pad_heads(wt[:, O_RV:O_HF], 1),
         wt[:, O_HF:N_IN],
         jnp.pad(wt[:, O_IF:O_RQ], ((0, 0), (0, LANES - 2 * N_HEADS), (0, 0)))], axis=1)
    wout_b = w_out.astype(BF16)
    wout_p = jnp.concatenate([_pad_heads(wout_b[:, 0:M_WIDTH + R_WIDTH], 1), wout_b[:, M_WIDTH + R_WIDTH:]],
                             axis=1)

    def row(a):
        return jnp.pad(a, ((0, 0), (0, D_MODEL - a.shape[-1])))[:, None, :]

    vec = jnp.concatenate([_pad_heads(conv_w, 2), row(_pad_heads(conv_b, 1)), row(b_mgate),
                           row(_pad_heads(m_norm_w, 1)), row(_pad_heads(r_norm_w, 1)), row(h_norm_w),
                           row(ln_g), row(ln_b)], axis=1)
    vec = jnp.pad(vec, ((0, 0), (0, V_ROWS - vec.shape[1]), (0, 0)))
    vec = jnp.concatenate([vec, jnp.repeat(vec[:, V_CONVW:V_CONVB + 1], SUBLANES, axis=1)], axis=1)
    dvec = jnp.pad(jnp.concatenate([conv_w, conv_b[:, None, :]], axis=1),
                   ((0, 0), (0, SUBLANES - CONV_W - 1), (0, 0)))
    dcol = jnp.concatenate([m_norm_w, r_norm_w, h_norm_w, b_mgate], axis=1)[:, :, None]
    hlb = hgrn_lb.astype(F32)

    lv = jnp.asarray(_level_table(CHUNK))

    c, s = _rope_angles(jnp.arange(T, dtype=jnp.int32))
    z = jnp.zeros((T, LANES // 2 - ROPE_HALF), F32)
    cos_p = jnp.concatenate([c, z, c, z], axis=1)
    sin_p = jnp.concatenate([-s, z, s, z], axis=1)

    hp = x_prompt
    st = [[] for _ in range(6)]
    for l in range(DEPTH):
        outs = _prefill_layer(l, hp, win_p, wout_p, cos_p, sin_p, vec, hlb, lv, PREFILL_BLOCK)
        hp = outs[0]
        for k in range(6):
            st[k].append(outs[1 + k])
    mC_p = jnp.stack(st[0])
    mn_p = jnp.stack(st[1])
    mm_p = jnp.stack(st[2])[:, :, 0, :]
    conv_p = _unpad_heads(jnp.stack(st[3]))
    ret_p = jnp.stack(st[4])
    hgrn_p = jnp.stack(st[5])

    n_s = x_sample.shape[0]
    cs, ss = _rope_angles(PAST_LEN + jnp.arange(x_sample.shape[1], dtype=jnp.int32))
    cos_s = jnp.broadcast_to(cs[0][:, None], (ROPE_HALF, n_s))
    sin_s = jnp.broadcast_to(ss[0][:, None], (ROPE_HALF, n_s))
    ys, c_t, n_t, m_t, conv_t, r_t, s_t = _decode(
        x_sample[:, 0, :], wt, wout_b, cos_s, sin_s, dvec, dcol, hlb.T, vec,
        jnp.transpose(state_mlstm_C, (0, 2, 3, 4, 1)), jnp.transpose(state_mlstm_n, (0, 2, 3, 1)),
        jnp.transpose(state_mlstm_m, (0, 2, 1)), jnp.transpose(state_mlstm_conv, (0, 2, 1, 3)),
        jnp.transpose(state_ret, (0, 2, 3, 4, 1)), jnp.transpose(state_hgrn, (0, 2, 3, 4, 1)))
    hs = ys[DEPTH - 1][:, None, :]
    mC_s = jnp.transpose(c_t, (0, 4, 1, 2, 3))
    mn_s = jnp.transpose(n_t, (0, 3, 1, 2))
    mm_s = jnp.transpose(m_t, (0, 2, 1))
    conv_s = jnp.transpose(conv_t, (0, 2, 1, 3))
    ret_s = jnp.transpose(r_t, (0, 4, 1, 2, 3))
    hgrn_s = jnp.transpose(s_t, (0, 4, 1, 2, 3))

    return (hp, hs, mC_p, mn_p, mm_p, conv_p, ret_p, hgrn_p, mC_s, mn_s, mm_s, conv_s, ret_s, hgrn_s)
```

```python
import functools
import math

import numpy as np
import jax
import jax.numpy as jnp
from jax import lax
from jax.experimental import pallas as pl
from jax.experimental.pallas import tpu as pltpu

F32 = jnp.float32
BF16 = jnp.bfloat16

D_MODEL = 1024
DEPTH = 2
PAST_LEN = 16384
N_HEADS = 4
M_DH = 96
R_DH = 96
H_DK = 64
M_WIDTH = N_HEADS * M_DH
R_WIDTH = N_HEADS * R_DH
H_WIDTH = N_HEADS * H_DK
D_MIX = M_WIDTH + R_WIDTH + H_WIDTH
CONV_W = 4
ROPE_BASE = 10000.0
LN_EPS = 1e-5
HEAD_EPS = 1e-6
ALPHA = (2 * DEPTH) ** 0.25

LANES = 128
SUBLANES = 8
HP = N_HEADS * LANES
ROPE_HALF = R_DH // 2
VMEM_LIMIT = 56 * 1024 * 1024

O_MQ, O_MK, O_MV, O_MO, O_MZ = 0, M_WIDTH, 2 * M_WIDTH, 3 * M_WIDTH, 4 * M_WIDTH
O_IF = 5 * M_WIDTH
O_RQ = O_IF + 2 * N_HEADS
O_RK, O_RV, O_RG = O_RQ + R_WIDTH, O_RQ + 2 * R_WIDTH, O_RQ + 3 * R_WIDTH
O_HF = O_RQ + 4 * R_WIDTH
O_HI, O_HQ, O_HG = O_HF + H_WIDTH, O_HF + 2 * H_WIDTH, O_HF + 3 * H_WIDTH
N_IN = O_HF + 4 * H_WIDTH
O_HK = N_IN

C_MQ, C_MK, C_MV, C_MO, C_MZ = 0, HP, 2 * HP, 3 * HP, 4 * HP
C_RQ, C_RK, C_RV, C_RG = 5 * HP, 6 * HP, 7 * HP, 8 * HP
C_HF = 9 * HP
C_HI = C_HF + H_WIDTH
C_HQ = C_HI + H_WIDTH
C_HG = C_HQ + H_WIDTH
C_IF = C_HG + H_WIDTH
NP_IN = C_IF + LANES
C_HK = NP_IN
NP_SCR = NP_IN + H_WIDTH
MIXP = 2 * HP + H_WIDTH

V_CONVW, V_CONVB, V_BIF, V_MNW, V_RNW, V_HNW, V_LNG, V_LNB = 0, CONV_W, CONV_W + 1, CONV_W + 2, CONV_W + 3, CONV_W + 4, CONV_W + 5, CONV_W + 6
V_ROWS = 2 * SUBLANES
V_CONV8 = V_ROWS
V_ROWS8 = V_ROWS + (CONV_W + 1) * SUBLANES

CHUNK = 128
N_LEVELS = int(math.log2(CHUNK))
LG = [math.log1p(-2.0 ** (-5.0 - h)) for h in range(N_HEADS)]


def _pad_groups(a, axis, group, padded):
    shp = a.shape
    n = shp[axis] // group
    a = a.reshape(shp[:axis] + (n, group) + shp[axis + 1:])
    pad = [(0, 0)] * a.ndim
    pad[axis + 1] = (0, padded - group)
    return jnp.pad(a, pad).reshape(shp[:axis] + (n * padded,) + shp[axis + 1:])


def _pad_heads(a, axis):
    return _pad_groups(a, axis, M_DH, LANES)


def _pad_rope_heads(a, axis):
    return _pad_groups(a, axis, ROPE_HALF, LANES // 2)


def _unpad_heads(a, dh=M_DH):
    shp = a.shape
    return a.reshape(shp[:-1] + (shp[-1] // LANES, LANES))[..., :dh].reshape(shp[:-1] + (shp[-1] // LANES * dh,))


def _level_table(L):
    nl = int(math.log2(L))
    tt = np.arange(L)[:, None]
    ss = np.arange(L)[None, :]
    lv = np.full((L, L), -1, np.int32)
    for n in range(nl):
        w = L >> (n + 1)
        same = (tt // (2 * w)) == (ss // (2 * w))
        cond = same & ((tt % (2 * w)) >= w) & ((ss % (2 * w)) < w)
        lv = np.where(cond, n, lv)
    lv = np.where(tt == ss, nl, lv)
    return lv.astype(np.int32)


def _dot(a, b):
    return jnp.dot(a, b, preferred_element_type=F32)


def _dot_nt(a, b):
    return lax.dot_general(a, b, (((1,), (1,)), ((), ())), preferred_element_type=F32)


NEG_LOG2E = -1.4426950408889634


def _sigmoid(x):
    return 1.0 / (1.0 + jnp.exp2(x * NEG_LOG2E))


def _silu(x):
    return x * _sigmoid(x)


def _log_sigmoid(x):
    return jnp.minimum(x, 0.0) - jnp.log(1.0 + jnp.exp2(jnp.abs(x) * NEG_LOG2E))


def _lane(shape):
    return lax.broadcasted_iota(jnp.int32, shape, len(shape) - 1)


def _hgrn_lower_bounds(hlb, layer_axis):
    def take(a, l):
        return a[l:l + 1] if layer_axis == 0 else a[:, l:l + 1]
    mx = jnp.max(hlb, axis=layer_axis, keepdims=True)
    e = jnp.exp(hlb - mx)
    p = e / jnp.sum(e, axis=layer_axis, keepdims=True)
    out = []
    acc = None
    for l in range(DEPTH):
        acc = take(p, l) if acc is None else acc + take(p, l)
        out.append(acc - take(p, 0))
    return out


def _hgrn_gates(fpre, lb):
    a = jnp.log(lb)
    c = jnp.log1p(-lb) + _log_sigmoid(fpre)
    mx = jnp.maximum(a, c)
    lf = mx + jnp.log(jnp.exp(a - mx) + jnp.exp(c - mx))
    kh = (1.0 - lb) * _sigmoid(-fpre)
    return lf, kh


def _cumsum_rows(x):
    L, W = x.shape
    sub = lax.broadcasted_iota(jnp.int32, (SUBLANES, W), 0)
    out = []
    carry = None
    for r in range(L // SUBLANES):
        t = x[r * SUBLANES:(r + 1) * SUBLANES, :]
        s = 1
        while s < SUBLANES:
            t = t + jnp.where(sub >= s, pltpu.roll(t, s, 0), 0.0)
            s *= 2
        if carry is not None:
            t = t + carry
        carry = t[SUBLANES - 1:SUBLANES, :]
        out.append(t)
    return jnp.concatenate(out, axis=0)


def _mid_row_broadcast(bc, w):
    L, W = bc.shape
    if 2 * w >= SUBLANES:
        blocks = [jnp.broadcast_to(bc[b * 2 * w + w:b * 2 * w + w + 1, :], (2 * w, W)) for b in range(L // (2 * w))]
        return blocks[0] if len(blocks) == 1 else jnp.concatenate(blocks, axis=0)
    x3 = bc.reshape(L // SUBLANES, SUBLANES, W)
    sub = lax.broadcasted_iota(jnp.int32, x3.shape, 1)
    out = None
    for b in range(SUBLANES // (2 * w)):
        piece = jnp.broadcast_to(x3[:, b * 2 * w + w:b * 2 * w + w + 1, :], x3.shape)
        out = piece if out is None else jnp.where(sub >= b * 2 * w, piece, out)
    return out.reshape(L, W)


def _layer_norm_rows(r, g, b):
    mu = jnp.mean(r, axis=-1, keepdims=True)
    d = r - mu
    var = jnp.mean(d * d, axis=-1, keepdims=True)
    return d * lax.rsqrt(var + LN_EPS) * g + b


def _head_layer_norm_padded(h, valid):
    dh = M_DH
    hz = jnp.where(valid, h, 0.0)
    mu = jnp.sum(hz, axis=-1, keepdims=True) * (1.0 / dh)
    d = jnp.where(valid, h - mu, 0.0)
    var = jnp.sum(d * d, axis=-1, keepdims=True) * (1.0 / dh)
    return d * lax.rsqrt(var + HEAD_EPS)


def _pair_rms_norm(h, low):
    sq = h * h
    m0 = jnp.sum(jnp.where(low, sq, 0.0), axis=-1, keepdims=True) * (1.0 / H_DK)
    m1 = jnp.sum(jnp.where(low, 0.0, sq), axis=-1, keepdims=True) * (1.0 / H_DK)
    return h * jnp.where(low, lax.rsqrt(m0 + HEAD_EPS), lax.rsqrt(m1 + HEAD_EPS))


def _prefill_kernel(layer, TB,
                    x_ref, win_ref, wout_ref, cos_ref, sin_ref, vec_ref, hlb_ref, lv_ref,
                    y_ref, cout_ref, nout_ref, mout_ref, convout_ref, rout_ref, sout_ref,
                    p_ref, pre_ref, ho_ref, cum_ref, ml_ref, pm_ref, kw_ref, qt_ref, kt_ref, qi_ref, dh_ref,
                    u_ref, ur_ref, uh_ref, c_ref, r_ref, s_ref, m_ref, dm_ref, dec_ref,
                    xb_ref, it_ref, gr_ref, gw_ref, ge_ref):
    L = CHUNK
    NL = N_LEVELS
    n_chunks = TB // L
    j = pl.program_id(1)
    n_tb = pl.num_programs(1)

    @pl.when(j == 0)
    def _init():
        c_ref[...] = jnp.zeros_like(c_ref)
        r_ref[...] = jnp.zeros_like(r_ref)
        s_ref[...] = jnp.zeros_like(s_ref)
        m_ref[...] = jnp.zeros_like(m_ref)
        pre_ref[0:SUBLANES, :] = jnp.zeros((SUBLANES, 2 * HP), F32)
        ti = lax.broadcasted_iota(jnp.int32, (L, L), 0)
        si = lax.broadcasted_iota(jnp.int32, (L, L), 1)
        dist = (ti - si).astype(F32)
        row = lax.broadcasted_iota(jnp.int32, (L, LANES), 0).astype(F32)
        for h in range(N_HEADS):
            dm_ref[h] = jnp.where(ti >= si, jnp.exp(dist * LG[h]), 0.0)
            dec_ref[0:L, h * LANES:(h + 1) * LANES] = jnp.exp((row + 1.0) * LG[h])
            dec_ref[L:2 * L, h * LANES:(h + 1) * LANES] = jnp.exp((L - 1.0 - row) * LG[h])

    xb_ref[...] = x_ref[0].astype(BF16)
    lb = _hgrn_lower_bounds(hlb_ref[...], 0)[layer]
    PIECE = 2 * LANES

    def project(c0):
        c1 = min(c0 + PIECE, NP_IN)
        res = _dot_nt(xb_ref[...], win_ref[c0:c1, :])
        if c0 < C_MV:
            pre_ref[SUBLANES:SUBLANES + TB, c0:c1] = res
        else:
            p_ref[:, c0:c1] = res

    def gate_math(r0, n):
        rows = slice(r0, r0 + n)
        lf, kh = _hgrn_gates(p_ref[rows, C_HF:C_HF + H_WIDTH], lb)
        p_ref[rows, C_HF:C_HF + H_WIDTH] = lf
        p_ref[rows, C_HK:C_HK + H_WIDTH] = kh
        g = p_ref[rows, C_IF:C_IF + LANES] + vec_ref[V_BIF:V_BIF + 1, 0:LANES]
        p_ref[rows, C_IF:C_IF + LANES] = jnp.where(_lane((n, LANES)) < N_HEADS, g, _log_sigmoid(g))

    def conv_math(half, r0, n):
        cols = slice(half * HP, (half + 1) * HP)
        lo = SUBLANES + r0
        def tap(i):
            return jnp.tile(vec_ref[V_CONV8 + i * SUBLANES:V_CONV8 + (i + 1) * SUBLANES, cols], (n // SUBLANES, 1))

        ext = pre_ref[lo - SUBLANES:lo + n, cols]
        acc = tap(CONV_W) + tap(CONV_W - 1) * ext[SUBLANES:, :]
        for t in range(1, CONV_W):
            shifted = pltpu.roll(ext, t, 0)[SUBLANES:, :]
            acc = acc + tap(CONV_W - 1 - t) * shifted
        scale = 1.0 if half == 0 else M_DH ** -0.5
        p_ref[r0:r0 + n, C_MQ + half * HP:C_MQ + (half + 1) * HP] = _silu(acc) * scale

    def rope_math(col, scale, r0, n):
        rows = slice(r0, r0 + n)
        cos = cos_ref[rows, :]
        sin = sin_ref[rows, :]
        for h in range(N_HEADS):
            sl = slice(col + h * LANES, col + (h + 1) * LANES)
            v = p_ref[rows, sl]
            p_ref[rows, sl] = (v * cos + pltpu.roll(v, LANES // 2, 1) * sin) * scale

    def value_math(r0, n):
        one96 = jnp.where(_lane((n, LANES)) == M_DH, 1.0, 0.0)
        for h in range(N_HEADS):
            sl = slice(C_MV + h * LANES, C_MV + (h + 1) * LANES)
            p_ref[r0:r0 + n, sl] = p_ref[r0:r0 + n, sl] + one96

    order = ([C_HF, C_HF + 4 * H_WIDTH] + list(range(C_MQ, C_MV, PIECE)) + list(range(C_RQ, C_RV, PIECE))
             + list(range(C_MV, C_MO, PIECE)) + list(range(C_HF + PIECE, C_HF + 4 * H_WIDTH, PIECE))
             + list(range(C_MO, C_RQ, PIECE)) + list(range(C_RV, C_HF, PIECE)))
    assert sorted(order) == list(range(0, NP_IN, PIECE))
    pos = {c0: i for i, c0 in enumerate(order)}
    vec = []
    for r0 in range(0, TB, 64):
        vec.append((pos[C_HF + 4 * H_WIDTH], functools.partial(gate_math, r0, 64)))
    for half in range(2):
        for r0 in range(0, TB, 32):
            vec.append((pos[C_MQ + half * HP + HP - PIECE], functools.partial(conv_math, half, r0, 32)))
    for col, scale in ((C_RQ, 1.0), (C_RK, R_DH ** -0.5)):
        for r0 in range(0, TB, 64):
            vec.append((pos[col + HP - PIECE], functools.partial(rope_math, col, scale, r0, 64)))
    for r0 in range(0, TB, 128):
        vec.append((pos[C_MV + HP - PIECE], functools.partial(value_math, r0, 128)))
    per_piece = -(-len(vec) // (len(order) - 6))
    vi = 0
    for i, c0 in enumerate(order):
        project(c0)
        done = 0
        while vi < len(vec) and vec[vi][0] < i and done < per_piece:
            vec[vi][1]()
            vi += 1
            done += 1
    while vi < len(vec):
        vec[vi][1]()
        vi += 1
    pre_ref[SUBLANES - (CONV_W - 1):SUBLANES, :] = pre_ref[TB + SUBLANES - (CONV_W - 1):TB + SUBLANES, :]

    tril = lax.broadcasted_iota(jnp.int32, (L, L), 0) >= lax.broadcasted_iota(jnp.int32, (L, L), 1)
    bd = (lax.broadcasted_iota(jnp.int32, (LANES, LANES), 0) < H_DK) == (_lane((LANES, LANES)) < H_DK)
    low = _lane((L, LANES)) < H_DK
    lane = _lane((L, LANES))

    def intra(c):
        rows = slice(c * L, (c + 1) * L)

        ga = p_ref[rows, C_IF:C_IF + LANES]
        cum = _cumsum_rows(ga)
        bc = _cumsum_rows(p_ref[rows, C_HF:C_HF + H_WIDTH])

        def head(col, h, dt=None):
            a = p_ref[rows, col + h * LANES:col + (h + 1) * LANES]
            return a if dt is None else a.astype(dt)

        qh = p_ref[rows, C_HQ:C_HQ + H_WIDTH]
        khh = p_ref[rows, C_HK:C_HK + H_WIDTH]
        head_sel = [jnp.where((_lane((1, H_WIDTH)) & H_DK) == hh * H_DK, 1.0, 0.0).astype(BF16) for hh in range(2)]
        for n in range(NL + 1):
            if n < NL:
                fac = jnp.exp2(jnp.abs(bc - _mid_row_broadcast(bc, L >> (n + 1))) * NEG_LOG2E)
                qn, kn = (qh * fac).astype(BF16), (khh * fac).astype(BF16)
            else:
                qn, kn = qh.astype(BF16), khh.astype(BF16)
            qt_ref[n, 0] = qn * head_sel[0]
            qt_ref[n, 1] = qn * head_sel[1]
            kt_ref[n] = kn
        e_pre = jnp.exp(bc)
        qi_ref[rows, :] = (qh * e_pre).astype(BF16)
        dh_ref[c:c + 1, :] = e_pre[L - 1:L, :]
        k_suf = khh * jnp.exp(bc[L - 1:L, :] - bc)

        for h in range(N_HEADS):
            v = head(C_RV, h, BF16)
            s = (_dot_nt(head(C_RQ, h, BF16), head(C_RK, h, BF16)) * dm_ref[h]).astype(BF16)
            ho_ref[rows, HP + h * LANES:HP + (h + 1) * LANES] = _dot(s, v)
            kd_t = (head(C_RK, h) * dec_ref[L:2 * L, h * LANES:(h + 1) * LANES]).T.astype(BF16)
            ur_ref[c, h] = _dot(kd_t, v)

        lv = lv_ref[...]
        heads = [(p, hh) for p in range(N_HEADS // 2) for hh in range(2)]

        def level_scores(p, hh):
            psl = slice(p * LANES, (p + 1) * LANES)
            return [_dot_nt(qt_ref[n, hh, :, psl], kt_ref[n, :, psl]) for n in range(NL + 1)]

        def fold(scores):
            s_mat = jnp.zeros((L, L), F32)
            for n, sc in enumerate(scores):
                s_mat = jnp.where(lv == n, sc, s_mat)
            return s_mat.astype(BF16)

        pending = level_scores(*heads[0])

        cum_ref[rows, :] = cum
        a_all = ga - pltpu.roll(cum, LANES - N_HEADS, 1)
        a_t = a_all.T
        ml = jnp.zeros((L, LANES), F32)
        for h in range(N_HEADS):
            b_col = cum[:, N_HEADS + h:N_HEADS + h + 1]
            log_d = jnp.where(tril, b_col + a_t[h:h + 1, :], -jnp.inf)
            m_loc = jnp.max(log_d, axis=1, keepdims=True)
            pm_ref[h] = (_dot_nt(head(C_MQ, h, BF16), head(C_MK, h, BF16)) * jnp.exp(log_d - m_loc)).astype(BF16)
            ml = jnp.where(lane == h, m_loc, ml)
            w_loc = jnp.exp(b_col[L - 1:L, :] + a_all[:, h:h + 1] - m_loc[L - 1:L, :])
            kw_ref[h] = (head(C_MK, h) * w_loc).T.astype(BF16)
        ml_ref[rows, :] = ml

        s_bf = []
        for i in range(len(heads)):
            nxt = level_scores(*heads[i + 1]) if i + 1 < len(heads) else None
            s_bf.append(fold(pending))
            pending = nxt
            if i == 0:
                for h in range(N_HEADS):
                    v = head(C_MV, h, BF16)
                    ho_ref[rows, h * LANES:(h + 1) * LANES] = _dot(pm_ref[h], v)
                    u_ref[c, h] = _dot(kw_ref[h], v)
        for p in range(N_HEADS // 2):
            psl = slice(p * LANES, (p + 1) * LANES)
            vp = p_ref[rows, C_HI + p * LANES:C_HI + (p + 1) * LANES]
            ho_ref[rows, 2 * HP + p * LANES:2 * HP + (p + 1) * LANES] = (
                _dot(s_bf[2 * p], jnp.where(low, vp, 0.0).astype(BF16))
                + _dot(s_bf[2 * p + 1], jnp.where(low, 0.0, vp).astype(BF16)))
            uh_ref[c, p] = jnp.where(bd, _dot(k_suf[:, psl].T.astype(BF16), vp.astype(BF16)), 0.0)

    def carry_states(c):
        rows = slice(c * L, (c + 1) * L)

        ml = ml_ref[rows, :]
        b_all = pltpu.roll(cum_ref[rows, :], LANES - N_HEADS, 1)
        m_prev = m_ref[0:1, :]
        m_t = jnp.maximum(ml, b_all + m_prev)
        r_all = jnp.exp(ml - m_t)
        w_all = jnp.exp(b_all + m_prev - m_t)
        e_all = jnp.exp(-m_t)
        for h in range(N_HEADS):
            sl = slice(h * LANES, (h + 1) * LANES)
            gr_ref[rows, sl] = jnp.broadcast_to(r_all[:, h:h + 1], (L, LANES))
            gw_ref[rows, sl] = jnp.broadcast_to(w_all[:, h:h + 1], (L, LANES))
            ge_ref[rows, sl] = jnp.broadcast_to(e_all[:, h:h + 1], (L, LANES))
        m_new = m_t[L - 1:L, :]
        dec_row = jnp.exp(b_all[L - 1:L, :] + m_prev - m_new)
        g_row = jnp.exp(ml[L - 1:L, :] - m_new)
        m_ref[0:1, :] = m_new

        for h in range(N_HEADS):
            q = p_ref[rows, C_MQ + h * LANES:C_MQ + (h + 1) * LANES].astype(BF16)
            it_ref[rows, h * LANES:(h + 1) * LANES] = _dot(q, c_ref[h].astype(BF16))
        for h in range(N_HEADS):
            q = p_ref[rows, C_RQ + h * LANES:C_RQ + (h + 1) * LANES]
            qd = (q * dec_ref[0:L, h * LANES:(h + 1) * LANES]).astype(BF16)
            it_ref[rows, HP + h * LANES:HP + (h + 1) * LANES] = _dot(qd, r_ref[h].astype(BF16))
        for p in range(N_HEADS // 2):
            psl = slice(p * LANES, (p + 1) * LANES)
            it_ref[rows, 2 * HP + p * LANES:2 * HP + (p + 1) * LANES] = _dot(qi_ref[rows, psl], s_ref[p].astype(BF16))

        d_col = dh_ref[c:c + 1, :].T
        for h in range(N_HEADS):
            c_ref[h] = dec_row[:, h:h + 1] * c_ref[h] + g_row[:, h:h + 1] * u_ref[c, h]
            r_ref[h] = math.exp(L * LG[h]) * r_ref[h] + ur_ref[c, h]
        for p in range(N_HEADS // 2):
            s_ref[p] = d_col[p * LANES:(p + 1) * LANES, :] * s_ref[p] + uh_ref[c, p]

    def mix_rows(rows, T):
        valid = _lane((T, LANES)) < M_DH
        low_t = _lane((T, LANES)) < H_DK
        parts = []
        for h in range(N_HEADS):
            sl = slice(h * LANES, (h + 1) * LANES)
            num = gr_ref[rows, sl] * ho_ref[rows, sl] + gw_ref[rows, sl] * it_ref[rows, sl]
            den = num[:, M_DH:M_DH + 1]
            hm = num / jnp.maximum(jnp.abs(den), ge_ref[rows, sl])
            o = p_ref[rows, C_MO + h * LANES:C_MO + (h + 1) * LANES]
            z = p_ref[rows, C_MZ + h * LANES:C_MZ + (h + 1) * LANES]
            gain = vec_ref[V_MNW:V_MNW + 1, sl]
            parts.append((_head_layer_norm_padded(hm, valid) * gain * _sigmoid(o) * _silu(z)).astype(BF16))
        for h in range(N_HEADS):
            sl = slice(HP + h * LANES, HP + (h + 1) * LANES)
            hn = _head_layer_norm_padded(ho_ref[rows, sl] + it_ref[rows, sl], valid)
            g = p_ref[rows, C_RG + h * LANES:C_RG + (h + 1) * LANES]
            parts.append((hn * vec_ref[V_RNW:V_RNW + 1, h * LANES:(h + 1) * LANES] * _silu(g)).astype(BF16))
        for p in range(N_HEADS // 2):
            sl = slice(2 * HP + p * LANES, 2 * HP + (p + 1) * LANES)
            hn = _pair_rms_norm(ho_ref[rows, sl] + it_ref[rows, sl], low_t)
            g = p_ref[rows, C_HG + p * LANES:C_HG + (p + 1) * LANES]
            parts.append((hn * vec_ref[V_HNW:V_HNW + 1, p * LANES:(p + 1) * LANES] * _silu(g)).astype(BF16))
        return jnp.concatenate(parts, axis=-1)

    def out_rows(rows, proj):
        y_ref[0, rows, :] = _layer_norm_rows(ALPHA * x_ref[0, rows, :] + proj,
                                             vec_ref[V_LNG:V_LNG + 1, :], vec_ref[V_LNB:V_LNB + 1, :])

    intra(0)
    for c in range(n_chunks):
        if c + 1 < n_chunks:
            intra(c + 1)
        carry_states(c)
    half = TB // 2
    rows0, rows1 = slice(0, half), slice(half, TB)
    mix0 = mix_rows(rows0, half)
    mix1 = mix_rows(rows1, half)
    proj0 = _dot(mix0, wout_ref[...])
    proj1 = _dot(mix1, wout_ref[...])
    out_rows(rows0, proj0)
    out_rows(rows1, proj1)

    @pl.when(j == n_tb - 1)
    def _final_states():
        convout_ref[0] = pre_ref[SUBLANES - (CONV_W - 1):SUBLANES, :]
        for h in range(N_HEADS):
            c_aug = c_ref[h]
            cout_ref[0, h] = c_aug[0:M_DH, 0:M_DH]
            nout_ref[0, h:h + 1, :] = c_aug.T[M_DH:M_DH + 1, 0:M_DH]
            r_full = r_ref[h]
            rout_ref[0, h] = jnp.concatenate(
                [r_full[0:ROPE_HALF, 0:R_DH], r_full[LANES // 2:LANES // 2 + ROPE_HALF, 0:R_DH]], axis=0)
            p, hh = divmod(h, 2)
            sout_ref[0, h] = s_ref[p][hh * H_DK:(hh + 1) * H_DK, hh * H_DK:(hh + 1) * H_DK]
        mout_ref[0] = m_ref[0:1, 0:N_HEADS]


def _full_spec(shape):
    nd = len(shape)
    return pl.BlockSpec(shape, lambda *_: (0,) * nd)


def _prefill_layer(layer, x, win, wout, cos, sin, vec, hlb, lv, TB):
    B, T, _ = x.shape
    n_tb = T // TB
    n_chunks = TB // CHUNK
    kern = functools.partial(_prefill_kernel, layer, TB)

    def layer_spec(a, single_buffer=False):
        kw = dict(pipeline_mode=pl.Buffered(1)) if single_buffer else {}
        return pl.BlockSpec((None,) + a.shape[1:], lambda b, j: (layer,) + (0,) * (a.ndim - 1), **kw)

    in_specs = [
        pl.BlockSpec((1, TB, D_MODEL), lambda b, j: (b, j, 0)),
        layer_spec(win, True), layer_spec(wout, True),
        pl.BlockSpec((TB, LANES), lambda b, j: (j, 0)),
        pl.BlockSpec((TB, LANES), lambda b, j: (j, 0)),
        layer_spec(vec), _full_spec(hlb.shape), _full_spec(lv.shape),
    ]
    out_shape = (
        jax.ShapeDtypeStruct((B, T, D_MODEL), F32),
        jax.ShapeDtypeStruct((B, N_HEADS, M_DH, M_DH), F32),
        jax.ShapeDtypeStruct((B, N_HEADS, M_DH), F32),
        jax.ShapeDtypeStruct((B, 1, N_HEADS), F32),
        jax.ShapeDtypeStruct((B, CONV_W - 1, 2 * HP), F32),
        jax.ShapeDtypeStruct((B, N_HEADS, R_DH, R_DH), F32),
        jax.ShapeDtypeStruct((B, N_HEADS, H_DK, H_DK), F32),
    )
    out_specs = (
        pl.BlockSpec((1, TB, D_MODEL), lambda b, j: (b, j, 0)),
        pl.BlockSpec((1, N_HEADS, M_DH, M_DH), lambda b, j: (b, 0, 0, 0)),
        pl.BlockSpec((1, N_HEADS, M_DH), lambda b, j: (b, 0, 0)),
        pl.BlockSpec((1, 1, N_HEADS), lambda b, j: (b, 0, 0)),
        pl.BlockSpec((1, CONV_W - 1, 2 * HP), lambda b, j: (b, 0, 0)),
        pl.BlockSpec((1, N_HEADS, R_DH, R_DH), lambda b, j: (b, 0, 0, 0)),
        pl.BlockSpec((1, N_HEADS, H_DK, H_DK), lambda b, j: (b, 0, 0, 0)),
    )
    L = CHUNK
    scratch = [
        pltpu.VMEM((TB, NP_SCR), F32),
        pltpu.VMEM((TB + SUBLANES, 2 * HP), F32),
        pltpu.VMEM((TB, MIXP), F32),
        pltpu.VMEM((TB, LANES), F32),
        pltpu.VMEM((TB, LANES), F32),
        pltpu.VMEM((N_HEADS, L, L), BF16),
        pltpu.VMEM((N_HEADS, LANES, L), BF16),
        pltpu.VMEM((N_LEVELS + 1, 2, L, H_WIDTH), BF16),
        pltpu.VMEM((N_LEVELS + 1, L, H_WIDTH), BF16),
        pltpu.VMEM((TB, H_WIDTH), BF16),
        pltpu.VMEM((max(n_chunks, SUBLANES), H_WIDTH), F32),
        pltpu.VMEM((n_chunks, N_HEADS, LANES, LANES), F32),
        pltpu.VMEM((n_chunks, N_HEADS, LANES, LANES), F32),
        pltpu.VMEM((n_chunks, N_HEADS // 2, LANES, LANES), F32),
        pltpu.VMEM((N_HEADS, LANES, LANES), F32),
        pltpu.VMEM((N_HEADS, LANES, LANES), F32),
        pltpu.VMEM((N_HEADS // 2, LANES, LANES), F32),
        pltpu.VMEM((SUBLANES, LANES), F32),
        pltpu.VMEM((N_HEADS, L, L), F32),
        pltpu.VMEM((2 * L, HP), F32),
        pltpu.VMEM((TB, D_MODEL), BF16),
        pltpu.VMEM((TB, MIXP), F32),
        pltpu.VMEM((TB, HP), F32),
        pltpu.VMEM((TB, HP), F32),
        pltpu.VMEM((TB, HP), F32),
    ]
    return pl.pallas_call(
        kern, grid=(B, n_tb), in_specs=in_specs, out_specs=out_specs, out_shape=out_shape,
        scratch_shapes=scratch,
        compiler_params=pltpu.CompilerParams(dimension_semantics=("arbitrary", "arbitrary"),
                                             vmem_limit_bytes=VMEM_LIMIT),
        name=f"prefill_layer{layer}",
    )(x, win, wout, cos, sin, vec, hlb, lv)


DEC_KB = 4
KR_M = M_DH // DEC_KB
KR_H = H_DK // DEC_KB
G_DEC, G_WS, G_EN, G_GAM, G_SM, G_QN, G_SR, G_SH = (i * N_HEADS for i in range(8))


def _stream_state(q_blk, dec, kw_blk, v, in_ref, out_ref, nrows):
    acc = jnp.zeros_like(v)
    for kk in range(nrows):
        st = in_ref[0, 0, kk]
        d = dec if dec.shape[0] == 1 else dec[kk:kk + 1, :]
        acc = acc + q_blk[kk:kk + 1, :] * st
        out_ref[0, 0, kk] = d * st + kw_blk[kk:kk + 1, :] * v
    return acc


def _decode_kernel(x_ref, wt_ref, wout_ref, cos_ref, sin_ref, dvec_ref, dcol_ref, hlbt_ref, vec_ref,
                   cin_ref, nin_ref, min_ref, convin_ref, rin_ref, sin_st_ref,
                   y_ref, cout_ref, nout_ref, mout_ref, convout_ref, rout_ref, sout_ref,
                   p_ref, xs_ref, ht_ref, mix_ref, g_ref):
    l = pl.program_id(0)
    h = pl.program_id(1)
    kb = pl.program_id(2)
    first = jnp.logical_and(h == 0, kb == 0)
    last = jnp.logical_and(h == N_HEADS - 1, kb == DEC_KB - 1)
    B = x_ref.shape[0]

    @pl.when(jnp.logical_and(first, l == 0))
    def _load_x():
        xs_ref[...] = x_ref[...]

    @pl.when(first)
    def _project():
        p_ref[0:N_IN, :] = _dot_nt(wt_ref[0], xs_ref[...].astype(BF16))
        ht_ref[...] = jnp.zeros_like(ht_ref)

        pre = p_ref[O_MQ:O_MQ + 2 * M_WIDTH, :].T
        cw = dvec_ref[0, 0:CONV_W, :]
        acc = dvec_ref[0, CONV_W:CONV_W + 1, :] + cw[CONV_W - 1:CONV_W, :] * pre
        for jj in range(CONV_W - 1):
            acc = acc + cw[jj:jj + 1, :] * convin_ref[0, jj]
        for jj in range(CONV_W - 2):
            convout_ref[0, jj] = convin_ref[0, jj + 1]
        convout_ref[0, CONV_W - 2] = pre
        p_ref[O_MQ:O_MQ + 2 * M_WIDTH, :] = _silu(acc).T

        gt = p_ref[O_IF:O_IF + 2 * N_HEADS, :] + dcol_ref[0, D_MIX:D_MIX + 2 * N_HEADS, :]
        ig = gt[0:N_HEADS]
        lf = _log_sigmoid(gt[N_HEADS:2 * N_HEADS])
        m_prev = min_ref[0]
        m_t = jnp.maximum(ig, lf + m_prev)
        dec = jnp.exp(lf + m_prev - m_t)
        w_s = jnp.exp(ig - m_t)
        mout_ref[0] = m_t
        g_ref[G_DEC:G_DEC + N_HEADS, :] = dec
        g_ref[G_WS:G_WS + N_HEADS, :] = w_s
        g_ref[G_EN:G_EN + N_HEADS, :] = jnp.exp(-m_t)
        for hh in range(N_HEADS):
            q = p_ref[O_MQ + hh * M_DH:O_MQ + (hh + 1) * M_DH, :]
            k = p_ref[O_MK + hh * M_DH:O_MK + (hh + 1) * M_DH, :] * (M_DH ** -0.5)
            n_old = nin_ref[0, hh]
            d_h = dec[hh:hh + 1]
            kw = k * w_s[hh:hh + 1]
            g_ref[G_SM + hh:G_SM + hh + 1, :] = jnp.sum(q * k, axis=0, keepdims=True) * w_s[hh:hh + 1]
            g_ref[G_QN + hh:G_QN + hh + 1, :] = jnp.sum(q * n_old, axis=0, keepdims=True)
            nout_ref[0, hh] = d_h * n_old + kw
            p_ref[O_MK + hh * M_DH:O_MK + (hh + 1) * M_DH, :] = kw

        cos = cos_ref[...]
        sin = sin_ref[...]
        for hh in range(N_HEADS):
            g_ref[G_GAM + hh:G_GAM + hh + 1, :] = jnp.full((1, B), math.exp(LG[hh]), F32)
            rot = []
            for off, scale in ((O_RQ, 1.0), (O_RK, R_DH ** -0.5)):
                x1 = p_ref[off + hh * R_DH:off + hh * R_DH + ROPE_HALF, :]
                x2 = p_ref[off + hh * R_DH + ROPE_HALF:off + (hh + 1) * R_DH, :]
                r1 = (x1 * cos - x2 * sin) * scale
                r2 = (x1 * sin + x2 * cos) * scale
                p_ref[off + hh * R_DH:off + hh * R_DH + ROPE_HALF, :] = r1
                p_ref[off + hh * R_DH + ROPE_HALF:off + (hh + 1) * R_DH, :] = r2
                rot.append((r1, r2))
            (q1, q2), (k1, k2) = rot
            g_ref[G_SR + hh:G_SR + hh + 1, :] = (jnp.sum(q1 * k1, axis=0, keepdims=True)
                                                 + jnp.sum(q2 * k2, axis=0, keepdims=True))

        lbs = _hgrn_lower_bounds(hlbt_ref[...], 1)
        lb = lbs[0]
        for i in range(1, DEPTH):
            lb = jnp.where(l == i, lbs[i], lb)
        lfh, kh = _hgrn_gates(p_ref[O_HF:O_HF + H_WIDTH, :], lb)
        eh = jnp.exp(lfh)
        qh = p_ref[O_HQ:O_HQ + H_WIDTH, :]
        p_ref[O_HF:O_HF + H_WIDTH, :] = eh
        p_ref[O_HK:O_HK + H_WIDTH, :] = kh
        p_ref[O_HQ:O_HQ + H_WIDTH, :] = qh * eh
        qk = qh * kh
        for hh in range(N_HEADS):
            g_ref[G_SH + hh:G_SH + hh + 1, :] = jnp.sum(qk[hh * H_DK:(hh + 1) * H_DK], axis=0, keepdims=True)

    def rows(base, width, n):
        return pl.ds(pl.multiple_of(base + h * width + kb * n, SUBLANES), n)

    dec = g_ref[pl.ds(G_DEC + h, 1), :]
    acc = _stream_state(p_ref[rows(O_MQ, M_DH, KR_M), :], dec, p_ref[rows(O_MK, M_DH, KR_M), :],
                        p_ref[pl.ds(pl.multiple_of(O_MV + h * M_DH, SUBLANES), M_DH), :], cin_ref, cout_ref, KR_M)
    ht_ref[pl.ds(pl.multiple_of(h * M_DH, SUBLANES), M_DH), :] += acc

    gam = g_ref[pl.ds(G_GAM + h, 1), :]
    acc = _stream_state(p_ref[rows(O_RQ, R_DH, KR_M), :], gam, p_ref[rows(O_RK, R_DH, KR_M), :],
                        p_ref[pl.ds(pl.multiple_of(O_RV + h * R_DH, SUBLANES), R_DH), :], rin_ref, rout_ref, KR_M)
    ht_ref[pl.ds(pl.multiple_of(M_WIDTH + h * R_DH, SUBLANES), R_DH), :] += acc

    acc = _stream_state(p_ref[rows(O_HQ, H_DK, KR_H), :], p_ref[rows(O_HF, H_DK, KR_H), :],
                        p_ref[rows(O_HK, H_DK, KR_H), :],
                        p_ref[pl.ds(pl.multiple_of(O_HI + h * H_DK, SUBLANES), H_DK), :], sin_st_ref, sout_ref, KR_H)
    ht_ref[pl.ds(pl.multiple_of(M_WIDTH + R_WIDTH + h * H_DK, SUBLANES), H_DK), :] += acc

    @pl.when(last)
    def _finish():
        def head_ln(a):
            mu = jnp.mean(a, axis=0, keepdims=True)
            d = a - mu
            return d * lax.rsqrt(jnp.mean(d * d, axis=0, keepdims=True) + HEAD_EPS)

        for hh in range(N_HEADS):
            sl = slice(hh * M_DH, (hh + 1) * M_DH)
            v = p_ref[O_MV + hh * M_DH:O_MV + (hh + 1) * M_DH, :]
            d_h = g_ref[G_DEC + hh:G_DEC + hh + 1, :]
            s = g_ref[G_SM + hh:G_SM + hh + 1, :]
            num = s * v + ht_ref[sl, :] * d_h
            den = s + g_ref[G_QN + hh:G_QN + hh + 1, :] * d_h
            hm = num / jnp.maximum(jnp.abs(den), g_ref[G_EN + hh:G_EN + hh + 1, :])
            o = p_ref[O_MO + hh * M_DH:O_MO + (hh + 1) * M_DH, :]
            z = p_ref[O_MZ + hh * M_DH:O_MZ + (hh + 1) * M_DH, :]
            mix_ref[sl, :] = head_ln(hm) * dcol_ref[0, sl, :] * _sigmoid(o) * _silu(z)
        for hh in range(N_HEADS):
            sl = slice(M_WIDTH + hh * R_DH, M_WIDTH + (hh + 1) * R_DH)
            v = p_ref[O_RV + hh * R_DH:O_RV + (hh + 1) * R_DH, :]
            hr = g_ref[G_SR + hh:G_SR + hh + 1, :] * v + ht_ref[sl, :] * math.exp(LG[hh])
            gg = p_ref[O_RG + hh * R_DH:O_RG + (hh + 1) * R_DH, :]
            mix_ref[sl, :] = head_ln(hr) * dcol_ref[0, sl, :] * _silu(gg)
        for hh in range(N_HEADS):
            sl = slice(M_WIDTH + R_WIDTH + hh * H_DK, M_WIDTH + R_WIDTH + (hh + 1) * H_DK)
            v = p_ref[O_HI + hh * H_DK:O_HI + (hh + 1) * H_DK, :]
            ho = g_ref[G_SH + hh:G_SH + hh + 1, :] * v + ht_ref[sl, :]
            hn = ho * lax.rsqrt(jnp.mean(ho * ho, axis=0, keepdims=True) + HEAD_EPS)
            gg = p_ref[O_HG + hh * H_DK:O_HG + (hh + 1) * H_DK, :]
            mix_ref[sl, :] = hn * dcol_ref[0, sl, :] * _silu(gg)
        x = xs_ref[...]
        out = _dot(mix_ref[...].T.astype(BF16), wout_ref[0])
        y = _layer_norm_rows(ALPHA * x + out, vec_ref[0, V_LNG:V_LNG + 1, :], vec_ref[0, V_LNB:V_LNB + 1, :])
        xs_ref[...] = y
        y_ref[0] = y


def _decode(x, wt, wout, cos, sin, dvec, dcol, hlbt, vec, st_c, st_n, st_m, st_conv, st_r, st_s):
    B = x.shape[0]

    def lspec(a):
        nd = a.ndim
        return pl.BlockSpec((1,) + a.shape[1:], lambda l, h, kb: (l,) + (0,) * (nd - 1))

    def kv_spec(a, n):
        return pl.BlockSpec((1, 1, n) + a.shape[3:], lambda l, h, kb: (l, h, kb, 0, 0))

    in_specs = [
        _full_spec(x.shape), lspec(wt), lspec(wout), _full_spec(cos.shape), _full_spec(sin.shape),
        lspec(dvec), lspec(dcol), _full_spec(hlbt.shape), lspec(vec),
        kv_spec(st_c, KR_M), lspec(st_n), lspec(st_m), lspec(st_conv), kv_spec(st_r, KR_M), kv_spec(st_s, KR_H),
    ]
    out_shape = (
        jax.ShapeDtypeStruct((DEPTH, B, D_MODEL), F32),
        jax.ShapeDtypeStruct(st_c.shape, F32), jax.ShapeDtypeStruct(st_n.shape, F32),
        jax.ShapeDtypeStruct(st_m.shape, F32), jax.ShapeDtypeStruct(st_conv.shape, F32),
        jax.ShapeDtypeStruct(st_r.shape, F32), jax.ShapeDtypeStruct(st_s.shape, F32),
    )
    out_specs = (
        pl.BlockSpec((1, B, D_MODEL), lambda l, h, kb: (l, 0, 0)),
        kv_spec(st_c, KR_M), lspec(st_n), lspec(st_m), lspec(st_conv), kv_spec(st_r, KR_M), kv_spec(st_s, KR_H),
    )
    scratch = [
        pltpu.VMEM((N_IN + H_WIDTH, B), F32),
        pltpu.VMEM((B, D_MODEL), F32),
        pltpu.VMEM((D_MIX, B), F32),
        pltpu.VMEM((D_MIX, B), F32),
        pltpu.VMEM((8 * N_HEADS, B), F32),
    ]
    return pl.pallas_call(
        _decode_kernel, grid=(DEPTH, N_HEADS, DEC_KB), in_specs=in_specs, out_specs=out_specs,
        out_shape=out_shape, scratch_shapes=scratch,
        compiler_params=pltpu.CompilerParams(dimension_semantics=("arbitrary", "arbitrary", "arbitrary"),
                                             vmem_limit_bytes=VMEM_LIMIT),
        name="decode_step",
    )(x, wt, wout, cos, sin, dvec, dcol, hlbt, vec, st_c, st_n, st_m, st_conv, st_r, st_s)


def _rope_angles(pos):
    inv = ROPE_BASE ** (-jnp.arange(ROPE_HALF, dtype=F32) / ROPE_HALF)
    ang = pos.astype(F32)[:, None] * inv[None, :]
    return jnp.cos(ang), jnp.sin(ang)


PREFILL_BLOCK = 512


def kernel(x_prompt, x_sample, state_mlstm_C, state_mlstm_n, state_mlstm_m, state_mlstm_conv, state_ret, state_hgrn, w_in, conv_w, conv_b, b_mgate, m_norm_w, r_norm_w, h_norm_w, hgrn_lb, w_out, ln_g, ln_b):
    B, T, _ = x_prompt.shape

    wt = jnp.swapaxes(w_in, 1, 2).astype(BF16)
    win_p = jnp.concatenate(
        [_pad_heads(wt[:, O_MQ:O_IF], 1),
         _pad_rope_heads(wt[:, O_RQ:O_RV], 1), _pad_heads(wt[:, O_RV:O_HF], 1),
         wt[:, O_HF:N_IN],
         jnp.pad(wt[:, O_IF:O_RQ], ((0, 0), (0, LANES - 2 * N_HEADS), (0, 0)))], axis=1)
    wout_b = w_out.astype(BF16)
    wout_p = jnp.concatenate([_pad_heads(wout_b[:, 0:M_WIDTH + R_WIDTH], 1), wout_b[:, M_WIDTH + R_WIDTH:]],
                             axis=1)

    def row(a):
        return jnp.pad(a, ((0, 0), (0, D_MODEL - a.shape[-1])))[:, None, :]

    vec = jnp.concatenate([_pad_heads(conv_w, 2), row(_pad_heads(conv_b, 1)), row(b_mgate),
                           row(_pad_heads(m_norm_w, 1)), row(_pad_heads(r_norm_w, 1)), row(h_norm_w),
                           row(ln_g), row(ln_b)], axis=1)
    vec = jnp.pad(vec, ((0, 0), (0, V_ROWS - vec.shape[1]), (0, 0)))
    vec = jnp.concatenate([vec, jnp.repeat(vec[:, V_CONVW:V_CONVB + 1], SUBLANES, axis=1)], axis=1)
    dvec = jnp.pad(jnp.concatenate([conv_w, conv_b[:, None, :]], axis=1),
                   ((0, 0), (0, SUBLANES - CONV_W - 1), (0, 0)))
    dcol = jnp.concatenate([m_norm_w, r_norm_w, h_norm_w, b_mgate], axis=1)[:, :, None]
    hlb = hgrn_lb.astype(F32)

    lv = jnp.asarray(_level_table(CHUNK))

    c, s = _rope_angles(jnp.arange(T, dtype=jnp.int32))
    z = jnp.zeros((T, LANES // 2 - ROPE_HALF), F32)
    cos_p = jnp.concatenate([c, z, c, z], axis=1)
    sin_p = jnp.concatenate([-s, z, s, z], axis=1)

    hp = x_prompt
    st = [[] for _ in range(6)]
    for l in range(DEPTH):
        outs = _prefill_layer(l, hp, win_p, wout_p, cos_p, sin_p, vec, hlb, lv, PREFILL_BLOCK)
        hp = outs[0]
        for k in range(6):
            st[k].append(outs[1 + k])
    mC_p = jnp.stack(st[0])
    mn_p = jnp.stack(st[1])
    mm_p = jnp.stack(st[2])[:, :, 0, :]
    conv_p = _unpad_heads(jnp.stack(st[3]))
    ret_p = jnp.stack(st[4])
    hgrn_p = jnp.stack(st[5])

    n_s = x_sample.shape[0]
    cs, ss = _rope_angles(PAST_LEN + jnp.arange(x_sample.shape[1], dtype=jnp.int32))
    cos_s = jnp.broadcast_to(cs[0][:, None], (ROPE_HALF, n_s))
    sin_s = jnp.broadcast_to(ss[0][:, None], (ROPE_HALF, n_s))
    ys, c_t, n_t, m_t, conv_t, r_t, s_t = _decode(
        x_sample[:, 0, :], wt, wout_b, cos_s, sin_s, dvec, dcol, hlb.T, vec,
        jnp.transpose(state_mlstm_C, (0, 2, 3, 4, 1)), jnp.transpose(state_mlstm_n, (0, 2, 3, 1)),
        jnp.transpose(state_mlstm_m, (0, 2, 1)), jnp.transpose(state_mlstm_conv, (0, 2, 1, 3)),
        jnp.transpose(state_ret, (0, 2, 3, 4, 1)), jnp.transpose(state_hgrn, (0, 2, 3, 4, 1)))
    hs = ys[DEPTH - 1][:, None, :]
    mC_s = jnp.transpose(c_t, (0, 4, 1, 2, 3))
    mn_s = jnp.transpose(n_t, (0, 3, 1, 2))
    mm_s = jnp.transpose(m_t, (0, 2, 1))
    conv_s = jnp.transpose(conv_t, (0, 2, 1, 3))
    ret_s = jnp.transpose(r_t, (0, 4, 1, 2, 3))
    hgrn_s = jnp.transpose(s_t, (0, 4, 1, 2, 3))

    return (hp, hs, mC_p, mn_p, mm_p, conv_p, ret_p, hgrn_p, mC_s, mn_s, mm_s, conv_s, ret_s, hgrn_s)
```

```python
import functools
import math

import numpy as np
import jax
import jax.numpy as jnp
from jax import lax
from jax.experimental import pallas as pl
from jax.experimental.pallas import tpu as pltpu

F32 = jnp.float32
BF16 = jnp.bfloat16

D_MODEL = 1024
DEPTH = 2
PAST_LEN = 16384
N_HEADS = 4
M_DH = 96
R_DH = 96
H_DK = 64
M_WIDTH = N_HEADS * M_DH
R_WIDTH = N_HEADS * R_DH
H_WIDTH = N_HEADS * H_DK
D_MIX = M_WIDTH + R_WIDTH + H_WIDTH
CONV_W = 4
ROPE_BASE = 10000.0
LN_EPS = 1e-5
HEAD_EPS = 1e-6
ALPHA = (2 * DEPTH) ** 0.25

LANES = 128
SUBLANES = 8
HP = N_HEADS * LANES
ROPE_HALF = R_DH // 2
VMEM_LIMIT = 56 * 1024 * 1024

O_MQ, O_MK, O_MV, O_MO, O_MZ = 0, M_WIDTH, 2 * M_WIDTH, 3 * M_WIDTH, 4 * M_WIDTH
O_IF = 5 * M_WIDTH
O_RQ = O_IF + 2 * N_HEADS
O_RK, O_RV, O_RG = O_RQ + R_WIDTH, O_RQ + 2 * R_WIDTH, O_RQ + 3 * R_WIDTH
O_HF = O_RQ + 4 * R_WIDTH
O_HI, O_HQ, O_HG = O_HF + H_WIDTH, O_HF + 2 * H_WIDTH, O_HF + 3 * H_WIDTH
N_IN = O_HF + 4 * H_WIDTH
O_HK = N_IN

C_MQ, C_MK, C_MV, C_MO, C_MZ = 0, HP, 2 * HP, 3 * HP, 4 * HP
C_RQ, C_RK, C_RV, C_RG = 5 * HP, 6 * HP, 7 * HP, 8 * HP
C_HF = 9 * HP
C_HI = C_HF + H_WIDTH
C_HQ = C_HI + H_WIDTH
C_HG = C_HQ + H_WIDTH
C_IF = C_HG + H_WIDTH
NP_IN = C_IF + LANES
C_HK = NP_IN
NP_SCR = NP_IN + H_WIDTH
MIXP = 2 * HP + H_WIDTH

V_CONVW, V_CONVB, V_BIF, V_MNW, V_RNW, V_HNW, V_LNG, V_LNB = 0, CONV_W, CONV_W + 1, CONV_W + 2, CONV_W + 3, CONV_W + 4, CONV_W + 5, CONV_W + 6
V_ROWS = 2 * SUBLANES
V_CONV8 = V_ROWS
V_ROWS8 = V_ROWS + (CONV_W + 1) * SUBLANES

CHUNK = 128
N_LEVELS = int(math.log2(CHUNK))
LG = [math.log1p(-2.0 ** (-5.0 - h)) for h in range(N_HEADS)]


def _pad_groups(a, axis, group, padded):
    shp = a.shape
    n = shp[axis] // group
    a = a.reshape(shp[:axis] + (n, group) + shp[axis + 1:])
    pad = [(0, 0)] * a.ndim
    pad[axis + 1] = (0, padded - group)
    return jnp.pad(a, pad).reshape(shp[:axis] + (n * padded,) + shp[axis + 1:])


def _pad_heads(a, axis):
    return _pad_groups(a, axis, M_DH, LANES)


def _pad_rope_heads(a, axis):
    return _pad_groups(a, axis, ROPE_HALF, LANES // 2)


def _unpad_heads(a, dh=M_DH):
    shp = a.shape
    return a.reshape(shp[:-1] + (shp[-1] // LANES, LANES))[..., :dh].reshape(shp[:-1] + (shp[-1] // LANES * dh,))


def _level_table(L):
    nl = int(math.log2(L))
    tt = np.arange(L)[:, None]
    ss = np.arange(L)[None, :]
    lv = np.full((L, L), -1, np.int32)
    for n in range(nl):
        w = L >> (n + 1)
        same = (tt // (2 * w)) == (ss // (2 * w))
        cond = same & ((tt % (2 * w)) >= w) & ((ss % (2 * w)) < w)
        lv = np.where(cond, n, lv)
    lv = np.where(tt == ss, nl, lv)
    return lv.astype(np.int32)


def _dot(a, b):
    return jnp.dot(a, b, preferred_element_type=F32)


def _dot_nt(a, b):
    return lax.dot_general(a, b, (((1,), (1,)), ((), ())), preferred_element_type=F32)


NEG_LOG2E = -1.4426950408889634


def _sigmoid(x):
    return 1.0 / (1.0 + jnp.exp2(x * NEG_LOG2E))


def _silu(x):
    return x * _sigmoid(x)


def _log_sigmoid(x):
    return jnp.minimum(x, 0.0) - jnp.log(1.0 + jnp.exp2(jnp.abs(x) * NEG_LOG2E))


def _lane(shape):
    return lax.broadcasted_iota(jnp.int32, shape, len(shape) - 1)


def _hgrn_lower_bounds(hlb, layer_axis):
    def take(a, l):
        return a[l:l + 1] if layer_axis == 0 else a[:, l:l + 1]
    mx = jnp.max(hlb, axis=layer_axis, keepdims=True)
    e = jnp.exp(hlb - mx)
    p = e / jnp.sum(e, axis=layer_axis, keepdims=True)
    out = []
    acc = None
    for l in range(DEPTH):
        acc = take(p, l) if acc is None else acc + take(p, l)
        out.append(acc - take(p, 0))
    return out


def _hgrn_gates(fpre, lb):
    a = jnp.log(lb)
    c = jnp.log1p(-lb) + _log_sigmoid(fpre)
    mx = jnp.maximum(a, c)
    lf = mx + jnp.log(jnp.exp(a - mx) + jnp.exp(c - mx))
    kh = (1.0 - lb) * _sigmoid(-fpre)
    return lf, kh


def _cumsum_rows(x):
    L, W = x.shape
    sub = lax.broadcasted_iota(jnp.int32, (SUBLANES, W), 0)
    out = []
    carry = None
    for r in range(L // SUBLANES):
        t = x[r * SUBLANES:(r + 1) * SUBLANES, :]
        s = 1
        while s < SUBLANES:
            t = t + jnp.where(sub >= s, pltpu.roll(t, s, 0), 0.0)
            s *= 2
        if carry is not None:
            t = t + carry
        carry = t[SUBLANES - 1:SUBLANES, :]
        out.append(t)
    return jnp.concatenate(out, axis=0)


def _mid_row_broadcast(bc, w):
    L, W = bc.shape
    if 2 * w >= SUBLANES:
        blocks = [jnp.broadcast_to(bc[b * 2 * w + w:b * 2 * w + w + 1, :], (2 * w, W)) for b in range(L // (2 * w))]
        return blocks[0] if len(blocks) == 1 else jnp.concatenate(blocks, axis=0)
    x3 = bc.reshape(L // SUBLANES, SUBLANES, W)
    sub = lax.broadcasted_iota(jnp.int32, x3.shape, 1)
    out = None
    for b in range(SUBLANES // (2 * w)):
        piece = jnp.broadcast_to(x3[:, b * 2 * w + w:b * 2 * w + w + 1, :], x3.shape)
        out = piece if out is None else jnp.where(sub >= b * 2 * w, piece, out)
    return out.reshape(L, W)


def _layer_norm_rows(r, g, b):
    mu = jnp.mean(r, axis=-1, keepdims=True)
    d = r - mu
    var = jnp.mean(d * d, axis=-1, keepdims=True)
    return d * lax.rsqrt(var + LN_EPS) * g + b


def _head_layer_norm_padded(h, valid):
    dh = M_DH
    hz = jnp.where(valid, h, 0.0)
    mu = jnp.sum(hz, axis=-1, keepdims=True) * (1.0 / dh)
    d = jnp.where(valid, h - mu, 0.0)
    var = jnp.sum(d * d, axis=-1, keepdims=True) * (1.0 / dh)
    return d * lax.rsqrt(var + HEAD_EPS)


def _pair_rms_norm(h, low):
    sq = h * h
    m0 = jnp.sum(jnp.where(low, sq, 0.0), axis=-1, keepdims=True) * (1.0 / H_DK)
    m1 = jnp.sum(jnp.where(low, 0.0, sq), axis=-1, keepdims=True) * (1.0 / H_DK)
    return h * jnp.where(low, lax.rsqrt(m0 + HEAD_EPS), lax.rsqrt(m1 + HEAD_EPS))


def _prefill_kernel(layer, TB,
                    x_ref, win_ref, wout_ref, cos_ref, sin_ref, vec_ref, hlb_ref, lv_ref,
                    y_ref, cout_ref, nout_ref, mout_ref, convout_ref, rout_ref, sout_ref,
                    p_ref, pre_ref, ho_ref, cum_ref, ml_ref, pm_ref, kw_ref, qt_ref, kt_ref, qi_ref, dh_ref,
                    u_ref, ur_ref, uh_ref, c_ref, r_ref, s_ref, m_ref, dm_ref, dec_ref,
                    xb_ref, it_ref, gr_ref, gw_ref, ge_ref):
    L = CHUNK
    NL = N_LEVELS
    n_chunks = TB // L
    j = pl.program_id(1)
    n_tb = pl.num_programs(1)

    @pl.when(j == 0)
    def _init():
        c_ref[...] = jnp.zeros_like(c_ref)
        r_ref[...] = jnp.zeros_like(r_ref)
        s_ref[...] = jnp.zeros_like(s_ref)
        m_ref[...] = jnp.zeros_like(m_ref)
        pre_ref[0:SUBLANES, :] = jnp.zeros((SUBLANES, 2 * HP), F32)
        ti = lax.broadcasted_iota(jnp.int32, (L, L), 0)
        si = lax.broadcasted_iota(jnp.int32, (L, L), 1)
        dist = (ti - si).astype(F32)
        row = lax.broadcasted_iota(jnp.int32, (L, LANES), 0).astype(F32)
        for h in range(N_HEADS):
            dm_ref[h] = jnp.where(ti >= si, jnp.exp(dist * LG[h]), 0.0)
            dec_ref[0:L, h * LANES:(h + 1) * LANES] = jnp.exp((row + 1.0) * LG[h])
            dec_ref[L:2 * L, h * LANES:(h + 1) * LANES] = jnp.exp((L - 1.0 - row) * LG[h])

    xb_ref[...] = x_ref[0].astype(BF16)
    lb = _hgrn_lower_bounds(hlb_ref[...], 0)[layer]
    PIECE = 2 * LANES

    def project(c0):
        c1 = min(c0 + PIECE, NP_IN)
        res = _dot_nt(xb_ref[...], win_ref[c0:c1, :])
        if c0 < C_MV:
            pre_ref[SUBLANES:SUBLANES + TB, c0:c1] = res
        else:
            p_ref[:, c0:c1] = res

    def gate_math(r0, n):
        rows = slice(r0, r0 + n)
        lf, kh = _hgrn_gates(p_ref[rows, C_HF:C_HF + H_WIDTH], lb)
        p_ref[rows, C_HF:C_HF + H_WIDTH] = lf
        p_ref[rows, C_HK:C_HK + H_WIDTH] = kh
        g = p_ref[rows, C_IF:C_IF + LANES] + vec_ref[V_BIF:V_BIF + 1, 0:LANES]
        p_ref[rows, C_IF:C_IF + LANES] = jnp.where(_lane((n, LANES)) < N_HEADS, g, _log_sigmoid(g))

    def conv_math(half, r0, n):
        cols = slice(half * HP, (half + 1) * HP)
        lo = SUBLANES + r0
        def tap(i):
            return jnp.tile(vec_ref[V_CONV8 + i * SUBLANES:V_CONV8 + (i + 1) * SUBLANES, cols], (n // SUBLANES, 1))

        ext = pre_ref[lo - SUBLANES:lo + n, cols]
        acc = tap(CONV_W) + tap(CONV_W - 1) * ext[SUBLANES:, :]
        for t in range(1, CONV_W):
            shifted = pltpu.roll(ext, t, 0)[SUBLANES:, :]
            acc = acc + tap(CONV_W - 1 - t) * shifted
        scale = 1.0 if half == 0 else M_DH ** -0.5
        p_ref[r0:r0 + n, C_MQ + half * HP:C_MQ + (half + 1) * HP] = _silu(acc) * scale

    def rope_math(col, scale, r0, n):
        rows = slice(r0, r0 + n)
        cos = cos_ref[rows, :]
        sin = sin_ref[rows, :]
        for h in range(N_HEADS):
            sl = slice(col + h * LANES, col + (h + 1) * LANES)
            v = p_ref[rows, sl]
            p_ref[rows, sl] = (v * cos + pltpu.roll(v, LANES // 2, 1) * sin) * scale

    def value_math(r0, n):
        one96 = jnp.where(_lane((n, LANES)) == M_DH, 1.0, 0.0)
        for h in range(N_HEADS):
            sl = slice(C_MV + h * LANES, C_MV + (h + 1) * LANES)
            p_ref[r0:r0 + n, sl] = p_ref[r0:r0 + n, sl] + one96

    order = ([C_HF, C_HF + 4 * H_WIDTH] + list(range(C_MQ, C_MV, PIECE)) + list(range(C_RQ, C_RV, PIECE))
             + list(range(C_MV, C_MO, PIECE)) + list(range(C_HF + PIECE, C_HF + 4 * H_WIDTH, PIECE))
             + list(range(C_MO, C_RQ, PIECE)) + list(range(C_RV, C_HF, PIECE)))
    assert sorted(order) == list(range(0, NP_IN, PIECE))
    pos = {c0: i for i, c0 in enumerate(order)}
    vec = []
    for r0 in range(0, TB, 64):
        vec.append((pos[C_HF + 4 * H_WIDTH], functools.partial(gate_math, r0, 64)))
    for half in range(2):
        for r0 in range(0, TB, 32):
            vec.append((pos[C_MQ + half * HP + HP - PIECE], functools.partial(conv_math, half, r0, 32)))
    for col, scale in ((C_RQ, 1.0), (C_RK, R_DH ** -0.5)):
        for r0 in range(0, TB, 64):
            vec.append((pos[col + HP - PIECE], functools.partial(rope_math, col, scale, r0, 64)))
    for r0 in range(0, TB, 128):
        vec.append((pos[C_MV + HP - PIECE], functools.partial(value_math, r0, 128)))
    per_piece = -(-len(vec) // (len(order) - 6))
    vi = 0
    for i, c0 in enumerate(order):
        project(c0)
        done = 0
        while vi < len(vec) and vec[vi][0] < i and done < per_piece:
            vec[vi][1]()
            vi += 1
            done += 1
    while vi < len(vec):
        vec[vi][1]()
        vi += 1
    pre_ref[SUBLANES - (CONV_W - 1):SUBLANES, :] = pre_ref[TB + SUBLANES - (CONV_W - 1):TB + SUBLANES, :]

    tril = lax.broadcasted_iota(jnp.int32, (L, L), 0) >= lax.broadcasted_iota(jnp.int32, (L, L), 1)
    bd = (lax.broadcasted_iota(jnp.int32, (LANES, LANES), 0) < H_DK) == (_lane((LANES, LANES)) < H_DK)
    low = _lane((L, LANES)) < H_DK
    lane = _lane((L, LANES))

    def intra(c):
        rows = slice(c * L, (c + 1) * L)

        ga = p_ref[rows, C_IF:C_IF + LANES]
        cum = _cumsum_rows(ga)
        bc = _cumsum_rows(p_ref[rows, C_HF:C_HF + H_WIDTH])

        def head(col, h, dt=None):
            a = p_ref[rows, col + h * LANES:col + (h + 1) * LANES]
            return a if dt is None else a.astype(dt)

        qh = p_ref[rows, C_HQ:C_HQ + H_WIDTH]
        khh = p_ref[rows, C_HK:C_HK + H_WIDTH]
        head_sel = [jnp.where((_lane((1, H_WIDTH)) & H_DK) == hh * H_DK, 1.0, 0.0).astype(BF16) for hh in range(2)]
        for n in range(NL + 1):
            if n < NL:
                fac = jnp.exp2(jnp.abs(bc - _mid_row_broadcast(bc, L >> (n + 1))) * NEG_LOG2E)
                qn, kn = (qh * fac).astype(BF16), (khh * fac).astype(BF16)
            else:
                qn, kn = qh.astype(BF16), khh.astype(BF16)
            qt_ref[n, 0] = qn * head_sel[0]
            qt_ref[n, 1] = qn * head_sel[1]
            kt_ref[n] = kn
        e_pre = jnp.exp(bc)
        qi_ref[rows, :] = (qh * e_pre).astype(BF16)
        dh_ref[c:c + 1, :] = e_pre[L - 1:L, :]
        k_suf = khh * jnp.exp(bc[L - 1:L, :] - bc)

        for h in range(N_HEADS):
            v = head(C_RV, h, BF16)
            s = (_dot_nt(head(C_RQ, h, BF16), head(C_RK, h, BF16)) * dm_ref[h]).astype(BF16)
            ho_ref[rows, HP + h * LANES:HP + (h + 1) * LANES] = _dot(s, v)
            kd_t = (head(C_RK, h) * dec_ref[L:2 * L, h * LANES:(h + 1) * LANES]).T.astype(BF16)
            ur_ref[c, h] = _dot(kd_t, v)

        lv = lv_ref[...]
        heads = [(p, hh) for p in range(N_HEADS // 2) for hh in range(2)]

        def level_scores(p, hh):
            psl = slice(p * LANES, (p + 1) * LANES)
            return [_dot_nt(qt_ref[n, hh, :, psl], kt_ref[n, :, psl]) for n in range(NL + 1)]

        def fold(scores):
            s_mat = jnp.zeros((L, L), F32)
            for n, sc in enumerate(scores):
                s_mat = jnp.where(lv == n, sc, s_mat)
            return s_mat.astype(BF16)

        pending = level_scores(*heads[0])

        cum_ref[rows, :] = cum
        a_all = ga - pltpu.roll(cum, LANES - N_HEADS, 1)
        a_t = a_all.T
        ml = jnp.zeros((L, LANES), F32)
        for h in range(N_HEADS):
            b_col = cum[:, N_HEADS + h:N_HEADS + h + 1]
            log_d = jnp.where(tril, b_col + a_t[h:h + 1, :], -jnp.inf)
            m_loc = jnp.max(log_d, axis=1, keepdims=True)
            pm_ref[h] = (_dot_nt(head(C_MQ, h, BF16), head(C_MK, h, BF16)) * jnp.exp(log_d - m_loc)).astype(BF16)
            ml = jnp.where(lane == h, m_loc, ml)
            w_loc = jnp.exp(b_col[L - 1:L, :] + a_all[:, h:h + 1] - m_loc[L - 1:L, :])
            kw_ref[h] = (head(C_MK, h) * w_loc).T.astype(BF16)
        ml_ref[rows, :] = ml

        s_bf = []
        for i in range(len(heads)):
            nxt = level_scores(*heads[i + 1]) if i + 1 < len(heads) else None
            s_bf.append(fold(pending))
            pending = nxt
            if i == 0:
                for h in range(N_HEADS):
                    v = head(C_MV, h, BF16)
                    ho_ref[rows, h * LANES:(h + 1) * LANES] = _dot(pm_ref[h], v)
                    u_ref[c, h] = _dot(kw_ref[h], v)
        for p in range(N_HEADS // 2):
            psl = slice(p * LANES, (p + 1) * LANES)
            vp = p_ref[rows, C_HI + p * LANES:C_HI + (p + 1) * LANES]
            ho_ref[rows, 2 * HP + p * LANES:2 * HP + (p + 1) * LANES] = (
                _dot(s_bf[2 * p], jnp.where(low, vp, 0.0).astype(BF16))
                + _dot(s_bf[2 * p + 1], jnp.where(low, 0.0, vp).astype(BF16)))
            uh_ref[c, p] = jnp.where(bd, _dot(k_suf[:, psl].T.astype(BF16), vp.astype(BF16)), 0.0)

    def carry_states(c):
        rows = slice(c * L, (c + 1) * L)

        ml = ml_ref[rows, :]
        b_all = pltpu.roll(cum_ref[rows, :], LANES - N_HEADS, 1)
        m_prev = m_ref[0:1, :]
        m_t = jnp.maximum(ml, b_all + m_prev)
        r_all = jnp.exp(ml - m_t)
        w_all = jnp.exp(b_all + m_prev - m_t)
        e_all = jnp.exp(-m_t)
        for h in range(N_HEADS):
            sl = slice(h * LANES, (h + 1) * LANES)
            gr_ref[rows, sl] = jnp.broadcast_to(r_all[:, h:h + 1], (L, LANES))
            gw_ref[rows, sl] = jnp.broadcast_to(w_all[:, h:h + 1], (L, LANES))
            ge_ref[rows, sl] = jnp.broadcast_to(e_all[:, h:h + 1], (L, LANES))
        m_new = m_t[L - 1:L, :]
        dec_row = jnp.exp(b_all[L - 1:L, :] + m_prev - m_new)
        g_row = jnp.exp(ml[L - 1:L, :] - m_new)
        m_ref[0:1, :] = m_new

        for h in range(N_HEADS):
            q = p_ref[rows, C_MQ + h * LANES:C_MQ + (h + 1) * LANES].astype(BF16)
            it_ref[rows, h * LANES:(h + 1) * LANES] = _dot(q, c_ref[h].astype(BF16))
        for h in range(N_HEADS):
            q = p_ref[rows, C_RQ + h * LANES:C_RQ + (h + 1) * LANES]
            qd = (q * dec_ref[0:L, h * LANES:(h + 1) * LANES]).astype(BF16)
            it_ref[rows, HP + h * LANES:HP + (h + 1) * LANES] = _dot(qd, r_ref[h].astype(BF16))
        for p in range(N_HEADS // 2):
            psl = slice(p * LANES, (p + 1) * LANES)
            it_ref[rows, 2 * HP + p * LANES:2 * HP + (p + 1) * LANES] = _dot(qi_ref[rows, psl], s_ref[p].astype(BF16))

        d_col = dh_ref[c:c + 1, :].T
        for h in range(N_HEADS):
            c_ref[h] = dec_row[:, h:h + 1] * c_ref[h] + g_row[:, h:h + 1] * u_ref[c, h]
            r_ref[h] = math.exp(L * LG[h]) * r_ref[h] + ur_ref[c, h]
        for p in range(N_HEADS // 2):
            s_ref[p] = d_col[p * LANES:(p + 1) * LANES, :] * s_ref[p] + uh_ref[c, p]

    def mix_rows(rows, T):
        valid = _lane((T, LANES)) < M_DH
        low_t = _lane((T, LANES)) < H_DK
        parts = []
        for h in range(N_HEADS):
            sl = slice(h * LANES, (h + 1) * LANES)
            num = gr_ref[rows, sl] * ho_ref[rows, sl] + gw_ref[rows, sl] * it_ref[rows, sl]
            den = num[:, M_DH:M_DH + 1]
            hm = num / jnp.maximum(jnp.abs(den), ge_ref[rows, sl])
            o = p_ref[rows, C_MO + h * LANES:C_MO + (h + 1) * LANES]
            z = p_ref[rows, C_MZ + h * LANES:C_MZ + (h + 1) * LANES]
            gain = vec_ref[V_MNW:V_MNW + 1, sl]
            parts.append((_head_layer_norm_padded(hm, valid) * gain * _sigmoid(o) * _silu(z)).astype(BF16))
        for h in range(N_HEADS):
            sl = slice(HP + h * LANES, HP + (h + 1) * LANES)
            hn = _head_layer_norm_padded(ho_ref[rows, sl] + it_ref[rows, sl], valid)
            g = p_ref[rows, C_RG + h * LANES:C_RG + (h + 1) * LANES]
            parts.append((hn * vec_ref[V_RNW:V_RNW + 1, h * LANES:(h + 1) * LANES] * _silu(g)).astype(BF16))
        for p in range(N_HEADS // 2):
            sl = slice(2 * HP + p * LANES, 2 * HP + (p + 1) * LANES)
            hn = _pair_rms_norm(ho_ref[rows, sl] + it_ref[rows, sl], low_t)
            g = p_ref[rows, C_HG + p * LANES:C_HG + (p + 1) * LANES]
            parts.append((hn * vec_ref[V_HNW:V_HNW + 1, p * LANES:(p + 1) * LANES] * _silu(g)).astype(BF16))
        return jnp.concatenate(parts, axis=-1)

    def out_rows(rows, proj):
        y_ref[0, rows, :] = _layer_norm_rows(ALPHA * x_ref[0, rows, :] + proj,
                                             vec_ref[V_LNG:V_LNG + 1, :], vec_ref[V_LNB:V_LNB + 1, :])

    intra(0)
    for c in range(n_chunks):
        if c + 1 < n_chunks:
            intra(c + 1)
        carry_states(c)
    half = TB // 2
    rows0, rows1 = slice(0, half), slice(half, TB)
    mix0 = mix_rows(rows0, half)
    mix1 = mix_rows(rows1, half)
    proj0 = _dot(mix0, wout_ref[...])
    proj1 = _dot(mix1, wout_ref[...])
    out_rows(rows0, proj0)
    out_rows(rows1, proj1)

    @pl.when(j == n_tb - 1)
    def _final_states():
        convout_ref[0] = pre_ref[SUBLANES - (CONV_W - 1):SUBLANES, :]
        for h in range(N_HEADS):
            c_aug = c_ref[h]
            cout_ref[0, h] = c_aug[0:M_DH, 0:M_DH]
            nout_ref[0, h:h + 1, :] = c_aug.T[M_DH:M_DH + 1, 0:M_DH]
            r_full = r_ref[h]
            rout_ref[0, h] = jnp.concatenate(
                [r_full[0:ROPE_HALF, 0:R_DH], r_full[LANES // 2:LANES // 2 + ROPE_HALF, 0:R_DH]], axis=0)
            p, hh = divmod(h, 2)
            sout_ref[0, h] = s_ref[p][hh * H_DK:(hh + 1) * H_DK, hh * H_DK:(hh + 1) * H_DK]
        mout_ref[0] = m_ref[0:1, 0:N_HEADS]


def _full_spec(shape):
    nd = len(shape)
    return pl.BlockSpec(shape, lambda *_: (0,) * nd)


def _prefill_layer(layer, x, win, wout, cos, sin, vec, hlb, lv, TB):
    B, T, _ = x.shape
    n_tb = T // TB
    n_chunks = TB // CHUNK
    kern = functools.partial(_prefill_kernel, layer, TB)

    def layer_spec(a, single_buffer=False):
        kw = dict(pipeline_mode=pl.Buffered(1)) if single_buffer else {}
        return pl.BlockSpec((None,) + a.shape[1:], lambda b, j: (layer,) + (0,) * (a.ndim - 1), **kw)

    in_specs = [
        pl.BlockSpec((1, TB, D_MODEL), lambda b, j: (b, j, 0)),
        layer_spec(win, True), layer_spec(wout, True),
        pl.BlockSpec((TB, LANES), lambda b, j: (j, 0)),
        pl.BlockSpec((TB, LANES), lambda b, j: (j, 0)),
        layer_spec(vec), _full_spec(hlb.shape), _full_spec(lv.shape),
    ]
    out_shape = (
        jax.ShapeDtypeStruct((B, T, D_MODEL), F32),
        jax.ShapeDtypeStruct((B, N_HEADS, M_DH, M_DH), F32),
        jax.ShapeDtypeStruct((B, N_HEADS, M_DH), F32),
        jax.ShapeDtypeStruct((B, 1, N_HEADS), F32),
        jax.ShapeDtypeStruct((B, CONV_W - 1, 2 * HP), F32),
        jax.ShapeDtypeStruct((B, N_HEADS, R_DH, R_DH), F32),
        jax.ShapeDtypeStruct((B, N_HEADS, H_DK, H_DK), F32),
    )
    out_specs = (
        pl.BlockSpec((1, TB, D_MODEL), lambda b, j: (b, j, 0)),
        pl.BlockSpec((1, N_HEADS, M_DH, M_DH), lambda b, j: (b, 0, 0, 0)),
        pl.BlockSpec((1, N_HEADS, M_DH), lambda b, j: (b, 0, 0)),
        pl.BlockSpec((1, 1, N_HEADS), lambda b, j: (b, 0, 0)),
        pl.BlockSpec((1, CONV_W - 1, 2 * HP), lambda b, j: (b, 0, 0)),
        pl.BlockSpec((1, N_HEADS, R_DH, R_DH), lambda b, j: (b, 0, 0, 0)),
        pl.BlockSpec((1, N_HEADS, H_DK, H_DK), lambda b, j: (b, 0, 0, 0)),
    )
    L = CHUNK
    scratch = [
        pltpu.VMEM((TB, NP_SCR), F32),
        pltpu.VMEM((TB + SUBLANES, 2 * HP), F32),
        pltpu.VMEM((TB, MIXP), F32),
        pltpu.VMEM((TB, LANES), F32),
        pltpu.VMEM((TB, LANES), F32),
        pltpu.VMEM((N_HEADS, L, L), BF16),
        pltpu.VMEM((N_HEADS, LANES, L), BF16),
        pltpu.VMEM((N_LEVELS + 1, 2, L, H_WIDTH), BF16),
        pltpu.VMEM((N_LEVELS + 1, L, H_WIDTH), BF16),
        pltpu.VMEM((TB, H_WIDTH), BF16),
        pltpu.VMEM((max(n_chunks, SUBLANES), H_WIDTH), F32),
        pltpu.VMEM((n_chunks, N_HEADS, LANES, LANES), F32),
        pltpu.VMEM((n_chunks, N_HEADS, LANES, LANES), F32),
        pltpu.VMEM((n_chunks, N_HEADS // 2, LANES, LANES), F32),
        pltpu.VMEM((N_HEADS, LANES, LANES), F32),
        pltpu.VMEM((N_HEADS, LANES, LANES), F32),
        pltpu.VMEM((N_HEADS // 2, LANES, LANES), F32),
        pltpu.VMEM((SUBLANES, LANES), F32),
        pltpu.VMEM((N_HEADS, L, L), F32),
        pltpu.VMEM((2 * L, HP), F32),
        pltpu.VMEM((TB, D_MODEL), BF16),
        pltpu.VMEM((TB, MIXP), F32),
        pltpu.VMEM((TB, HP), F32),
        pltpu.VMEM((TB, HP), F32),
        pltpu.VMEM((TB, HP), F32),
    ]
    return pl.pallas_call(
        kern, grid=(B, n_tb), in_specs=in_specs, out_specs=out_specs, out_shape=out_shape,
        scratch_shapes=scratch,
        compiler_params=pltpu.CompilerParams(dimension_semantics=("arbitrary", "arbitrary"),
                                             vmem_limit_bytes=VMEM_LIMIT),
        name=f"prefill_layer{layer}",
    )(x, win, wout, cos, sin, vec, hlb, lv)


DEC_KB = 2
KR_M = M_DH // DEC_KB
KR_H = H_DK // DEC_KB
G_DEC, G_WS, G_EN, G_GAM, G_SM, G_QN, G_SR, G_SH = (i * N_HEADS for i in range(8))


def _stream_state(q_blk, dec, kw_blk, v, in_ref, out_ref, nrows):
    acc = jnp.zeros_like(v)
    for kk in range(nrows):
        st = in_ref[0, 0, kk]
        d = dec if dec.shape[0] == 1 else dec[kk:kk + 1, :]
        acc = acc + q_blk[kk:kk + 1, :] * st
        out_ref[0, 0, kk] = d * st + kw_blk[kk:kk + 1, :] * v
    return acc


def _decode_kernel(x_ref, wt_ref, wout_ref, cos_ref, sin_ref, dvec_ref, dcol_ref, hlbt_ref, vec_ref,
                   cin_ref, nin_ref, min_ref, convin_ref, rin_ref, sin_st_ref,
                   y_ref, cout_ref, nout_ref, mout_ref, convout_ref, rout_ref, sout_ref,
                   p_ref, xs_ref, ht_ref, mix_ref, g_ref):
    l = pl.program_id(0)
    h = pl.program_id(1)
    kb = pl.program_id(2)
    first = jnp.logical_and(h == 0, kb == 0)
    last = jnp.logical_and(h == N_HEADS - 1, kb == DEC_KB - 1)
    B = x_ref.shape[0]

    @pl.when(jnp.logical_and(first, l == 0))
    def _load_x():
        xs_ref[...] = x_ref[...]

    @pl.when(first)
    def _project():
        p_ref[0:N_IN, :] = _dot_nt(wt_ref[0], xs_ref[...].astype(BF16))
        ht_ref[...] = jnp.zeros_like(ht_ref)

        pre = p_ref[O_MQ:O_MQ + 2 * M_WIDTH, :].T
        cw = dvec_ref[0, 0:CONV_W, :]
        acc = dvec_ref[0, CONV_W:CONV_W + 1, :] + cw[CONV_W - 1:CONV_W, :] * pre
        for jj in range(CONV_W - 1):
            acc = acc + cw[jj:jj + 1, :] * convin_ref[0, jj]
        for jj in range(CONV_W - 2):
            convout_ref[0, jj] = convin_ref[0, jj + 1]
        convout_ref[0, CONV_W - 2] = pre
        p_ref[O_MQ:O_MQ + 2 * M_WIDTH, :] = _silu(acc).T

        gt = p_ref[O_IF:O_IF + 2 * N_HEADS, :] + dcol_ref[0, D_MIX:D_MIX + 2 * N_HEADS, :]
        ig = gt[0:N_HEADS]
        lf = _log_sigmoid(gt[N_HEADS:2 * N_HEADS])
        m_prev = min_ref[0]
        m_t = jnp.maximum(ig, lf + m_prev)
        dec = jnp.exp(lf + m_prev - m_t)
        w_s = jnp.exp(ig - m_t)
        mout_ref[0] = m_t
        g_ref[G_DEC:G_DEC + N_HEADS, :] = dec
        g_ref[G_WS:G_WS + N_HEADS, :] = w_s
        g_ref[G_EN:G_EN + N_HEADS, :] = jnp.exp(-m_t)
        for hh in range(N_HEADS):
            q = p_ref[O_MQ + hh * M_DH:O_MQ + (hh + 1) * M_DH, :]
            k = p_ref[O_MK + hh * M_DH:O_MK + (hh + 1) * M_DH, :] * (M_DH ** -0.5)
            n_old = nin_ref[0, hh]
            d_h = dec[hh:hh + 1]
            kw = k * w_s[hh:hh + 1]
            g_ref[G_SM + hh:G_SM + hh + 1, :] = jnp.sum(q * k, axis=0, keepdims=True) * w_s[hh:hh + 1]
            g_ref[G_QN + hh:G_QN + hh + 1, :] = jnp.sum(q * n_old, axis=0, keepdims=True)
            nout_ref[0, hh] = d_h * n_old + kw
            p_ref[O_MK + hh * M_DH:O_MK + (hh + 1) * M_DH, :] = kw

        cos = cos_ref[...]
        sin = sin_ref[...]
        for hh in range(N_HEADS):
            g_ref[G_GAM + hh:G_GAM + hh + 1, :] = jnp.full((1, B), math.exp(LG[hh]), F32)
            rot = []
            for off, scale in ((O_RQ, 1.0), (O_RK, R_DH ** -0.5)):
                x1 = p_ref[off + hh * R_DH:off + hh * R_DH + ROPE_HALF, :]
                x2 = p_ref[off + hh * R_DH + ROPE_HALF:off + (hh + 1) * R_DH, :]
                r1 = (x1 * cos - x2 * sin) * scale
                r2 = (x1 * sin + x2 * cos) * scale
                p_ref[off + hh * R_DH:off + hh * R_DH + ROPE_HALF, :] = r1
                p_ref[off + hh * R_DH + ROPE_HALF:off + (hh + 1) * R_DH, :] = r2
                rot.append((r1, r2))
            (q1, q2), (k1, k2) = rot
            g_ref[G_SR + hh:G_SR + hh + 1, :] = (jnp.sum(q1 * k1, axis=0, keepdims=True)
                                                 + jnp.sum(q2 * k2, axis=0, keepdims=True))

        lbs = _hgrn_lower_bounds(hlbt_ref[...], 1)
        lb = lbs[0]
        for i in range(1, DEPTH):
            lb = jnp.where(l == i, lbs[i], lb)
        lfh, kh = _hgrn_gates(p_ref[O_HF:O_HF + H_WIDTH, :], lb)
        eh = jnp.exp(lfh)
        qh = p_ref[O_HQ:O_HQ + H_WIDTH, :]
        p_ref[O_HF:O_HF + H_WIDTH, :] = eh
        p_ref[O_HK:O_HK + H_WIDTH, :] = kh
        p_ref[O_HQ:O_HQ + H_WIDTH, :] = qh * eh
        qk = qh * kh
        for hh in range(N_HEADS):
            g_ref[G_SH + hh:G_SH + hh + 1, :] = jnp.sum(qk[hh * H_DK:(hh + 1) * H_DK], axis=0, keepdims=True)

    def rows(base, width, n):
        return pl.ds(pl.multiple_of(base + h * width + kb * n, SUBLANES), n)

    dec = g_ref[pl.ds(G_DEC + h, 1), :]
    acc = _stream_state(p_ref[rows(O_MQ, M_DH, KR_M), :], dec, p_ref[rows(O_MK, M_DH, KR_M), :],
                        p_ref[pl.ds(pl.multiple_of(O_MV + h * M_DH, SUBLANES), M_DH), :], cin_ref, cout_ref, KR_M)
    ht_ref[pl.ds(pl.multiple_of(h * M_DH, SUBLANES), M_DH), :] += acc

    gam = g_ref[pl.ds(G_GAM + h, 1), :]
    acc = _stream_state(p_ref[rows(O_RQ, R_DH, KR_M), :], gam, p_ref[rows(O_RK, R_DH, KR_M), :],
                        p_ref[pl.ds(pl.multiple_of(O_RV + h * R_DH, SUBLANES), R_DH), :], rin_ref, rout_ref, KR_M)
    ht_ref[pl.ds(pl.multiple_of(M_WIDTH + h * R_DH, SUBLANES), R_DH), :] += acc

    acc = _stream_state(p_ref[rows(O_HQ, H_DK, KR_H), :], p_ref[rows(O_HF, H_DK, KR_H), :],
                        p_ref[rows(O_HK, H_DK, KR_H), :],
                        p_ref[pl.ds(pl.multiple_of(O_HI + h * H_DK, SUBLANES), H_DK), :], sin_st_ref, sout_ref, KR_H)
    ht_ref[pl.ds(pl.multiple_of(M_WIDTH + R_WIDTH + h * H_DK, SUBLANES), H_DK), :] += acc

    @pl.when(last)
    def _finish():
        def head_ln(a):
            mu = jnp.mean(a, axis=0, keepdims=True)
            d = a - mu
            return d * lax.rsqrt(jnp.mean(d * d, axis=0, keepdims=True) + HEAD_EPS)

        for hh in range(N_HEADS):
            sl = slice(hh * M_DH, (hh + 1) * M_DH)
            v = p_ref[O_MV + hh * M_DH:O_MV + (hh + 1) * M_DH, :]
            d_h = g_ref[G_DEC + hh:G_DEC + hh + 1, :]
            s = g_ref[G_SM + hh:G_SM + hh + 1, :]
            num = s * v + ht_ref[sl, :] * d_h
            den = s + g_ref[G_QN + hh:G_QN + hh + 1, :] * d_h
            hm = num / jnp.maximum(jnp.abs(den), g_ref[G_EN + hh:G_EN + hh + 1, :])
            o = p_ref[O_MO + hh * M_DH:O_MO + (hh + 1) * M_DH, :]
            z = p_ref[O_MZ + hh * M_DH:O_MZ + (hh + 1) * M_DH, :]
            mix_ref[sl, :] = head_ln(hm) * dcol_ref[0, sl, :] * _sigmoid(o) * _silu(z)
        for hh in range(N_HEADS):
            sl = slice(M_WIDTH + hh * R_DH, M_WIDTH + (hh + 1) * R_DH)
            v = p_ref[O_RV + hh * R_DH:O_RV + (hh + 1) * R_DH, :]
            hr = g_ref[G_SR + hh:G_SR + hh + 1, :] * v + ht_ref[sl, :] * math.exp(LG[hh])
            gg = p_ref[O_RG + hh * R_DH:O_RG + (hh + 1) * R_DH, :]
            mix_ref[sl, :] = head_ln(hr) * dcol_ref[0, sl, :] * _silu(gg)
        for hh in range(N_HEADS):
            sl = slice(M_WIDTH + R_WIDTH + hh * H_DK, M_WIDTH + R_WIDTH + (hh + 1) * H_DK)
            v = p_ref[O_HI + hh * H_DK:O_HI + (hh + 1) * H_DK, :]
            ho = g_ref[G_SH + hh:G_SH + hh + 1, :] * v + ht_ref[sl, :]
            hn = ho * lax.rsqrt(jnp.mean(ho * ho, axis=0, keepdims=True) + HEAD_EPS)
            gg = p_ref[O_HG + hh * H_DK:O_HG + (hh + 1) * H_DK, :]
            mix_ref[sl, :] = hn * dcol_ref[0, sl, :] * _silu(gg)
        x = xs_ref[...]
        out = _dot(mix_ref[...].T.astype(BF16), wout_ref[0])
        y = _layer_norm_rows(ALPHA * x + out, vec_ref[0, V_LNG:V_LNG + 1, :], vec_ref[0, V_LNB:V_LNB + 1, :])
        xs_ref[...] = y
        y_ref[0] = y


def _decode(x, wt, wout, cos, sin, dvec, dcol, hlbt, vec, st_c, st_n, st_m, st_conv, st_r, st_s):
    B = x.shape[0]

    def lspec(a, single_buffer=False):
        nd = a.ndim
        kw = dict(pipeline_mode=pl.Buffered(1)) if single_buffer else {}
        return pl.BlockSpec((1,) + a.shape[1:], lambda l, h, kb: (l,) + (0,) * (nd - 1), **kw)

    def kv_spec(a, n):
        return pl.BlockSpec((1, 1, n) + a.shape[3:], lambda l, h, kb: (l, h, kb, 0, 0))

    in_specs = [
        _full_spec(x.shape), lspec(wt, True), lspec(wout, True), _full_spec(cos.shape), _full_spec(sin.shape),
        lspec(dvec), lspec(dcol), _full_spec(hlbt.shape), lspec(vec),
        kv_spec(st_c, KR_M), lspec(st_n), lspec(st_m), lspec(st_conv), kv_spec(st_r, KR_M), kv_spec(st_s, KR_H),
    ]
    out_shape = (
        jax.ShapeDtypeStruct((DEPTH, B, D_MODEL), F32),
        jax.ShapeDtypeStruct(st_c.shape, F32), jax.ShapeDtypeStruct(st_n.shape, F32),
        jax.ShapeDtypeStruct(st_m.shape, F32), jax.ShapeDtypeStruct(st_conv.shape, F32),
        jax.ShapeDtypeStruct(st_r.shape, F32), jax.ShapeDtypeStruct(st_s.shape, F32),
    )
    out_specs = (
        pl.BlockSpec((1, B, D_MODEL), lambda l, h, kb: (l, 0, 0)),
        kv_spec(st_c, KR_M), lspec(st_n), lspec(st_m), lspec(st_conv), kv_spec(st_r, KR_M), kv_spec(st_s, KR_H),
    )
    scratch = [
        pltpu.VMEM((N_IN + H_WIDTH, B), F32),
        pltpu.VMEM((B, D_MODEL), F32),
        pltpu.VMEM((D_MIX, B), F32),
        pltpu.VMEM((D_MIX, B), F32),
        pltpu.VMEM((8 * N_HEADS, B), F32),
    ]
    return pl.pallas_call(
        _decode_kernel, grid=(DEPTH, N_HEADS, DEC_KB), in_specs=in_specs, out_specs=out_specs,
        out_shape=out_shape, scratch_shapes=scratch,
        compiler_params=pltpu.CompilerParams(dimension_semantics=("arbitrary", "arbitrary", "arbitrary"),
                                             vmem_limit_bytes=VMEM_LIMIT),
        name="decode_step",
    )(x, wt, wout, cos, sin, dvec, dcol, hlbt, vec, st_c, st_n, st_m, st_conv, st_r, st_s)


def _rope_angles(pos):
    inv = ROPE_BASE ** (-jnp.arange(ROPE_HALF, dtype=F32) / ROPE_HALF)
    ang = pos.astype(F32)[:, None] * inv[None, :]
    return jnp.cos(ang), jnp.sin(ang)


PREFILL_BLOCK = 512


def kernel(x_prompt, x_sample, state_mlstm_C, state_mlstm_n, state_mlstm_m, state_mlstm_conv, state_ret, state_hgrn, w_in, conv_w, conv_b, b_mgate, m_norm_w, r_norm_w, h_norm_w, hgrn_lb, w_out, ln_g, ln_b):
    B, T, _ = x_prompt.shape

    wt = jnp.swapaxes(w_in, 1, 2).astype(BF16)
    win_p = jnp.concatenate(
        [_pad_heads(wt[:, O_MQ:O_IF], 1),
         _pad_rope_heads(wt[:, O_RQ:O_RV], 1), _pad_heads(wt[:, O_RV:O_HF], 1),
         wt[:, O_HF:N_IN],
         jnp.pad(wt[:, O_IF:O_RQ], ((0, 0), (0, LANES - 2 * N_HEADS), (0, 0)))], axis=1)
    wout_b = w_out.astype(BF16)
    wout_p = jnp.concatenate([_pad_heads(wout_b[:, 0:M_WIDTH + R_WIDTH], 1), wout_b[:, M_WIDTH + R_WIDTH:]],
                             axis=1)

    def row(a):
        return jnp.pad(a, ((0, 0), (0, D_MODEL - a.shape[-1])))[:, None, :]

    vec = jnp.concatenate([_pad_heads(conv_w, 2), row(_pad_heads(conv_b, 1)), row(b_mgate),
                           row(_pad_heads(m_norm_w, 1)), row(_pad_heads(r_norm_w, 1)), row(h_norm_w),
                           row(ln_g), row(ln_b)], axis=1)
    vec = jnp.pad(vec, ((0, 0), (0, V_ROWS - vec.shape[1]), (0, 0)))
    vec = jnp.concatenate([vec, jnp.repeat(vec[:, V_CONVW:V_CONVB + 1], SUBLANES, axis=1)], axis=1)
    dvec = jnp.pad(jnp.concatenate([conv_w, conv_b[:, None, :]], axis=1),
                   ((0, 0), (0, SUBLANES - CONV_W - 1), (0, 0)))
    dcol = jnp.concatenate([m_norm_w, r_norm_w, h_norm_w, b_mgate], axis=1)[:, :, None]
    hlb = hgrn_lb.astype(F32)

    lv = jnp.asarray(_level_table(CHUNK))

    c, s = _rope_angles(jnp.arange(T, dtype=jnp.int32))
    z = jnp.zeros((T, LANES // 2 - ROPE_HALF), F32)
    cos_p = jnp.concatenate([c, z, c, z], axis=1)
    sin_p = jnp.concatenate([-s, z, s, z], axis=1)

    hp = x_prompt
    st = [[] for _ in range(6)]
    for l in range(DEPTH):
        outs = _prefill_layer(l, hp, win_p, wout_p, cos_p, sin_p, vec, hlb, lv, PREFILL_BLOCK)
        hp = outs[0]
        for k in range(6):
            st[k].append(outs[1 + k])
    mC_p = jnp.stack(st[0])
    mn_p = jnp.stack(st[1])
    mm_p = jnp.stack(st[2])[:, :, 0, :]
    conv_p = _unpad_heads(jnp.stack(st[3]))
    ret_p = jnp.stack(st[4])
    hgrn_p = jnp.stack(st[5])

    n_s = x_sample.shape[0]
    cs, ss = _rope_angles(PAST_LEN + jnp.arange(x_sample.shape[1], dtype=jnp.int32))
    cos_s = jnp.broadcast_to(cs[0][:, None], (ROPE_HALF, n_s))
    sin_s = jnp.broadcast_to(ss[0][:, None], (ROPE_HALF, n_s))
    ys, c_t, n_t, m_t, conv_t, r_t, s_t = _decode(
        x_sample[:, 0, :], wt, wout_b, cos_s, sin_s, dvec, dcol, hlb.T, vec,
        jnp.transpose(state_mlstm_C, (0, 2, 3, 4, 1)), jnp.transpose(state_mlstm_n, (0, 2, 3, 1)),
        jnp.transpose(state_mlstm_m, (0, 2, 1)), jnp.transpose(state_mlstm_conv, (0, 2, 1, 3)),
        jnp.transpose(state_ret, (0, 2, 3, 4, 1)), jnp.transpose(state_hgrn, (0, 2, 3, 4, 1)))
    hs = ys[DEPTH - 1][:, None, :]
    mC_s = jnp.transpose(c_t, (0, 4, 1, 2, 3))
    mn_s = jnp.transpose(n_t, (0, 3, 1, 2))
    mm_s = jnp.transpose(m_t, (0, 2, 1))
    conv_s = jnp.transpose(conv_t, (0, 2, 1, 3))
    ret_s = jnp.transpose(r_t, (0, 4, 1, 2, 3))
    hgrn_s = jnp.transpose(s_t, (0, 4, 1, 2, 3))

    return (hp, hs, mC_p, mn_p, mm_p, conv_p, ret_p, hgrn_p, mC_s, mn_s, mm_s, conv_s, ret_s, hgrn_s)
```
